```python
import jax
import jax.numpy as jnp
from jax import lax
import numpy as np

D_MODEL = 4096
BATCH = 2
SEQ = 8192
DEPTH = 1
DEC_BATCH = 32
DEC_SEQ = 64
PAST_LEN = 2048

CHUNK = 64
N_Q_HEADS = 64
N_KV_HEADS = 8
HEAD_DIM = 64
GROUP = N_Q_HEADS // N_KV_HEADS
WINDOW = 128
WIN_CHUNKS = WINDOW // CHUNK
N_BAND = WIN_CHUNKS + 1
ROT_DIM = HEAD_DIM // 4
ROPE_THETA = 500000.0
M_HEADS = 8
M_QK_DIM = 256
M_V_DIM = 512
ATT_WIDTH = N_Q_HEADS * HEAD_DIM
KV_WIDTH = N_KV_HEADS * HEAD_DIM
M_QK_WIDTH = M_HEADS * M_QK_DIM
M_V_WIDTH = M_HEADS * M_V_DIM
PROJ_SIZES = (ATT_WIDTH, KV_WIDTH, KV_WIDTH, M_QK_WIDTH, M_QK_WIDTH, M_V_WIDTH, 2 * M_HEADS, M_V_WIDTH, D_MODEL, D_MODEL)
PROJ_WIDTH = sum(PROJ_SIZES)
D_FF = -(-8 * D_MODEL // (3 * 256)) * 256
ALPHA = (2.0 * DEPTH) ** 0.25
BETA = (8.0 * DEPTH) ** -0.25
LN_EPS = 1e-5
M_INIT = -1e30
F32 = jnp.float32

kernel_name = 'hybrid_swa_sink_mlstm_stream_step'


def _layer_norm(x, g, b):
    xf = x.astype(F32)
    mu = jnp.mean(xf, -1, keepdims=True)
    var = jnp.mean(jnp.square(xf - mu), -1, keepdims=True)
    return ((xf - mu) * lax.rsqrt(var + LN_EPS) * g.astype(F32) + b.astype(F32)).astype(x.dtype)


def _adaln(c, w_ada, b_ada):
    mod = jnp.einsum('bd,de->be', jax.nn.silu(c), w_ada) + b_ada
    return jnp.split(mod[:, None, :], 6, axis=-1)


def _partial_rope(x, pos):
    half = ROT_DIM // 2
    inv = ROPE_THETA ** (-2.0 * jnp.arange(half, dtype=F32) / ROT_DIM)
    ang = pos.astype(F32)[:, None] * inv[None, :]
    cos = jnp.cos(ang)[:, None, :]
    sin = jnp.sin(ang)[:, None, :]
    xf = x.astype(F32)
    x1, x2, rest = xf[..., :half], xf[..., half:ROT_DIM], xf[..., ROT_DIM:]
    return jnp.concatenate([x1 * cos - x2 * sin, x2 * cos + x1 * sin, rest], -1).astype(x.dtype)


def _project(u, w_in, b_if):
    B, T, _ = u.shape
    z = jnp.einsum('btd,dp->btp', u, w_in)
    pts = []
    acc = 0
    for s in PROJ_SIZES[:-1]:
        acc += s
        pts.append(acc)
    qa, ka, va, qm, km, vm, g_if, o_m, g_a, g_b = jnp.split(z, pts, axis=-1)
    qa = qa.reshape(B, T, N_Q_HEADS, HEAD_DIM)
    ka = ka.reshape(B, T, N_KV_HEADS, HEAD_DIM)
    va = va.reshape(B, T, N_KV_HEADS, HEAD_DIM)
    qm = qm.reshape(B, T, M_HEADS, M_QK_DIM) * (M_QK_DIM ** -0.5)
    km = km.reshape(B, T, M_HEADS, M_QK_DIM)
    vm = vm.reshape(B, T, M_HEADS, M_V_DIM)
    g_if = (g_if + b_if).astype(F32)
    li = g_if[..., :M_HEADS]
    lf = jax.nn.log_sigmoid(g_if[..., M_HEADS:])
    return (qa, ka, va), (qm, km, vm, li, lf), (o_m, g_a, g_b)


def _sink_softmax(s, sinks):
    sink = sinks.astype(F32).reshape(N_KV_HEADS, GROUP)[:, :, None, None]
    m = jnp.maximum(jnp.max(s, -1, keepdims=True), sink)
    p = jnp.exp(s - m)
    return p / (jnp.sum(p, -1, keepdims=True) + jnp.exp(sink - m))


def _swa_prompt(q, k, v, sinks):
    B, S = q.shape[:2]
    nc = S // CHUNK
    qc = q.reshape(B, nc, CHUNK, N_KV_HEADS, GROUP, HEAD_DIM)
    padw = ((0, 0), (WIN_CHUNKS * CHUNK, 0), (0, 0), (0, 0))
    kp = jnp.pad(k, padw).reshape(B, nc + WIN_CHUNKS, CHUNK, N_KV_HEADS, HEAD_DIM)
    vp = jnp.pad(v, padw).reshape(B, nc + WIN_CHUNKS, CHUNK, N_KV_HEADS, HEAD_DIM)
    kb = jnp.concatenate([kp[:, j:j + nc] for j in range(N_BAND)], axis=2)
    vb = jnp.concatenate([vp[:, j:j + nc] for j in range(N_BAND)], axis=2)
    key_chunk = jnp.arange(nc)[:, None] + (jnp.arange(N_BAND * CHUNK) // CHUNK)[None, :] - WIN_CHUNKS
    valid = key_chunk >= 0
    s = jnp.einsum('bnqhgd,bnkhd->bnhgqk', qc, kb).astype(F32) * (HEAD_DIM ** -0.5)
    s = jnp.where(valid[None, :, None, None, None, :], s, -jnp.inf)
    p = _sink_softmax(s, sinks)
    o = jnp.einsum('bnhgqk,bnkhd->bnqhgd', p, vb.astype(F32))
    return o.reshape(B, S, ATT_WIDTH).astype(q.dtype)


def _swa_sample(q, k, v, ck, cv, sinks):
    B, T = q.shape[:2]
    kk = jnp.concatenate([ck.astype(k.dtype), k], axis=1)
    vv = jnp.concatenate([cv.astype(v.dtype), v], axis=1)
    qg = q.reshape(B, T, N_KV_HEADS, GROUP, HEAD_DIM)
    s = jnp.einsum('bqhgd,bkhd->bhgqk', qg, kk).astype(F32) * (HEAD_DIM ** -0.5)
    p = _sink_softmax(s, sinks)
    o = jnp.einsum('bhgqk,bkhd->bqhgd', p, vv.astype(F32)).reshape(B, T, ATT_WIDTH).astype(q.dtype)
    keep = ck.shape[1]
    return o, kk[:, -keep:], vv[:, -keep:]


def _mlstm_chunk(carry, inp):
    C, n, m = carry
    q, k, v, li, lf = inp
    L = q.shape[1]
    b = jnp.cumsum(lf, axis=1)
    dlog = b[:, :, None, :] - b[:, None, :, :] + li[:, None, :, :]
    causal = jnp.tril(jnp.ones((L, L), dtype=bool))
    dlog = jnp.where(causal[None, :, :, None], dlog, -jnp.inf)
    inter = b + m[:, None, :]
    m_t = jnp.maximum(inter, jnp.max(dlog, axis=2))
    w_intra = jnp.exp(dlog - m_t[:, :, None, :])
    w_inter = jnp.exp(inter - m_t)
    s = jnp.einsum('bthd,bshd->btsh', q, k) * w_intra
    num = jnp.einsum('btsh,bshv->bthv', s, v) + w_inter[..., None] * jnp.einsum('bhvd,bthd->bthv', C, q)
    den = jnp.sum(s, axis=2) + w_inter * jnp.einsum('bhd,bthd->bth', n, q)
    h = num / jnp.maximum(jnp.abs(den), jnp.exp(-m_t))[..., None]
    b_last = b[:, -1]
    m_new = m_t[:, -1]
    w_s = jnp.exp(b_last[:, None, :] - b + li - m_new[:, None, :])
    decay = jnp.exp(b_last + m - m_new)
    C_new = decay[..., None, None] * C + jnp.einsum('bsh,bshv,bshd->bhvd', w_s, v, k)
    n_new = decay[..., None] * n + jnp.einsum('bsh,bshd->bhd', w_s, k)
    return (C_new, n_new, m_new), h


def _mlstm_prompt(q, k, v, li, lf):
    B, S = q.shape[:2]
    nc = S // CHUNK

    def to_chunks(a):
        return jnp.moveaxis(a.astype(F32).reshape(B, nc, CHUNK, *a.shape[2:]), 1, 0)

    init = (jnp.zeros((B, M_HEADS, M_V_DIM, M_QK_DIM), F32),
            jnp.zeros((B, M_HEADS, M_QK_DIM), F32),
            jnp.full((B, M_HEADS), M_INIT, F32))
    carry, h = lax.scan(_mlstm_chunk, init, (to_chunks(q), to_chunks(k), to_chunks(v), to_chunks(li), to_chunks(lf)))
    h = jnp.moveaxis(h, 0, 1).reshape(B, S, M_V_WIDTH)
    return h, carry


def _merge(ya, yb, o_m, g_a, g_b, w_up_a, w_up_b, w_o):
    yb = jax.nn.sigmoid(o_m) * yb
    branch_a = jnp.einsum('bte,ed->btd', ya, w_up_a)
    branch_b = jnp.einsum('bte,ed->btd', yb, w_up_b)
    merged = jax.nn.sigmoid(g_a) * branch_a + jax.nn.sigmoid(g_b) * branch_b
    return jnp.einsum('btd,de->bte', merged, w_o)


def _swiglu(u, w_in, w_out):
    gate, up = jnp.split(jnp.einsum('btd,df->btf', u, w_in), 2, axis=-1)
    return jnp.einsum('btf,fd->btd', jax.nn.silu(gate) * up, w_out)


def setup_inputs(seed: int = 0) -> dict:
    key = jax.random.key(seed)
    ks = jax.random.split(key, 24)
    D = D_MODEL

    def nrm(k, shape, scale):
        return jax.random.normal(k, shape, F32) * scale

    keep = min(WINDOW, PAST_LEN)
    b_i = nrm(ks[12], (DEPTH, M_HEADS), 0.1)
    b_f = jnp.linspace(3.0, 6.0, M_HEADS, dtype=F32)[None, :] + nrm(ks[13], (DEPTH, M_HEADS), 0.1)
    return {
        'x_prompt': nrm(ks[0], (BATCH, SEQ, D), 1.0),
        'x_sample': nrm(ks[1], (DEC_BATCH, DEC_SEQ, D), 1.0),
        'cache_k_win': nrm(ks[2], (DEPTH, DEC_BATCH, keep, N_KV_HEADS, HEAD_DIM), 1.0),
        'cache_v_win': nrm(ks[3], (DEPTH, DEC_BATCH, keep, N_KV_HEADS, HEAD_DIM), 1.0),
        'state_C': nrm(ks[4], (DEPTH, DEC_BATCH, M_HEADS, M_V_DIM, M_QK_DIM), 0.5),
        'state_n': nrm(ks[5], (DEPTH, DEC_BATCH, M_HEADS, M_QK_DIM), 0.5),
        'state_m': nrm(ks[6], (DEPTH, DEC_BATCH, M_HEADS), 1.0),
        'c_prompt': nrm(ks[7], (BATCH, D), 1.0),
        'c_sample': nrm(ks[8], (DEC_BATCH, D), 1.0),
        'w_ada': nrm(ks[9], (DEPTH, D, 6 * D), 0.5 * D ** -0.5),
        'b_ada': nrm(ks[10], (DEPTH, 6 * D), 0.02),
        'w_in': nrm(ks[11], (DEPTH, D, PROJ_WIDTH), D ** -0.5),
        'b_if': jnp.concatenate([b_i, b_f], axis=-1),
        'attn_sinks': nrm(ks[14], (DEPTH, N_Q_HEADS), 0.5),
        'w_up_a': nrm(ks[15], (DEPTH, ATT_WIDTH, D), BETA * ATT_WIDTH ** -0.5),
        'w_up_b': nrm(ks[16], (DEPTH, M_V_WIDTH, D), BETA * M_V_WIDTH ** -0.5),
        'w_o': nrm(ks[17], (DEPTH, D, D), BETA * D ** -0.5),
        'ln1_g': 1.0 + nrm(ks[18], (DEPTH, D), 0.02),
        'ln1_b': nrm(ks[19], (DEPTH, D), 0.02),
        'w_ffn_in': nrm(ks[20], (DEPTH, D, 2 * D_FF), D ** -0.5),
        'w_ffn_out': nrm(ks[21], (DEPTH, D_FF, D), BETA * D_FF ** -0.5),
        'ln2_g': 1.0 + nrm(ks[22], (DEPTH, D), 0.02),
        'ln2_b': nrm(ks[23], (DEPTH, D), 0.02),
    }


def reference(x_prompt, x_sample, cache_k_win, cache_v_win, state_C, state_n, state_m,
              c_prompt, c_sample, w_ada, b_ada, w_in, b_if, attn_sinks, w_up_a, w_up_b, w_o,
              ln1_g, ln1_b, w_ffn_in, w_ffn_out, ln2_g, ln2_b):
    xp = x_prompt
    xs = x_sample
    S = xp.shape[1]
    T = xs.shape[1]
    pos_p = jnp.arange(S)
    pos_s = PAST_LEN + jnp.arange(T)
    kwp, vwp, Cp, np_, mp = [], [], [], [], []
    kws, vws, Cs, ns, ms = [], [], [], [], []
    for l in range(DEPTH):
        sh1, sc1, gt1, sh2, sc2, gt2 = _adaln(c_prompt, w_ada[l], b_ada[l])
        u = xp * (1 + sc1) + sh1
        (qa, ka, va), (qm, km, vm, li, lf), (o_m, g_a, g_b) = _project(u, w_in[l], b_if[l])
        qa = _partial_rope(qa, pos_p)
        ka = _partial_rope(ka, pos_p)
        ya = _swa_prompt(qa, ka, va, attn_sinks[l])
        yb, (C_f, n_f, m_f) = _mlstm_prompt(qm, km, vm, li, lf)
        mix = _merge(ya, yb.astype(xp.dtype), o_m, g_a, g_b, w_up_a[l], w_up_b[l], w_o[l])
        xp = _layer_norm(ALPHA * xp + gt1 * mix, ln1_g[l], ln1_b[l])
        xp = _layer_norm(ALPHA * xp + gt2 * _swiglu(xp * (1 + sc2) + sh2, w_ffn_in[l], w_ffn_out[l]), ln2_g[l], ln2_b[l])
        keep = cache_k_win.shape[2]
        kwp.append(ka[:, -keep:])
        vwp.append(va[:, -keep:])
        Cp.append(C_f.astype(xp.dtype))
        np_.append(n_f.astype(xp.dtype))
        mp.append(m_f.astype(xp.dtype))
        sh1, sc1, gt1, sh2, sc2, gt2 = _adaln(c_sample, w_ada[l], b_ada[l])
        u = xs * (1 + sc1) + sh1
        (qa, ka, va), (qm, km, vm, li, lf), (o_m, g_a, g_b) = _project(u, w_in[l], b_if[l])
        qa = _partial_rope(qa, pos_s)
        ka = _partial_rope(ka, pos_s)
        ya, k_new, v_new = _swa_sample(qa, ka, va, cache_k_win[l], cache_v_win[l], attn_sinks[l])
        carry0 = (state_C[l].astype(F32), state_n[l].astype(F32), state_m[l].astype(F32))
        (C_u, n_u, m_u), hb = _mlstm_chunk(carry0, (qm.astype(F32), km.astype(F32), vm.astype(F32), li, lf))
        yb = hb.reshape(xs.shape[0], T, M_V_WIDTH).astype(xs.dtype)
        mix = _merge(ya, yb, o_m, g_a, g_b, w_up_a[l], w_up_b[l], w_o[l])
        xs = _layer_norm(ALPHA * xs + gt1 * mix, ln1_g[l], ln1_b[l])
        xs = _layer_norm(ALPHA * xs + gt2 * _swiglu(xs * (1 + sc2) + sh2, w_ffn_in[l], w_ffn_out[l]), ln2_g[l], ln2_b[l])
        kws.append(k_new)
        vws.append(v_new)
        Cs.append(C_u.astype(xs.dtype))
        ns.append(n_u.astype(xs.dtype))
        ms.append(m_u.astype(xs.dtype))
    return (xp, xs,
            jnp.stack(kwp), jnp.stack(vwp), jnp.stack(Cp), jnp.stack(np_), jnp.stack(mp),
            jnp.stack(kws), jnp.stack(vws), jnp.stack(Cs), jnp.stack(ns), jnp.stack(ms))
```

```python
import functools

import jax
import jax.numpy as jnp
from jax import lax
from jax.experimental import pallas as pl
from jax.experimental.pallas import tpu as pltpu

F32 = jnp.float32
BF16 = jnp.bfloat16

D_MODEL = 4096
CHUNK = 64
N_Q_HEADS = 64
N_KV_HEADS = 8
HEAD_DIM = 64
GROUP = N_Q_HEADS // N_KV_HEADS
WINDOW = 128
ROT_DIM = HEAD_DIM // 4
ROPE_THETA = 500000.0
M_HEADS = 8
M_QK_DIM = 256
M_V_DIM = 512
ATT_WIDTH = N_Q_HEADS * HEAD_DIM
KV_WIDTH = N_KV_HEADS * HEAD_DIM
M_QK_WIDTH = M_HEADS * M_QK_DIM
M_V_WIDTH = M_HEADS * M_V_DIM
D_FF = 11008
DEPTH = 1
PAST_LEN = 2048
ALPHA = (2.0 * DEPTH) ** 0.25
LN_EPS = 1e-5
M_INIT = -1e30

LANES = 128
ROW_TILE = 1024
COL_TILE = 1024
D_FF_PAD = 11264
GATE_PAD = 2 * LANES
VMEM_LIMIT = 60 * 1024 * 1024


def _params(semantics):
    return pltpu.CompilerParams(dimension_semantics=semantics, vmem_limit_bytes=VMEM_LIMIT)


def _row_blocks(batch, steps, rows):
    if steps >= rows:
        assert steps % rows == 0
        return 1, rows
    bb = min(batch, rows // steps)
    assert batch % bb == 0
    return bb, steps


def _adaln_body(c_ref, w_ref, b_ref, o_ref):
    c = c_ref[...]
    s = c * jax.nn.sigmoid(c)
    o_ref[...] = jnp.dot(s, w_ref[...], preferred_element_type=F32) + b_ref[...]


def _adaln(c, w_ada, b_ada):
    rows, d = c.shape
    n = w_ada.shape[1]
    tn = 512
    return pl.pallas_call(
        _adaln_body,
        grid=(n // tn,),
        in_specs=[pl.BlockSpec((rows, d), lambda j: (0, 0)),
                  pl.BlockSpec((d, tn), lambda j: (0, j)),
                  pl.BlockSpec((1, tn), lambda j: (0, j))],
        out_specs=pl.BlockSpec((rows, tn), lambda j: (0, j)),
        out_shape=jax.ShapeDtypeStruct((rows, n), F32),
        compiler_params=_params(("parallel",)),
        name="adaln",
    )(c, w_ada, b_ada.reshape(1, n))


def _modulate_body(x_ref, sc_ref, sh_ref, o_ref):
    o_ref[...] = (x_ref[...] * (1.0 + sc_ref[...]) + sh_ref[...]).astype(o_ref.dtype)


def _modulate(x, mod, sc_idx, sh_idx):
    batch, steps, d = x.shape
    bb, tt = _row_blocks(batch, steps, 512)
    return pl.pallas_call(
        _modulate_body,
        grid=(batch // bb, steps // tt),
        in_specs=[pl.BlockSpec((bb, tt, d), lambda b, t: (b, t, 0)),
                  pl.BlockSpec((bb, 1, d), lambda b, t: (b, 0, sc_idx)),
                  pl.BlockSpec((bb, 1, d), lambda b, t: (b, 0, sh_idx))],
        out_specs=pl.BlockSpec((bb, tt, d), lambda b, t: (b, t, 0)),
        out_shape=jax.ShapeDtypeStruct((batch, steps, d), BF16),
        compiler_params=_params(("parallel", "parallel")),
        name="modulate",
    )(x, mod, mod)


def _load_rows(a_ref):
    bb, tt, k = a_ref.shape
    return a_ref[...].reshape(bb * tt, k)


def _rope_rows(x, cos_ref, sa_ref, sb_ref, bb):
    rows, n = x.shape
    tt = rows // bb

    def table(ref):
        t = ref[...]
        if bb == 1:
            return t
        return jnp.broadcast_to(t[None], (bb, tt, LANES)).reshape(rows, LANES)

    cos, sa, sb = table(cos_ref), table(sa_ref), table(sb_ref)
    outs = []
    for s in range(n // LANES):
        xs = x[:, s * LANES:(s + 1) * LANES]
        up = pltpu.roll(xs, LANES - ROT_DIM // 2, axis=1)
        dn = pltpu.roll(xs, ROT_DIM // 2, axis=1)
        outs.append(xs * cos + up * sa + dn * sb)
    return jnp.concatenate(outs, axis=1)


def _mm_plain_body(a_ref, w_ref, o_ref):
    acc = jnp.dot(_load_rows(a_ref), w_ref[...], preferred_element_type=F32)
    o_ref[...] = acc.reshape(o_ref.shape).astype(o_ref.dtype)


def _mm_rope_body(a_ref, w_ref, cos_ref, sa_ref, sb_ref, o_ref):
    acc = jnp.dot(_load_rows(a_ref), w_ref[...], preferred_element_type=F32)
    out = _rope_rows(acc, cos_ref, sa_ref, sb_ref, a_ref.shape[0])
    o_ref[...] = out.reshape(o_ref.shape).astype(o_ref.dtype)


def _mm_kvg_body(a_ref, w_ref, cos_ref, sa_ref, sb_ref, k_ref, v_ref, g_ref):
    acc = jnp.dot(_load_rows(a_ref), w_ref[...], preferred_element_type=F32)
    k = _rope_rows(acc[:, :KV_WIDTH], cos_ref, sa_ref, sb_ref, a_ref.shape[0])
    k_ref[...] = k.reshape(k_ref.shape)
    v_ref[...] = acc[:, KV_WIDTH:2 * KV_WIDTH].reshape(v_ref.shape)
    g_ref[...] = acc[:, 2 * KV_WIDTH:].reshape(g_ref.shape)


def _mm_gate_first_body(a_ref, w_ref, g_ref, o_ref):
    acc = jnp.dot(_load_rows(a_ref), w_ref[...], preferred_element_type=F32)
    gate = jax.nn.sigmoid(g_ref[...].astype(F32))
    o_ref[...] = gate * acc.reshape(o_ref.shape)


def _mm_gate_second_body(a_ref, w_ref, g_ref, t_ref, o_ref):
    acc = jnp.dot(_load_rows(a_ref), w_ref[...], preferred_element_type=F32)
    gate = jax.nn.sigmoid(g_ref[...].astype(F32))
    o_ref[...] = (t_ref[...] + gate * acc.reshape(o_ref.shape)).astype(o_ref.dtype)


def _mm_residual_body(a_ref, w_ref, x_ref, g_ref, o_ref):
    acc = jnp.dot(_load_rows(a_ref), w_ref[...], preferred_element_type=F32)
    o_ref[...] = ALPHA * x_ref[...] + g_ref[...] * acc.reshape(o_ref.shape)


def _mm_swiglu_body(a_ref, wg_ref, wu_ref, o_ref):
    a = _load_rows(a_ref)
    gate = jnp.dot(a, wg_ref[...], preferred_element_type=F32)
    up = jnp.dot(a, wu_ref[...], preferred_element_type=F32)
    out = gate * jax.nn.sigmoid(gate) * up
    o_ref[...] = out.reshape(o_ref.shape).astype(o_ref.dtype)


def _act_spec(bb, tt, width, col=None):
    if col is None:
        return pl.BlockSpec((bb, tt, width), lambda b, t, j: (b, t, j))
    return pl.BlockSpec((bb, tt, width), lambda b, t, j: (b, t, j + col))


def _matmul(body, a, weights, extras, extra_specs, outs, *, rows=ROW_TILE, tn=COL_TILE, name):
    batch, steps, k = a.shape
    n = weights[0].shape[1]
    bb, tt = _row_blocks(batch, steps, rows)
    grid = (batch // bb, steps // tt, n // tn)
    in_specs = [pl.BlockSpec((bb, tt, k), lambda b, t, j: (b, t, 0))]
    in_specs += [pl.BlockSpec((k, tn), lambda b, t, j: (0, j)) for _ in weights]
    in_specs += [spec(bb, tt) for spec in extra_specs]
    out_specs = [pl.BlockSpec((bb, tt, cols_tile), lambda b, t, j: (b, t, j)) for _, _, cols_tile in outs]
    out_shape = [jax.ShapeDtypeStruct((batch, steps, cols), dt) for cols, dt, _ in outs]
    single = len(outs) == 1
    return pl.pallas_call(
        body,
        grid=grid,
        in_specs=in_specs,
        out_specs=out_specs[0] if single else out_specs,
        out_shape=out_shape[0] if single else out_shape,
        compiler_params=_params(("parallel", "parallel", "parallel")),
        name=name,
    )(a, *weights, *extras)


def _rope_specs():
    return [lambda bb, tt: pl.BlockSpec((tt, LANES), lambda b, t, j: (t, 0))] * 3


def _attn_body(q_ref, kc_ref, kp_ref, vc_ref, vp_ref, sink_ref, o_ref, *, mask_first):
    tq = q_ref.shape[1]
    n_chunks = tq // CHUNK
    n_keys = WINDOW + CHUNK
    i = pl.program_id(1)
    h = pl.program_id(2)
    lane = lax.broadcasted_iota(jnp.int32, (1, LANES), 1)
    keep = (lane // HEAD_DIM) == (h % 2)

    def both_halves(cur_ref, prev_ref):
        pair = jnp.concatenate([prev_ref[0].astype(F32), cur_ref[0].astype(F32)], axis=0)
        swapped = pltpu.roll(pair, HEAD_DIM, axis=1)
        return jnp.where(keep, pair, swapped)

    k2 = both_halves(kc_ref, kp_ref)
    v2 = both_halves(vc_ref, vp_ref)
    k4 = jnp.concatenate([k2, k2], axis=1).astype(BF16)
    vext = jnp.concatenate([v2, jnp.ones_like(v2)], axis=1).astype(BF16)

    sink = sink_ref[0][:, 0:1]
    lane_head = lax.broadcasted_iota(jnp.int32, (CHUNK, 2 * LANES), 1) // HEAD_DIM
    out_low = lax.broadcasted_iota(jnp.int32, (CHUNK, LANES), 1) < HEAD_DIM
    key_col = lax.broadcasted_iota(jnp.int32, (1, n_keys), 1)

    for j in range(n_chunks):
        q = q_ref[0, j * CHUNK:(j + 1) * CHUNK, :] * (HEAD_DIM ** -0.5)
        parts = []
        for cg in range(GROUP // 4):
            qc = q[:, cg * 2 * LANES:(cg + 1) * 2 * LANES]
            for r in range(4):
                parts.append(jnp.where(lane_head == r, qc, jnp.zeros_like(qc)))
        lhs = jnp.concatenate(parts, axis=0)
        keys = k4[j * CHUNK:j * CHUNK + n_keys]
        s = lax.dot_general(lhs, keys, (((1,), (1,)), ((), ())), preferred_element_type=F32)
        if mask_first:
            first_valid = WINDOW - (i * tq + j * CHUNK)
            s = jnp.where(key_col >= first_valid, s, -jnp.inf)
        m = jnp.maximum(jnp.max(s, axis=-1, keepdims=True), sink)
        p = jnp.exp(s - m).astype(BF16)
        res = jnp.dot(p, vext[j * CHUNK:j * CHUNK + n_keys], preferred_element_type=F32)
        den = res[:, LANES:] + jnp.exp(sink - m)
        o = res[:, :LANES] / den
        for pr in range(GROUP // 2):
            a = o[(2 * pr) * CHUNK:(2 * pr + 1) * CHUNK]
            b = o[(2 * pr + 1) * CHUNK:(2 * pr + 2) * CHUNK]
            o_ref[0, j * CHUNK:(j + 1) * CHUNK, pr * LANES:(pr + 1) * LANES] = (
                jnp.where(out_low, a, b).astype(o_ref.dtype))


def _attention(q, k_cur, v_cur, k_prev, v_prev, sink_tab, *, tq, mask_first):
    batch, steps, _ = q.shape
    qw = GROUP * HEAD_DIM
    same_array = k_prev is k_cur
    per_tile = tq // WINDOW

    def prev_map(b, i, h):
        if same_array:
            return (b, jnp.maximum(i * per_tile - 1, 0), h // 2)
        return (b, 0, h // 2)

    return pl.pallas_call(
        functools.partial(_attn_body, mask_first=mask_first),
        grid=(batch, steps // tq, N_KV_HEADS),
        in_specs=[pl.BlockSpec((1, tq, qw), lambda b, i, h: (b, i, h)),
                  pl.BlockSpec((1, tq, LANES), lambda b, i, h: (b, i, h // 2)),
                  pl.BlockSpec((1, WINDOW, LANES), prev_map),
                  pl.BlockSpec((1, tq, LANES), lambda b, i, h: (b, i, h // 2)),
                  pl.BlockSpec((1, WINDOW, LANES), prev_map),
                  pl.BlockSpec((1, GROUP * CHUNK, LANES), lambda b, i, h: (h, 0, 0))],
        out_specs=pl.BlockSpec((1, tq, qw), lambda b, i, h: (b, i, h)),
        out_shape=jax.ShapeDtypeStruct((batch, steps, ATT_WIDTH), BF16),
        compiler_params=_params(("parallel", "parallel", "parallel")),
        name="attention",
    )(q, k_cur, k_prev, v_cur, v_prev, sink_tab)


def _log_sigmoid(x):
    return jnp.minimum(x, 0.0) - jnp.log1p(jnp.exp(-jnp.abs(x)))


def _mlstm_body(q_ref, k_ref, v_ref, o_ref, g_ref, bias_ref, c0_ref, n0_ref, m0_ref,
                y_ref, c_ref, n_ref, m_ref):
    @pl.when(pl.program_id(1) == 0)
    def _():
        c_ref[...] = c0_ref[...]
        n_ref[...] = n0_ref[...]
        m_ref[...] = m0_ref[...]

    g = g_ref[0] + bias_ref[...]
    li = g[:, :LANES]
    lf = _log_sigmoid(g[:, LANES:])
    row = lax.broadcasted_iota(jnp.int32, (CHUNK, CHUNK), 0)
    col = lax.broadcasted_iota(jnp.int32, (CHUNK, CHUNK), 1)
    causal = col <= row
    b = jnp.dot(causal.astype(F32), lf, preferred_element_type=F32,
                precision=lax.Precision.HIGHEST)
    r = li - b
    m_prev = m_ref[0]
    b_last = b[CHUNK - 1:CHUNK, :]
    m_new = b_last + jnp.maximum(m_prev, jnp.max(r, axis=0, keepdims=True))
    w_state = jnp.exp(r + (b_last - m_new))
    decay = jnp.exp(b_last + m_prev - m_new)
    r_rows = r.T

    for h in range(M_HEADS):
        r_row = r_rows[h:h + 1, :]
        dmat = jnp.where(causal, jnp.broadcast_to(r_row, (CHUNK, CHUNK)), -jnp.inf)
        m_h = m_prev[:, h:h + 1]
        gmax = jnp.maximum(jnp.max(dmat, axis=-1, keepdims=True), m_h)
        w_intra = jnp.exp(dmat - gmax)
        w_inter = jnp.exp(m_h - gmax)
        inv_floor = jnp.exp(-(b[:, h:h + 1] + gmax))

        q = q_ref[0, :, h * M_QK_DIM:(h + 1) * M_QK_DIM]
        k = k_ref[0, :, h * M_QK_DIM:(h + 1) * M_QK_DIM]
        v = v_ref[0, :, h * M_V_DIM:(h + 1) * M_V_DIM]
        c_old = c_ref[0, h]
        n_old = n_ref[0, h:h + 1, :]

        qk = lax.dot_general(q, k, (((1,), (1,)), ((), ())), preferred_element_type=F32)
        s = qk * w_intra
        intra = jnp.dot(s.astype(BF16), v, preferred_element_type=F32)
        inter = lax.dot_general(q, c_old.astype(BF16), (((1,), (1,)), ((), ())),
                                preferred_element_type=F32)
        nq = jnp.sum(q.astype(F32) * n_old, axis=-1, keepdims=True)
        num = intra + w_inter * inter
        den = jnp.sum(s, axis=-1, keepdims=True) + w_inter * nq
        hval = num / jnp.maximum(jnp.abs(den), inv_floor)
        og = jax.nn.sigmoid(o_ref[0, :, h * M_V_DIM:(h + 1) * M_V_DIM].astype(F32))
        y_ref[0, :, h * M_V_DIM:(h + 1) * M_V_DIM] = (og * hval).astype(y_ref.dtype)

        ws = w_state[:, h:h + 1]
        vw = (v.astype(F32) * ws).astype(BF16)
        dc = lax.dot_general(vw, k, (((0,), (0,)), ((), ())), preferred_element_type=F32)
        dec = decay[:, h:h + 1]
        c_ref[0, h] = dec * c_old + dc
        n_ref[0, h:h + 1, :] = dec * n_old + jnp.sum(k.astype(F32) * ws, axis=0, keepdims=True)

    m_ref[0] = m_new


def _mlstm(z, gates, bias, c0, n0, m0):
    batch, steps, _ = z.shape
    qk_blocks = M_QK_WIDTH // M_QK_WIDTH
    state_specs = [pl.BlockSpec((1, M_HEADS, M_V_DIM, M_QK_DIM), lambda b, c: (b, 0, 0, 0)),
                   pl.BlockSpec((1, M_HEADS, M_QK_DIM), lambda b, c: (b, 0, 0)),
                   pl.BlockSpec((1, 1, LANES), lambda b, c: (b, 0, 0))]
    del qk_blocks
    return pl.pallas_call(
        _mlstm_body,
        grid=(batch, steps // CHUNK),
        in_specs=[pl.BlockSpec((1, CHUNK, M_QK_WIDTH), lambda b, c: (b, c, 0)),
                  pl.BlockSpec((1, CHUNK, M_QK_WIDTH), lambda b, c: (b, c, 1)),
                  pl.BlockSpec((1, CHUNK, M_V_WIDTH), lambda b, c: (b, c, 1)),
                  pl.BlockSpec((1, CHUNK, M_V_WIDTH), lambda b, c: (b, c, 2)),
                  pl.BlockSpec((1, CHUNK, GATE_PAD), lambda b, c: (b, c, 0)),
                  pl.BlockSpec((1, GATE_PAD), lambda b, c: (0, 0))] + state_specs,
        out_specs=[pl.BlockSpec((1, CHUNK, M_V_WIDTH), lambda b, c: (b, c, 0))] + state_specs,
        out_shape=[jax.ShapeDtypeStruct((batch, steps, M_V_WIDTH), BF16),
                   jax.ShapeDtypeStruct(c0.shape, F32),
                   jax.ShapeDtypeStruct(n0.shape, F32),
                   jax.ShapeDtypeStruct(m0.shape, F32)],
        compiler_params=_params(("parallel", "arbitrary")),
        name="mlstm",
    )(z, z, z, z, gates, bias, c0, n0, m0)


def _ln_rows(x, g, b):
    mu = jnp.mean(x, axis=-1, keepdims=True)
    xc = x - mu
    var = jnp.mean(xc * xc, axis=-1, keepdims=True)
    return xc * lax.rsqrt(var + LN_EPS) * g + b


def _ln_mod_body(r_ref, g_ref, b_ref, sc_ref, sh_ref, x_ref, u_ref):
    y = _ln_rows(r_ref[...], g_ref[...], b_ref[...])
    x_ref[...] = y
    u_ref[...] = (y * (1.0 + sc_ref[...]) + sh_ref[...]).astype(u_ref.dtype)


def _ln_body(r_ref, g_ref, b_ref, x_ref):
    x_ref[...] = _ln_rows(r_ref[...], g_ref[...], b_ref[...])


def _layer_norm(r, g, b, mod=None, sc_idx=None, sh_idx=None):
    batch, steps, d = r.shape
    bb, tt = _row_blocks(batch, steps, 256)
    act = pl.BlockSpec((bb, tt, d), lambda i, t: (i, t, 0))
    vec = pl.BlockSpec((1, 1, d), lambda i, t: (0, 0, 0))
    g3, b3 = g.reshape(1, 1, d), b.reshape(1, 1, d)
    if mod is None:
        return pl.pallas_call(
            _ln_body, grid=(batch // bb, steps // tt),
            in_specs=[act, vec, vec], out_specs=act,
            out_shape=jax.ShapeDtypeStruct(r.shape, F32),
            compiler_params=_params(("parallel", "parallel")), name="layer_norm",
        )(r, g3, b3)
    return pl.pallas_call(
        _ln_mod_body, grid=(batch // bb, steps // tt),
        in_specs=[act, vec, vec,
                  pl.BlockSpec((bb, 1, d), lambda i, t: (i, 0, sc_idx)),
                  pl.BlockSpec((bb, 1, d), lambda i, t: (i, 0, sh_idx))],
        out_specs=[act, act],
        out_shape=[jax.ShapeDtypeStruct(r.shape, F32), jax.ShapeDtypeStruct(r.shape, BF16)],
        compiler_params=_params(("parallel", "parallel")), name="layer_norm_modulate",
    )(r, g3, b3, mod, mod)


SH1, SC1, GT1, SH2, SC2, GT2 = range(6)


def _rope_tables(pos):
    half = ROT_DIM // 2
    inv = ROPE_THETA ** (-2.0 * jnp.arange(half, dtype=F32) / ROT_DIM)
    ang = pos.astype(F32)[:, None] * inv[None, :]
    cos, sin = jnp.cos(ang), jnp.sin(ang)
    zeros = jnp.zeros_like(cos)
    rest = HEAD_DIM - ROT_DIM
    steps = pos.shape[0]
    one_head = lambda first, second, fill: jnp.concatenate(
        [first, second, jnp.full((steps, rest), fill, F32)], axis=1)
    cos_t = one_head(cos, cos, 1.0)
    up_t = one_head(-sin, zeros, 0.0)
    dn_t = one_head(zeros, sin, 0.0)
    reps = LANES // HEAD_DIM
    return tuple(jnp.tile(t, (1, reps)) for t in (cos_t, up_t, dn_t))


def _group_layer(x, mod, pos, wts, k_prev, v_prev, state, *, attn_tq, mask_first):
    batch, steps, d = x.shape
    ropes = _rope_tables(pos)

    u = _modulate(x, mod, SC1, SH1)

    qa = _matmul(_mm_rope_body, u, [wts["w_qa"]], ropes, _rope_specs(),
                 [(ATT_WIDTH, BF16, COL_TILE)], name="proj_q")
    kvg_n = 2 * KV_WIDTH + GATE_PAD
    ka, va, gates = _matmul(
        _mm_kvg_body, u, [wts["w_kvg"]], ropes, _rope_specs(),
        [(KV_WIDTH, F32, KV_WIDTH), (KV_WIDTH, F32, KV_WIDTH), (GATE_PAD, F32, GATE_PAD)],
        tn=kvg_n, name="proj_kvg")
    z = _matmul(_mm_plain_body, u, [wts["w_rest"]], [], [],
                [(wts["w_rest"].shape[1], BF16, COL_TILE)], name="proj_rest")

    if k_prev is None:
        k_prev, v_prev = ka, va
    ya = _attention(qa, ka, va, k_prev, v_prev, wts["sink_tab"], tq=attn_tq, mask_first=mask_first)

    yb, c_new, n_new, m_new = _mlstm(z, gates, wts["gate_bias"], *state)

    ga_col = (2 * M_QK_WIDTH + 2 * M_V_WIDTH) // COL_TILE
    gb_col = ga_col + D_MODEL // COL_TILE
    t = _matmul(_mm_gate_first_body, ya, [wts["w_up_a"]], [z],
                [lambda bb, tt: _act_spec(bb, tt, COL_TILE, ga_col)],
                [(d, F32, COL_TILE)], name="merge_a")
    merged = _matmul(_mm_gate_second_body, yb, [wts["w_up_b"]], [z, t],
                     [lambda bb, tt: _act_spec(bb, tt, COL_TILE, gb_col),
                      lambda bb, tt: _act_spec(bb, tt, COL_TILE)],
                     [(d, BF16, COL_TILE)], name="merge_b")

    def mod_spec(idx):
        per_vec = d // COL_TILE
        return lambda bb, tt: pl.BlockSpec((bb, 1, COL_TILE), lambda b, t_, j: (b, 0, idx * per_vec + j))

    r1 = _matmul(_mm_residual_body, merged, [wts["w_o"]], [x, mod],
                 [lambda bb, tt: _act_spec(bb, tt, COL_TILE), mod_spec(GT1)],
                 [(d, F32, COL_TILE)], name="out_proj")
    x1, u2 = _layer_norm(r1, wts["ln1_g"], wts["ln1_b"], mod, SC2, SH2)

    ff_tile = COL_TILE // 2
    hidden = _matmul(_mm_swiglu_body, u2, [wts["w_ffn_gate"], wts["w_ffn_up"]], [], [],
                     [(D_FF_PAD, BF16, ff_tile)], tn=ff_tile, name="ffn_in")

    def mod_spec_half(idx):
        per_vec = d // ff_tile
        return lambda bb, tt: pl.BlockSpec((bb, 1, ff_tile), lambda b, t_, j: (b, 0, idx * per_vec + j))

    r2 = _matmul(_mm_residual_body, hidden, [wts["w_ffn_out"]], [x1, mod],
                 [lambda bb, tt: _act_spec(bb, tt, ff_tile), mod_spec_half(GT2)],
                 [(d, F32, ff_tile)], rows=ROW_TILE // 2, tn=ff_tile, name="ffn_out")
    y = _layer_norm(r2, wts["ln2_g"], wts["ln2_b"])
    return y, ka, va, c_new, n_new, m_new


def _prepare_weights(w_in, b_if, attn_sinks, w_up_a, w_up_b, w_o, ln1_g, ln1_b,
                     w_ffn_in, w_ffn_out, ln2_g, ln2_b):
    d = D_MODEL
    o_qa = 0
    o_ka = o_qa + ATT_WIDTH
    o_qm = o_ka + 2 * KV_WIDTH
    o_km = o_qm + M_QK_WIDTH
    o_vm = o_km + M_QK_WIDTH
    o_if = o_vm + M_V_WIDTH
    o_om = o_if + 2 * M_HEADS
    pad_gate = jnp.zeros((d, LANES - M_HEADS), F32)
    w_kvg = jnp.concatenate(
        [w_in[:, o_ka:o_qm],
         w_in[:, o_if:o_if + M_HEADS], pad_gate,
         w_in[:, o_if + M_HEADS:o_om], pad_gate], axis=1).astype(BF16)
    w_rest = jnp.concatenate(
        [w_in[:, o_qm:o_km] * (M_QK_DIM ** -0.5),
         w_in[:, o_km:o_if],
         w_in[:, o_om:]], axis=1).astype(BF16)
    pad_bias = jnp.zeros((LANES - M_HEADS,), F32)
    gate_bias = jnp.concatenate([b_if[:M_HEADS], pad_bias, b_if[M_HEADS:], pad_bias]).reshape(1, GATE_PAD)
    ff_pad = D_FF_PAD - D_FF
    sink_tab = jnp.broadcast_to(attn_sinks.astype(F32).reshape(N_KV_HEADS, GROUP, 1, 1),
                                (N_KV_HEADS, GROUP, CHUNK, LANES)).reshape(N_KV_HEADS, GROUP * CHUNK, LANES)
    return {
        "w_qa": w_in[:, o_qa:o_ka].astype(BF16),
        "w_kvg": w_kvg,
        "w_rest": w_rest,
        "gate_bias": gate_bias,
        "sink_tab": sink_tab,
        "w_up_a": w_up_a.astype(BF16),
        "w_up_b": w_up_b.astype(BF16),
        "w_o": w_o.astype(BF16),
        "ln1_g": ln1_g, "ln1_b": ln1_b, "ln2_g": ln2_g, "ln2_b": ln2_b,
        "w_ffn_gate": jnp.pad(w_ffn_in[:, :D_FF], ((0, 0), (0, ff_pad))).astype(BF16),
        "w_ffn_up": jnp.pad(w_ffn_in[:, D_FF:], ((0, 0), (0, ff_pad))).astype(BF16),
        "w_ffn_out": jnp.pad(w_ffn_out, ((0, ff_pad), (0, 0))).astype(BF16),
    }


def kernel(x_prompt, x_sample, cache_k_win, cache_v_win, state_C, state_n, state_m, c_prompt, c_sample, w_ada, b_ada, w_in, b_if, attn_sinks, w_up_a, w_up_b, w_o, ln1_g, ln1_b, w_ffn_in, w_ffn_out, ln2_g, ln2_b):
    bp, sp, d = x_prompt.shape
    bs, ts, _ = x_sample.shape
    keep = cache_k_win.shape[2]
    l = 0

    wts = _prepare_weights(w_in[l], b_if[l], attn_sinks[l], w_up_a[l], w_up_b[l], w_o[l],
                           ln1_g[l], ln1_b[l], w_ffn_in[l], w_ffn_out[l], ln2_g[l], ln2_b[l])

    c_all = jnp.concatenate([c_prompt, c_sample], axis=0)
    rows = c_all.shape[0]
    rows_pad = -(-rows // 8) * 8
    c_all = jnp.pad(c_all, ((0, rows_pad - rows), (0, 0)))
    mod = _adaln(c_all, w_ada[l], b_ada[l])
    mod_p = mod[:bp].reshape(bp, 1, 6 * d)
    mod_s = mod[bp:bp + bs].reshape(bs, 1, 6 * d)

    def pad_lanes(m):
        return jnp.pad(m, ((0, 0), (0, LANES - M_HEADS))).reshape(m.shape[0], 1, LANES)

    state_p = (jnp.zeros((bp, M_HEADS, M_V_DIM, M_QK_DIM), F32),
               jnp.zeros((bp, M_HEADS, M_QK_DIM), F32),
               pad_lanes(jnp.full((bp, M_HEADS), M_INIT, F32)))
    yp, kap, vap, cp, np_, mp = _group_layer(
        x_prompt, mod_p, jnp.arange(sp), wts, None, None, state_p,
        attn_tq=min(512, sp), mask_first=True)

    ck = cache_k_win[l].reshape(bs, keep, KV_WIDTH)
    cv = cache_v_win[l].reshape(bs, keep, KV_WIDTH)
    state_s = (state_C[l], state_n[l], pad_lanes(state_m[l]))
    ys, kas, vas, cs, ns, ms = _group_layer(
        x_sample, mod_s, PAST_LEN + jnp.arange(ts), wts, ck, cv, state_s,
        attn_tq=ts, mask_first=False)

    def heads(a):
        return a.reshape(a.shape[0], a.shape[1], N_KV_HEADS, HEAD_DIM)

    kws = jnp.concatenate([ck, kas], axis=1)[:, -keep:]
    vws = jnp.concatenate([cv, vas], axis=1)[:, -keep:]
    return (yp, ys,
            heads(kap[:, -keep:])[None], heads(vap[:, -keep:])[None],
            cp[None], np_[None], mp[:, 0, :M_HEADS][None],
            heads(kws)[None], heads(vws)[None],
            cs[None], ns[None], ms[:, 0, :M_HEADS][None])
```

```python
import functools

import jax
import jax.numpy as jnp
from jax import lax
from jax.experimental import pallas as pl
from jax.experimental.pallas import tpu as pltpu

F32 = jnp.float32
BF16 = jnp.bfloat16

D_MODEL = 4096
CHUNK = 64
N_Q_HEADS = 64
N_KV_HEADS = 8
HEAD_DIM = 64
GROUP = N_Q_HEADS // N_KV_HEADS
WINDOW = 128
ROT_DIM = HEAD_DIM // 4
ROPE_THETA = 500000.0
M_HEADS = 8
M_QK_DIM = 256
M_V_DIM = 512
ATT_WIDTH = N_Q_HEADS * HEAD_DIM
KV_WIDTH = N_KV_HEADS * HEAD_DIM
M_QK_WIDTH = M_HEADS * M_QK_DIM
M_V_WIDTH = M_HEADS * M_V_DIM
D_FF = 11008
DEPTH = 1
PAST_LEN = 2048
ALPHA = (2.0 * DEPTH) ** 0.25
LN_EPS = 1e-5
M_INIT = -1e30

LANES = 128
ROW_TILE = 1024
COL_TILE = 1024
ATTN_ROWS = 512
GATE_PAD = 2 * LANES
VMEM_LIMIT = 60 * 1024 * 1024


def _params(semantics):
    return pltpu.CompilerParams(dimension_semantics=semantics, vmem_limit_bytes=VMEM_LIMIT)


def _row_blocks(batch, steps, rows):
    if steps >= rows:
        assert steps % rows == 0
        return 1, rows
    bb = min(batch, rows // steps)
    assert batch % bb == 0
    return bb, steps


def _adaln_body(c_ref, w_ref, b_ref, o_ref):
    c = c_ref[...]
    s = c * jax.nn.sigmoid(c)
    o_ref[...] = jnp.dot(s, w_ref[...], preferred_element_type=F32) + b_ref[...]


def _adaln(c, w_ada, b_ada):
    rows, d = c.shape
    n = w_ada.shape[1]
    tn = 512
    return pl.pallas_call(
        _adaln_body,
        grid=(n // tn,),
        in_specs=[pl.BlockSpec((rows, d), lambda j: (0, 0)),
                  pl.BlockSpec((d, tn), lambda j: (0, j)),
                  pl.BlockSpec((1, tn), lambda j: (0, j))],
        out_specs=pl.BlockSpec((rows, tn), lambda j: (0, j)),
        out_shape=jax.ShapeDtypeStruct((rows, n), F32),
        compiler_params=_params(("parallel",)),
        name="adaln",
    )(c, w_ada, b_ada.reshape(1, n))


def _modulate_body(x_ref, sc_ref, sh_ref, o_ref):
    o_ref[...] = (x_ref[...] * (1.0 + sc_ref[...]) + sh_ref[...]).astype(o_ref.dtype)


def _modulate(x, mod, sc_idx, sh_idx):
    batch, steps, d = x.shape
    bb, tt = _row_blocks(batch, steps, 512)
    return pl.pallas_call(
        _modulate_body,
        grid=(batch // bb, steps // tt),
        in_specs=[pl.BlockSpec((bb, tt, d), lambda b, t: (b, t, 0)),
                  pl.BlockSpec((bb, 1, d), lambda b, t: (b, 0, sc_idx)),
                  pl.BlockSpec((bb, 1, d), lambda b, t: (b, 0, sh_idx))],
        out_specs=pl.BlockSpec((bb, tt, d), lambda b, t: (b, t, 0)),
        out_shape=jax.ShapeDtypeStruct((batch, steps, d), BF16),
        compiler_params=_params(("parallel", "parallel")),
        name="modulate",
    )(x, mod, mod)


def _load_rows(a_ref):
    bb, tt, k = a_ref.shape
    return a_ref[...].reshape(bb * tt, k)


def _rope_rows(x, cos_ref, sa_ref, sb_ref, bb):
    rows, n = x.shape
    tt = rows // bb

    def table(ref):
        t = ref[...]
        if bb == 1:
            return t
        return jnp.broadcast_to(t[None], (bb, tt, LANES)).reshape(rows, LANES)

    cos, sa, sb = table(cos_ref), table(sa_ref), table(sb_ref)
    outs = []
    for s in range(n // LANES):
        xs = x[:, s * LANES:(s + 1) * LANES]
        up = pltpu.roll(xs, LANES - ROT_DIM // 2, axis=1)
        dn = pltpu.roll(xs, ROT_DIM // 2, axis=1)
        outs.append(xs * cos + up * sa + dn * sb)
    return jnp.concatenate(outs, axis=1)


def _mm_plain_body(a_ref, w_ref, o_ref):
    acc = jnp.dot(_load_rows(a_ref), w_ref[...], preferred_element_type=F32)
    o_ref[...] = acc.reshape(o_ref.shape).astype(o_ref.dtype)


def _mm_rope_body(a_ref, w_ref, cos_ref, sa_ref, sb_ref, o_ref):
    acc = jnp.dot(_load_rows(a_ref), w_ref[...], preferred_element_type=F32)
    out = _rope_rows(acc, cos_ref, sa_ref, sb_ref, a_ref.shape[0])
    o_ref[...] = out.reshape(o_ref.shape).astype(o_ref.dtype)


def _mm_kvg_body(a_ref, w_ref, wg_ref, cos_ref, sa_ref, sb_ref, k_ref, v_ref, g_ref):
    a = _load_rows(a_ref)
    acc = jnp.dot(a, w_ref[...], preferred_element_type=F32)
    k = _rope_rows(acc[:, :KV_WIDTH], cos_ref, sa_ref, sb_ref, a_ref.shape[0])
    k_ref[...] = k.reshape(k_ref.shape)
    v_ref[...] = acc[:, KV_WIDTH:].reshape(v_ref.shape)
    g_ref[...] = jnp.dot(a, wg_ref[...], preferred_element_type=F32).reshape(g_ref.shape)


def _mm_gate_first_body(a_ref, w_ref, g_ref, o_ref):
    acc = jnp.dot(_load_rows(a_ref), w_ref[...], preferred_element_type=F32)
    gate = jax.nn.sigmoid(g_ref[...].astype(F32))
    o_ref[...] = gate * acc.reshape(o_ref.shape)


def _mm_gate_second_body(a_ref, w_ref, g_ref, t_ref, o_ref):
    acc = jnp.dot(_load_rows(a_ref), w_ref[...], preferred_element_type=F32)
    gate = jax.nn.sigmoid(g_ref[...].astype(F32))
    o_ref[...] = (t_ref[...] + gate * acc.reshape(o_ref.shape)).astype(o_ref.dtype)


def _mm_residual_body(a_ref, w_ref, x_ref, g_ref, o_ref):
    acc = jnp.dot(_load_rows(a_ref), w_ref[...], preferred_element_type=F32)
    o_ref[...] = ALPHA * x_ref[...] + g_ref[...] * acc.reshape(o_ref.shape)


def _mm_swiglu_body(a_ref, wg_ref, wu_lo_ref, wu_hi_ref, o_ref):
    a = _load_rows(a_ref)
    gate = jnp.dot(a, wg_ref[...], preferred_element_type=F32)
    w_up = jnp.concatenate([wu_lo_ref[...], wu_hi_ref[...]], axis=1)
    up = jnp.dot(a, w_up, preferred_element_type=F32)
    out = gate * jax.nn.sigmoid(gate) * up
    o_ref[...] = out.reshape(o_ref.shape).astype(o_ref.dtype)


def _act_spec(bb, tt, width, col=None):
    if col is None:
        return pl.BlockSpec((bb, tt, width), lambda b, t, j: (b, t, j))
    return pl.BlockSpec((bb, tt, width), lambda b, t, j: (b, t, j + col))


def _matmul(body, a, weights, extras, extra_specs, outs, *, n_tiles, rows=ROW_TILE, name):
    batch, steps, k = a.shape
    bb, tt = _row_blocks(batch, steps, rows)
    grid = (batch // bb, steps // tt, n_tiles)
    in_specs = [pl.BlockSpec((bb, tt, k), lambda b, t, j: (b, t, 0))]
    in_specs += [pl.BlockSpec((k, cols), functools.partial(lambda b, t, j, col: (0, col(j)), col=col))
                 for _, cols, col in weights]
    in_specs += [spec(bb, tt) for spec in extra_specs]
    weights = [w for w, _, _ in weights]
    out_specs = [pl.BlockSpec((bb, tt, cols_tile), lambda b, t, j: (b, t, j)) for _, _, cols_tile in outs]
    out_shape = [jax.ShapeDtypeStruct((batch, steps, cols), dt) for cols, dt, _ in outs]
    single = len(outs) == 1
    return pl.pallas_call(
        body,
        grid=grid,
        in_specs=in_specs,
        out_specs=out_specs[0] if single else out_specs,
        out_shape=out_shape[0] if single else out_shape,
        compiler_params=_params(("parallel", "parallel", "parallel")),
        name=name,
    )(a, *weights, *extras)


def _rope_specs():
    return [lambda bb, tt: pl.BlockSpec((tt, LANES), lambda b, t, j: (t, 0))] * 3


def _attn_body(q_ref, kc_ref, kp_ref, vc_ref, vp_ref, sink_ref, o_ref, *, mask_first):
    bb, tq, _ = q_ref.shape
    n_chunks = tq // CHUNK
    n_keys = WINDOW + CHUNK
    i = pl.program_id(1)
    h = pl.program_id(2)

    lane = lax.broadcasted_iota(jnp.int32, (1, LANES), 1)
    keep = (lane // HEAD_DIM) == (h % 2)
    sink = sink_ref[0]
    lane_head = lax.broadcasted_iota(jnp.int32, (CHUNK, 2 * LANES), 1) // HEAD_DIM
    out_low = lax.broadcasted_iota(jnp.int32, (CHUNK, LANES), 1) < HEAD_DIM
    key_row = lax.broadcasted_iota(jnp.int32, (n_keys, 1), 0)
    ones = jnp.ones((16, n_keys), BF16)

    def both_halves(cur_ref, prev_ref, bi):
        pair = jnp.concatenate([prev_ref[bi].astype(F32), cur_ref[bi].astype(F32)], axis=0)
        swapped = pltpu.roll(pair, HEAD_DIM, axis=1)
        return jnp.where(keep, pair, swapped)

    def scores(bi, j, k4):
        q = q_ref[bi, j * CHUNK:(j + 1) * CHUNK, :] * (HEAD_DIM ** -0.5)
        parts = []
        for cg in range(GROUP // 4):
            qc = q[:, cg * 2 * LANES:(cg + 1) * 2 * LANES]
            for r in range(4):
                parts.append(jnp.where(lane_head == r, qc, jnp.zeros_like(qc)))
        qrows = jnp.concatenate(parts, axis=0)
        keys = k4[j * CHUNK:j * CHUNK + n_keys]
        st = lax.dot_general(keys, qrows, (((1,), (1,)), ((), ())), preferred_element_type=F32)
        if mask_first and j < WINDOW // CHUNK:
            first_valid = WINDOW - (i * tq + j * CHUNK)
            st = jnp.where(key_row >= first_valid, st, -jnp.inf)
        return st

    def values_t(j, v2):
        vwin = v2[j * CHUNK:j * CHUNK + n_keys]
        head = vwin[:LANES].T
        tail = vwin[LANES:]
        tail = jnp.concatenate([tail, tail], axis=0).T
        vt = jnp.concatenate([head[:HEAD_DIM], tail[:HEAD_DIM, :CHUNK]], axis=1).astype(BF16)
        return jnp.concatenate([vt, ones], axis=0)

    def finish(st, lhs):
        m = jnp.maximum(jnp.max(st, axis=0, keepdims=True), sink)
        pt = jnp.exp(st - m).astype(BF16)
        ot = jnp.dot(lhs, pt, preferred_element_type=F32)
        den = ot[HEAD_DIM:HEAD_DIM + 1] + jnp.exp(sink - m)
        return ot[:HEAD_DIM] / den

    def emit(bi, j, on):
        for pr in range(GROUP // 2):
            x = on[:, pr * LANES:(pr + 1) * LANES]
            xt = jnp.concatenate([x, x], axis=0).T
            o_ref[bi, j * CHUNK:(j + 1) * CHUNK, pr * LANES:(pr + 1) * LANES] = (
                jnp.where(out_low, xt[:CHUNK], xt[CHUNK:]).astype(o_ref.dtype))

    units, sts, lhss = [], [], []
    for bi in range(bb):
        k2 = both_halves(kc_ref, kp_ref, bi)
        v2 = both_halves(vc_ref, vp_ref, bi)
        k4 = jnp.concatenate([k2, k2], axis=1).astype(BF16)
        for j in range(n_chunks):
            units.append((bi, j))
            sts.append(scores(bi, j, k4))
            lhss.append(values_t(j, v2))
    for (bi, j), st, lhs in zip(units, sts, lhss):
        emit(bi, j, finish(st, lhs))


def _attention(q, k_cur, v_cur, k_prev, v_prev, sink_tab, *, bb, tq, mask_first):
    batch, steps, _ = q.shape
    qw = GROUP * HEAD_DIM
    same_array = k_prev is k_cur
    per_tile = tq // WINDOW

    def prev_map(b, i, h):
        if same_array:
            return (b, jnp.maximum(i * per_tile - 1, 0), h // 2)
        return (b, 0, h // 2)

    return pl.pallas_call(
        functools.partial(_attn_body, mask_first=mask_first),
        grid=(batch // bb, steps // tq, N_KV_HEADS),
        in_specs=[pl.BlockSpec((bb, tq, qw), lambda b, i, h: (b, i, h)),
                  pl.BlockSpec((bb, tq, LANES), lambda b, i, h: (b, i, h // 2)),
                  pl.BlockSpec((bb, WINDOW, LANES), prev_map),
                  pl.BlockSpec((bb, tq, LANES), lambda b, i, h: (b, i, h // 2)),
                  pl.BlockSpec((bb, WINDOW, LANES), prev_map),
                  pl.BlockSpec((1, 1, GROUP * CHUNK), lambda b, i, h: (h, 0, 0))],
        out_specs=pl.BlockSpec((bb, tq, qw), lambda b, i, h: (b, i, h)),
        out_shape=jax.ShapeDtypeStruct((batch, steps, ATT_WIDTH), BF16),
        compiler_params=_params(("parallel", "parallel", "parallel")),
        name="attention",
    )(q, k_cur, k_prev, v_cur, v_prev, sink_tab)


def _log_sigmoid(x):
    return jnp.minimum(x, 0.0) - jnp.log1p(jnp.exp(-jnp.abs(x)))


def _mlstm_body(q_ref, k_ref, v_ref, o_ref, g_ref, bias_ref, c0_ref, n0_ref, m0_ref,
                y_ref, c_ref, n_ref, m_ref):
    @pl.when(pl.program_id(1) == 0)
    def _():
        c_ref[...] = c0_ref[...]
        n_ref[...] = n0_ref[...]
        m_ref[...] = m0_ref[...]

    g = g_ref[0] + bias_ref[...]
    li = g[:, :LANES]
    lf = _log_sigmoid(g[:, LANES:])
    row = lax.broadcasted_iota(jnp.int32, (CHUNK, CHUNK), 0)
    col = lax.broadcasted_iota(jnp.int32, (CHUNK, CHUNK), 1)
    causal = col <= row
    b = jnp.dot(causal.astype(F32), lf, preferred_element_type=F32,
                precision=lax.Precision.HIGHEST)
    r = li - b
    m_prev = m_ref[0]
    b_last = b[CHUNK - 1:CHUNK, :]
    m_new = b_last + jnp.maximum(m_prev, jnp.max(r, axis=0, keepdims=True))
    w_state = jnp.exp(r + (b_last - m_new))
    decay = jnp.exp(b_last + m_prev - m_new)
    r_rows = r.T

    for h in range(M_HEADS):
        r_row = r_rows[h:h + 1, :]
        dmat = jnp.where(causal, jnp.broadcast_to(r_row, (CHUNK, CHUNK)), -jnp.inf)
        m_h = m_prev[:, h:h + 1]
        gmax = jnp.maximum(jnp.max(dmat, axis=-1, keepdims=True), m_h)
        w_intra = jnp.exp(dmat - gmax)
        w_inter = jnp.exp(m_h - gmax)
        inv_floor = jnp.exp(-(b[:, h:h + 1] + gmax))

        q = q_ref[0, :, h * M_QK_DIM:(h + 1) * M_QK_DIM] * (M_QK_DIM ** -0.5)
        k = k_ref[0, :, h * M_QK_DIM:(h + 1) * M_QK_DIM]
        v = v_ref[0, :, h * M_V_DIM:(h + 1) * M_V_DIM]
        c_old = c_ref[0, h]
        n_old = n_ref[0, h:h + 1, :]

        qk = lax.dot_general(q, k, (((1,), (1,)), ((), ())), preferred_element_type=F32)
        s = qk * w_intra
        intra = jnp.dot(s.astype(BF16), v, preferred_element_type=F32)
        inter = lax.dot_general(q, c_old.astype(BF16), (((1,), (1,)), ((), ())),
                                preferred_element_type=F32)
        nq = jnp.sum(q.astype(F32) * n_old, axis=-1, keepdims=True)
        num = intra + w_inter * inter
        den = jnp.sum(s, axis=-1, keepdims=True) + w_inter * nq
        hval = num / jnp.maximum(jnp.abs(den), inv_floor)
        og = jax.nn.sigmoid(o_ref[0, :, h * M_V_DIM:(h + 1) * M_V_DIM].astype(F32))
        y_ref[0, :, h * M_V_DIM:(h + 1) * M_V_DIM] = (og * hval).astype(y_ref.dtype)

        ws = w_state[:, h:h + 1]
        vw = (v.astype(F32) * ws).astype(BF16)
        dc = lax.dot_general(vw, k, (((0,), (0,)), ((), ())), preferred_element_type=F32)
        dec = decay[:, h:h + 1]
        c_ref[0, h] = dec * c_old + dc
        n_ref[0, h:h + 1, :] = dec * n_old + jnp.sum(k.astype(F32) * ws, axis=0, keepdims=True)

    m_ref[0] = m_new


def _mlstm(za, zb, gates, bias, c0, n0, m0):
    batch, steps, _ = za.shape
    state_specs = [pl.BlockSpec((1, M_HEADS, M_V_DIM, M_QK_DIM), lambda b, c: (b, 0, 0, 0)),
                   pl.BlockSpec((1, M_HEADS, M_QK_DIM), lambda b, c: (b, 0, 0)),
                   pl.BlockSpec((1, 1, LANES), lambda b, c: (b, 0, 0))]
    return pl.pallas_call(
        _mlstm_body,
        grid=(batch, steps // CHUNK),
        in_specs=[pl.BlockSpec((1, CHUNK, M_QK_WIDTH), lambda b, c: (b, c, 0)),
                  pl.BlockSpec((1, CHUNK, M_QK_WIDTH), lambda b, c: (b, c, 1)),
                  pl.BlockSpec((1, CHUNK, M_V_WIDTH), lambda b, c: (b, c, 1)),
                  pl.BlockSpec((1, CHUNK, M_V_WIDTH), lambda b, c: (b, c, 0)),
                  pl.BlockSpec((1, CHUNK, GATE_PAD), lambda b, c: (b, c, 0)),
                  pl.BlockSpec((1, GATE_PAD), lambda b, c: (0, 0))] + state_specs,
        out_specs=[pl.BlockSpec((1, CHUNK, M_V_WIDTH), lambda b, c: (b, c, 0))] + state_specs,
        out_shape=[jax.ShapeDtypeStruct((batch, steps, M_V_WIDTH), BF16),
                   jax.ShapeDtypeStruct(c0.shape, F32),
                   jax.ShapeDtypeStruct(n0.shape, F32),
                   jax.ShapeDtypeStruct(m0.shape, F32)],
        compiler_params=_params(("parallel", "arbitrary")),
        name="mlstm",
    )(za, za, za, zb, gates, bias, c0, n0, m0)


def _ln_rows(x, g, b):
    mu = jnp.mean(x, axis=-1, keepdims=True)
    xc = x - mu
    var = jnp.mean(xc * xc, axis=-1, keepdims=True)
    return xc * lax.rsqrt(var + LN_EPS) * g + b


def _ln_mod_body(r_ref, g_ref, b_ref, sc_ref, sh_ref, x_ref, u_ref):
    y = _ln_rows(r_ref[...], g_ref[...], b_ref[...])
    x_ref[...] = y
    u_ref[...] = (y * (1.0 + sc_ref[...]) + sh_ref[...]).astype(u_ref.dtype)


def _ln_body(r_ref, g_ref, b_ref, x_ref):
    x_ref[...] = _ln_rows(r_ref[...], g_ref[...], b_ref[...])


def _layer_norm(r, g, b, mod=None, sc_idx=None, sh_idx=None):
    batch, steps, d = r.shape
    bb, tt = _row_blocks(batch, steps, 256)
    act = pl.BlockSpec((bb, tt, d), lambda i, t: (i, t, 0))
    vec = pl.BlockSpec((1, 1, d), lambda i, t: (0, 0, 0))
    g3, b3 = g.reshape(1, 1, d), b.reshape(1, 1, d)
    if mod is None:
        return pl.pallas_call(
            _ln_body, grid=(batch // bb, steps // tt),
            in_specs=[act, vec, vec], out_specs=act,
            out_shape=jax.ShapeDtypeStruct(r.shape, F32),
            compiler_params=_params(("parallel", "parallel")), name="layer_norm",
        )(r, g3, b3)
    return pl.pallas_call(
        _ln_mod_body, grid=(batch // bb, steps // tt),
        in_specs=[act, vec, vec,
                  pl.BlockSpec((bb, 1, d), lambda i, t: (i, 0, sc_idx)),
                  pl.BlockSpec((bb, 1, d), lambda i, t: (i, 0, sh_idx))],
        out_specs=[act, act],
        out_shape=[jax.ShapeDtypeStruct(r.shape, F32), jax.ShapeDtypeStruct(r.shape, BF16)],
        compiler_params=_params(("parallel", "parallel")), name="layer_norm_modulate",
    )(r, g3, b3, mod, mod)


SH1, SC1, GT1, SH2, SC2, GT2 = range(6)


def _rope_tables(pos):
    half = ROT_DIM // 2
    inv = ROPE_THETA ** (-2.0 * jnp.arange(half, dtype=F32) / ROT_DIM)
    ang = pos.astype(F32)[:, None] * inv[None, :]
    cos, sin = jnp.cos(ang), jnp.sin(ang)
    zeros = jnp.zeros_like(cos)
    rest = HEAD_DIM - ROT_DIM
    steps = pos.shape[0]
    one_head = lambda first, second, fill: jnp.concatenate(
        [first, second, jnp.full((steps, rest), fill, F32)], axis=1)
    cos_t = one_head(cos, cos, 1.0)
    up_t = one_head(-sin, zeros, 0.0)
    dn_t = one_head(zeros, sin, 0.0)
    reps = LANES // HEAD_DIM
    return tuple(jnp.tile(t, (1, reps)) for t in (cos_t, up_t, dn_t))


def _group_layer(x, mod, pos, wts, k_prev, v_prev, state, *, attn_bb, attn_tq, mask_first):
    batch, steps, d = x.shape
    ropes = _rope_tables(pos)
    w_a, w_b = wts["w_a"], wts["w_b"]
    tile = COL_TILE

    u = _modulate(x, mod, SC1, SH1)

    kv_block = ATT_WIDTH // tile
    za_block = kv_block + 2 * KV_WIDTH // tile
    qa = _matmul(_mm_rope_body, u, [(w_a, tile, lambda j: j)], ropes, _rope_specs(),
                 [(ATT_WIDTH, BF16, tile)], n_tiles=ATT_WIDTH // tile, name="proj_q")
    ka, va, gates = _matmul(
        _mm_kvg_body, u, [(w_a, 2 * KV_WIDTH, lambda j: kv_block), (wts["w_g"], GATE_PAD, lambda j: 0)],
        ropes, _rope_specs(),
        [(KV_WIDTH, F32, KV_WIDTH), (KV_WIDTH, F32, KV_WIDTH), (GATE_PAD, F32, GATE_PAD)],
        n_tiles=1, name="proj_kvg")
    za_width = 2 * M_QK_WIDTH + M_V_WIDTH
    za = _matmul(_mm_plain_body, u, [(w_a, tile, lambda j: za_block + j)], [], [],
                 [(za_width, BF16, tile)], n_tiles=za_width // tile, name="proj_mem")
    zb_width = w_b.shape[1]
    zb = _matmul(_mm_plain_body, u, [(w_b, tile, lambda j: j)], [], [],
                 [(zb_width, BF16, tile)], n_tiles=zb_width // tile, name="proj_gates")

    if k_prev is None:
        k_prev, v_prev = ka, va
    ya = _attention(qa, ka, va, k_prev, v_prev, wts["sink_tab"], bb=attn_bb, tq=attn_tq, mask_first=mask_first)

    yb, c_new, n_new, m_new = _mlstm(za, zb, gates, wts["gate_bias"], *state)

    ga_col = M_V_WIDTH // tile
    gb_col = ga_col + d // tile
    t = _matmul(_mm_gate_first_body, ya, [(wts["w_up_a"], tile, lambda j: j)], [zb],
                [lambda bb, tt: _act_spec(bb, tt, tile, ga_col)],
                [(d, F32, tile)], n_tiles=d // tile, name="merge_a")
    merged = _matmul(_mm_gate_second_body, yb, [(wts["w_up_b"], tile, lambda j: j)], [zb, t],
                     [lambda bb, tt: _act_spec(bb, tt, tile, gb_col),
                      lambda bb, tt: _act_spec(bb, tt, tile)],
                     [(d, BF16, tile)], n_tiles=d // tile, name="merge_b")

    def mod_spec(idx, width):
        per_vec = d // width
        return lambda bb, tt: pl.BlockSpec((bb, 1, width), lambda b, t_, j: (b, 0, idx * per_vec + j))

    r1 = _matmul(_mm_residual_body, merged, [(wts["w_o"], tile, lambda j: j)], [x, mod],
                 [lambda bb, tt: _act_spec(bb, tt, tile), mod_spec(GT1, tile)],
                 [(d, F32, tile)], n_tiles=d // tile, name="out_proj")
    x1, u2 = _layer_norm(r1, wts["ln1_g"], wts["ln1_b"], mod, SC2, SH2)

    ff_tile = tile // 2
    half_tile = ff_tile // 2
    up_first = D_FF // half_tile
    up_last = 2 * D_FF // half_tile - 1
    ff_steps = -(-D_FF // ff_tile)
    w_ffn = wts["w_ffn"]
    hidden = _matmul(
        _mm_swiglu_body, u2,
        [(w_ffn, ff_tile, lambda j: j),
         (w_ffn, half_tile, lambda j: jnp.minimum(up_first + 2 * j, up_last)),
         (w_ffn, half_tile, lambda j: jnp.minimum(up_first + 2 * j + 1, up_last))],
        [], [], [(D_FF, BF16, ff_tile)], n_tiles=ff_steps, name="ffn_in")

    r2 = _matmul(_mm_residual_body, hidden, [(wts["w_ffn_out"], ff_tile, lambda j: j)], [x1, mod],
                 [lambda bb, tt: _act_spec(bb, tt, ff_tile), mod_spec(GT2, ff_tile)],
                 [(d, F32, ff_tile)], n_tiles=d // ff_tile, rows=ROW_TILE // 2, name="ffn_out")
    y = _layer_norm(r2, wts["ln2_g"], wts["ln2_b"])
    return y, ka, va, c_new, n_new, m_new


CAST_TILE = 512


def _cast_body(w_ref, o_ref):
    o_ref[...] = w_ref[0].astype(o_ref.dtype)


def _cast_gates_body(w_ref, o_ref):
    x = w_ref[0]
    valid = lax.broadcasted_iota(jnp.int32, (1, LANES), 1) < M_HEADS
    o_ref[:, :LANES] = jnp.where(valid, x, 0.0).astype(o_ref.dtype)
    f_first = pltpu.roll(x, LANES - M_HEADS, axis=1)
    o_ref[:, LANES:] = jnp.where(valid, f_first, 0.0).astype(o_ref.dtype)


def _cast_shifted_body(main_ref, next_ref, o_ref, *, shift):
    x = jnp.concatenate([main_ref[0], next_ref[0]], axis=1)
    o_ref[...] = x[:, shift:shift + o_ref.shape[1]].astype(o_ref.dtype)


def _cast_projection(w_in, layer):
    _, k, p = w_in.shape
    head_cols = ATT_WIDTH + 2 * KV_WIDTH + 2 * M_QK_WIDTH + M_V_WIDTH
    tail_start = head_cols + 2 * M_HEADS
    tail_cols = p - tail_start
    assert head_cols % CAST_TILE == 0 and tail_cols % CAST_TILE == 0
    params = _params(("parallel",))
    w_a = pl.pallas_call(
        _cast_body, grid=(head_cols // CAST_TILE,),
        in_specs=[pl.BlockSpec((1, k, CAST_TILE), lambda j: (layer, 0, j))],
        out_specs=pl.BlockSpec((k, CAST_TILE), lambda j: (0, j)),
        out_shape=jax.ShapeDtypeStruct((k, head_cols), BF16),
        compiler_params=params, name="cast_proj_head",
    )(w_in)
    w_g = pl.pallas_call(
        _cast_gates_body, grid=(1,),
        in_specs=[pl.BlockSpec((1, k, LANES), lambda j: (layer, 0, head_cols // LANES))],
        out_specs=pl.BlockSpec((k, GATE_PAD), lambda j: (0, 0)),
        out_shape=jax.ShapeDtypeStruct((k, GATE_PAD), BF16),
        compiler_params=params, name="cast_proj_gates",
    )(w_in)
    first = head_cols // CAST_TILE
    per_tile = CAST_TILE // LANES
    w_b = pl.pallas_call(
        functools.partial(_cast_shifted_body, shift=tail_start - head_cols),
        grid=(tail_cols // CAST_TILE,),
        in_specs=[pl.BlockSpec((1, k, CAST_TILE), lambda j: (layer, 0, first + j)),
                  pl.BlockSpec((1, k, LANES), lambda j: (layer, 0, (first + j + 1) * per_tile))],
        out_specs=pl.BlockSpec((k, CAST_TILE), lambda j: (0, j)),
        out_shape=jax.ShapeDtypeStruct((k, tail_cols), BF16),
        compiler_params=params, name="cast_proj_tail",
    )(w_in, w_in)
    return w_a, w_g, w_b


def kernel(x_prompt, x_sample, cache_k_win, cache_v_win, state_C, state_n, state_m, c_prompt, c_sample, w_ada, b_ada, w_in, b_if, attn_sinks, w_up_a, w_up_b, w_o, ln1_g, ln1_b, w_ffn_in, w_ffn_out, ln2_g, ln2_b):
    bp, sp, d = x_prompt.shape
    bs, ts, _ = x_sample.shape
    keep = cache_k_win.shape[2]
    l = 0

    w_a, w_g, w_b = _cast_projection(w_in, l)
    pad_bias = jnp.zeros((LANES - M_HEADS,), F32)
    wts = {
        "w_a": w_a, "w_g": w_g, "w_b": w_b,
        "gate_bias": jnp.concatenate([b_if[l, :M_HEADS], pad_bias, b_if[l, M_HEADS:], pad_bias]).reshape(1, GATE_PAD),
        "sink_tab": jnp.broadcast_to(attn_sinks[l].astype(F32).reshape(N_KV_HEADS, 1, GROUP, 1),
                                     (N_KV_HEADS, 1, GROUP, CHUNK)).reshape(N_KV_HEADS, 1, GROUP * CHUNK),
        "w_up_a": w_up_a[l].astype(BF16), "w_up_b": w_up_b[l].astype(BF16), "w_o": w_o[l].astype(BF16),
        "w_ffn": w_ffn_in[l].astype(BF16), "w_ffn_out": w_ffn_out[l].astype(BF16),
        "ln1_g": ln1_g[l], "ln1_b": ln1_b[l], "ln2_g": ln2_g[l], "ln2_b": ln2_b[l],
    }

    c_all = jnp.concatenate([c_prompt, c_sample], axis=0)
    rows = c_all.shape[0]
    rows_pad = -(-rows // 8) * 8
    c_all = jnp.pad(c_all, ((0, rows_pad - rows), (0, 0)))
    mod = _adaln(c_all, w_ada[l], b_ada[l])
    mod_p = mod[:bp].reshape(bp, 1, 6 * d)
    mod_s = mod[bp:bp + bs].reshape(bs, 1, 6 * d)

    def pad_lanes(m):
        return jnp.pad(m, ((0, 0), (0, LANES - M_HEADS))).reshape(m.shape[0], 1, LANES)

    state_p = (jnp.zeros((bp, M_HEADS, M_V_DIM, M_QK_DIM), F32),
               jnp.zeros((bp, M_HEADS, M_QK_DIM), F32),
               pad_lanes(jnp.full((bp, M_HEADS), M_INIT, F32)))
    yp, kap, vap, cp, np_, mp = _group_layer(
        x_prompt, mod_p, jnp.arange(sp), wts, None, None, state_p,
        attn_bb=1, attn_tq=min(ATTN_ROWS, sp), mask_first=True)

    ck = cache_k_win[l].reshape(bs, keep, KV_WIDTH)
    cv = cache_v_win[l].reshape(bs, keep, KV_WIDTH)
    state_s = (state_C[l], state_n[l], pad_lanes(state_m[l]))
    ys, kas, vas, cs, ns, ms = _group_layer(
        x_sample, mod_s, PAST_LEN + jnp.arange(ts), wts, ck, cv, state_s,
        attn_bb=min(bs, ATTN_ROWS // ts), attn_tq=ts, mask_first=False)

    def heads(a):
        return a.reshape(a.shape[0], a.shape[1], N_KV_HEADS, HEAD_DIM)

    kws = jnp.concatenate([ck, kas], axis=1)[:, -keep:]
    vws = jnp.concatenate([cv, vas], axis=1)[:, -keep:]
    return (yp, ys,
            heads(kap[:, -keep:])[None], heads(vap[:, -keep:])[None],
            cp[None], np_[None], mp[:, 0, :M_HEADS][None],
            heads(kws)[None], heads(vws)[None],
            cs[None], ns[None], ms[:, 0, :M_HEADS][None])
```

```python
import functools

import jax
import jax.numpy as jnp
from jax import lax
from jax.experimental import pallas as pl
from jax.experimental.pallas import tpu as pltpu

F32 = jnp.float32
BF16 = jnp.bfloat16

D_MODEL = 4096
CHUNK = 64
N_Q_HEADS = 64
N_KV_HEADS = 8
HEAD_DIM = 64
GROUP = N_Q_HEADS // N_KV_HEADS
WINDOW = 128
ROT_DIM = HEAD_DIM // 4
ROPE_THETA = 500000.0
M_HEADS = 8
M_QK_DIM = 256
M_V_DIM = 512
ATT_WIDTH = N_Q_HEADS * HEAD_DIM
KV_WIDTH = N_KV_HEADS * HEAD_DIM
M_QK_WIDTH = M_HEADS * M_QK_DIM
M_V_WIDTH = M_HEADS * M_V_DIM
D_FF = 11008
DEPTH = 1
PAST_LEN = 2048
ALPHA = (2.0 * DEPTH) ** 0.25
LN_EPS = 1e-5
M_INIT = -1e30

LANES = 128
ROW_TILE = 1024
COL_TILE = 1024
ATTN_ROWS = 512
MLSTM_ROWS = 256
GATE_PAD = 2 * LANES
VMEM_LIMIT = 60 * 1024 * 1024


def _params(semantics):
    return pltpu.CompilerParams(dimension_semantics=semantics, vmem_limit_bytes=VMEM_LIMIT)


def _row_blocks(batch, steps, rows):
    if steps >= rows:
        assert steps % rows == 0
        return 1, rows
    bb = min(batch, rows // steps)
    assert batch % bb == 0
    return bb, steps


def _adaln_body(c_ref, w_ref, b_ref, o_ref):
    c = c_ref[...]
    s = c * jax.nn.sigmoid(c)
    o_ref[...] = jnp.dot(s, w_ref[...], preferred_element_type=F32) + b_ref[...]


def _adaln(c, w_ada, b_ada):
    rows, d = c.shape
    n = w_ada.shape[1]
    tn = 512
    return pl.pallas_call(
        _adaln_body,
        grid=(n // tn,),
        in_specs=[pl.BlockSpec((rows, d), lambda j: (0, 0)),
                  pl.BlockSpec((d, tn), lambda j: (0, j)),
                  pl.BlockSpec((1, tn), lambda j: (0, j))],
        out_specs=pl.BlockSpec((rows, tn), lambda j: (0, j)),
        out_shape=jax.ShapeDtypeStruct((rows, n), F32),
        compiler_params=_params(("parallel",)),
        name="adaln",
    )(c, w_ada, b_ada.reshape(1, n))


def _modulate_body(x_ref, sc_ref, sh_ref, o_ref):
    o_ref[...] = (x_ref[...] * (1.0 + sc_ref[...]) + sh_ref[...]).astype(o_ref.dtype)


def _modulate(x, mod, sc_idx, sh_idx):
    batch, steps, d = x.shape
    bb, tt = _row_blocks(batch, steps, 512)
    return pl.pallas_call(
        _modulate_body,
        grid=(batch // bb, steps // tt),
        in_specs=[pl.BlockSpec((bb, tt, d), lambda b, t: (b, t, 0)),
                  pl.BlockSpec((bb, 1, d), lambda b, t: (b, 0, sc_idx)),
                  pl.BlockSpec((bb, 1, d), lambda b, t: (b, 0, sh_idx))],
        out_specs=pl.BlockSpec((bb, tt, d), lambda b, t: (b, t, 0)),
        out_shape=jax.ShapeDtypeStruct((batch, steps, d), BF16),
        compiler_params=_params(("parallel", "parallel")),
        name="modulate",
    )(x, mod, mod)


def _load_rows(a_ref):
    bb, tt, k = a_ref.shape
    return a_ref[...].reshape(bb * tt, k)


def _rope_rows(x, cos_ref, sa_ref, sb_ref, bb):
    rows, n = x.shape
    tt = rows // bb

    def table(ref):
        t = ref[...]
        if bb == 1:
            return t
        return jnp.broadcast_to(t[None], (bb, tt, LANES)).reshape(rows, LANES)

    cos, sa, sb = table(cos_ref), table(sa_ref), table(sb_ref)
    outs = []
    for s in range(n // LANES):
        xs = x[:, s * LANES:(s + 1) * LANES]
        up = pltpu.roll(xs, LANES - ROT_DIM // 2, axis=1)
        dn = pltpu.roll(xs, ROT_DIM // 2, axis=1)
        outs.append(xs * cos + up * sa + dn * sb)
    return jnp.concatenate(outs, axis=1)


def _mm_plain_body(a_ref, w_ref, o_ref):
    acc = jnp.dot(_load_rows(a_ref), w_ref[...], preferred_element_type=F32)
    o_ref[...] = acc.reshape(o_ref.shape).astype(o_ref.dtype)


def _mm_rope_body(a_ref, w_ref, cos_ref, sa_ref, sb_ref, o_ref):
    acc = jnp.dot(_load_rows(a_ref), w_ref[...], preferred_element_type=F32)
    out = _rope_rows(acc, cos_ref, sa_ref, sb_ref, a_ref.shape[0])
    o_ref[...] = out.reshape(o_ref.shape).astype(o_ref.dtype)


def _mm_kvg_body(a_ref, w_ref, wg_ref, cos_ref, sa_ref, sb_ref, k_ref, v_ref, g_ref):
    a = _load_rows(a_ref)
    acc = jnp.dot(a, w_ref[...], preferred_element_type=F32)
    k = _rope_rows(acc[:, :KV_WIDTH], cos_ref, sa_ref, sb_ref, a_ref.shape[0])
    k_ref[...] = k.reshape(k_ref.shape)
    v_ref[...] = acc[:, KV_WIDTH:].reshape(v_ref.shape)
    g_ref[...] = jnp.dot(a, wg_ref[...], preferred_element_type=F32).reshape(g_ref.shape)


def _mm_gate_first_body(a_ref, w_ref, g_ref, o_ref):
    acc = jnp.dot(_load_rows(a_ref), w_ref[...], preferred_element_type=F32)
    gate = jax.nn.sigmoid(g_ref[...].astype(F32))
    o_ref[...] = gate * acc.reshape(o_ref.shape)


def _mm_gate_second_body(a_ref, w_ref, g_ref, t_ref, o_ref):
    acc = jnp.dot(_load_rows(a_ref), w_ref[...], preferred_element_type=F32)
    gate = jax.nn.sigmoid(g_ref[...].astype(F32))
    o_ref[...] = (t_ref[...] + gate * acc.reshape(o_ref.shape)).astype(o_ref.dtype)


def _mm_residual_body(a_ref, w_ref, x_ref, g_ref, o_ref):
    acc = jnp.dot(_load_rows(a_ref), w_ref[...], preferred_element_type=F32)
    o_ref[...] = ALPHA * x_ref[...] + g_ref[...] * acc.reshape(o_ref.shape)


def _mm_swiglu_body(a_ref, wg_ref, wu_lo_ref, wu_hi_ref, o_ref):
    a = _load_rows(a_ref)
    gate = jnp.dot(a, wg_ref[...], preferred_element_type=F32)
    w_up = jnp.concatenate([wu_lo_ref[...], wu_hi_ref[...]], axis=1)
    up = jnp.dot(a, w_up, preferred_element_type=F32)
    out = gate * jax.nn.sigmoid(gate) * up
    o_ref[...] = out.reshape(o_ref.shape).astype(o_ref.dtype)


def _act_spec(bb, tt, width, col=None):
    if col is None:
        return pl.BlockSpec((bb, tt, width), lambda b, t, j: (b, t, j))
    return pl.BlockSpec((bb, tt, width), lambda b, t, j: (b, t, j + col))


def _matmul(body, a, weights, extras, extra_specs, outs, *, n_tiles, rows=ROW_TILE, name):
    batch, steps, k = a.shape
    bb, tt = _row_blocks(batch, steps, rows)
    grid = (batch // bb, steps // tt, n_tiles)
    in_specs = [pl.BlockSpec((bb, tt, k), lambda b, t, j: (b, t, 0))]
    in_specs += [pl.BlockSpec((k, cols), functools.partial(lambda b, t, j, col: (0, col(j)), col=col))
                 for _, cols, col in weights]
    in_specs += [spec(bb, tt) for spec in extra_specs]
    weights = [w for w, _, _ in weights]
    out_specs = [pl.BlockSpec((bb, tt, cols_tile), lambda b, t, j: (b, t, j)) for _, _, cols_tile in outs]
    out_shape = [jax.ShapeDtypeStruct((batch, steps, cols), dt) for cols, dt, _ in outs]
    single = len(outs) == 1
    return pl.pallas_call(
        body,
        grid=grid,
        in_specs=in_specs,
        out_specs=out_specs[0] if single else out_specs,
        out_shape=out_shape[0] if single else out_shape,
        compiler_params=_params(("parallel", "parallel", "parallel")),
        name=name,
    )(a, *weights, *extras)


def _rope_specs():
    return [lambda bb, tt: pl.BlockSpec((tt, LANES), lambda b, t, j: (t, 0))] * 3


def _attn_body(q_ref, kc_ref, kp_ref, vc_ref, vp_ref, sink_ref, o_ref, *, mask_first):
    bb, tq, _ = q_ref.shape
    n_chunks = tq // CHUNK
    n_keys = WINDOW + CHUNK
    i = pl.program_id(1)
    h = pl.program_id(2)

    lane = lax.broadcasted_iota(jnp.int32, (1, LANES), 1)
    keep = (lane // HEAD_DIM) == (h % 2)
    sink = sink_ref[0]
    lane_head = lax.broadcasted_iota(jnp.int32, (CHUNK, 2 * LANES), 1) // HEAD_DIM
    out_low = lax.broadcasted_iota(jnp.int32, (CHUNK, LANES), 1) < HEAD_DIM
    key_row = lax.broadcasted_iota(jnp.int32, (n_keys, 1), 0)
    ones = jnp.ones((16, n_keys), BF16)

    def both_halves(cur_ref, prev_ref, bi):
        pair = jnp.concatenate([prev_ref[bi].astype(F32), cur_ref[bi].astype(F32)], axis=0)
        swapped = pltpu.roll(pair, HEAD_DIM, axis=1)
        return jnp.where(keep, pair, swapped)

    def scores(bi, j, k4):
        q = q_ref[bi, j * CHUNK:(j + 1) * CHUNK, :] * (HEAD_DIM ** -0.5)
        parts = []
        for cg in range(GROUP // 4):
            qc = q[:, cg * 2 * LANES:(cg + 1) * 2 * LANES]
            for r in range(4):
                parts.append(jnp.where(lane_head == r, qc, jnp.zeros_like(qc)))
        qrows = jnp.concatenate(parts, axis=0)
        keys = k4[j * CHUNK:j * CHUNK + n_keys]
        st = lax.dot_general(keys, qrows, (((1,), (1,)), ((), ())), preferred_element_type=F32)
        if mask_first and j < WINDOW // CHUNK:
            first_valid = WINDOW - (i * tq + j * CHUNK)
            st = jnp.where(key_row >= first_valid, st, -jnp.inf)
        return st

    def values_t(j, v2):
        vwin = v2[j * CHUNK:j * CHUNK + n_keys]
        head = vwin[:LANES].T
        tail = vwin[LANES:]
        tail = jnp.concatenate([tail, tail], axis=0).T
        vt = jnp.concatenate([head[:HEAD_DIM], tail[:HEAD_DIM, :CHUNK]], axis=1).astype(BF16)
        return jnp.concatenate([vt, ones], axis=0)

    def finish(st, lhs):
        m = jnp.maximum(jnp.max(st, axis=0, keepdims=True), sink)
        pt = jnp.exp(st - m).astype(BF16)
        ot = jnp.dot(lhs, pt, preferred_element_type=F32)
        den = ot[HEAD_DIM:HEAD_DIM + 1] + jnp.exp(sink - m)
        return ot[:HEAD_DIM] / den

    def emit(bi, j, on):
        for pr in range(GROUP // 2):
            x = on[:, pr * LANES:(pr + 1) * LANES]
            xt = jnp.concatenate([x, x], axis=0).T
            o_ref[bi, j * CHUNK:(j + 1) * CHUNK, pr * LANES:(pr + 1) * LANES] = (
                jnp.where(out_low, xt[:CHUNK], xt[CHUNK:]).astype(o_ref.dtype))

    units, sts, lhss = [], [], []
    for bi in range(bb):
        k2 = both_halves(kc_ref, kp_ref, bi)
        v2 = both_halves(vc_ref, vp_ref, bi)
        k4 = jnp.concatenate([k2, k2], axis=1).astype(BF16)
        for j in range(n_chunks):
            units.append((bi, j))
            sts.append(scores(bi, j, k4))
            lhss.append(values_t(j, v2))
    for (bi, j), st, lhs in zip(units, sts, lhss):
        emit(bi, j, finish(st, lhs))


def _attention(q, k_cur, v_cur, k_prev, v_prev, sink_tab, *, bb, tq, mask_first):
    batch, steps, _ = q.shape
    qw = GROUP * HEAD_DIM
    same_array = k_prev is k_cur
    per_tile = tq // WINDOW

    def prev_map(b, i, h):
        if same_array:
            return (b, jnp.maximum(i * per_tile - 1, 0), h // 2)
        return (b, 0, h // 2)

    return pl.pallas_call(
        functools.partial(_attn_body, mask_first=mask_first),
        grid=(batch // bb, steps // tq, N_KV_HEADS),
        in_specs=[pl.BlockSpec((bb, tq, qw), lambda b, i, h: (b, i, h)),
                  pl.BlockSpec((bb, tq, LANES), lambda b, i, h: (b, i, h // 2)),
                  pl.BlockSpec((bb, WINDOW, LANES), prev_map),
                  pl.BlockSpec((bb, tq, LANES), lambda b, i, h: (b, i, h // 2)),
                  pl.BlockSpec((bb, WINDOW, LANES), prev_map),
                  pl.BlockSpec((1, 1, GROUP * CHUNK), lambda b, i, h: (h, 0, 0))],
        out_specs=pl.BlockSpec((bb, tq, qw), lambda b, i, h: (b, i, h)),
        out_shape=jax.ShapeDtypeStruct((batch, steps, ATT_WIDTH), BF16),
        compiler_params=_params(("parallel", "parallel", "parallel")),
        name="attention",
    )(q, k_cur, k_prev, v_cur, v_prev, sink_tab)


def _log_sigmoid(x):
    return jnp.minimum(x, 0.0) - jnp.log1p(jnp.exp(-jnp.abs(x)))


def _mlstm_body(q_ref, k_ref, v_ref, o_ref, g_ref, bias_ref, c0_ref, n0_ref, m0_ref,
                y_ref, c_ref, n_ref, m_ref):
    @pl.when(pl.program_id(1) == 0)
    def _():
        c_ref[...] = c0_ref[...]
        n_ref[...] = n0_ref[...]
        m_ref[...] = m0_ref[...]

    steps = q_ref.shape[1]
    g = g_ref[0] + bias_ref[...]
    li = g[:, :LANES]
    lf = _log_sigmoid(g[:, LANES:])
    row = lax.broadcasted_iota(jnp.int32, (steps, steps), 0)
    col = lax.broadcasted_iota(jnp.int32, (steps, steps), 1)
    causal = col <= row
    b = jnp.dot(causal.astype(F32), lf, preferred_element_type=F32,
                precision=lax.Precision.HIGHEST)
    r = li - b
    m_prev = m_ref[0]
    b_last = b[steps - 1:steps, :]
    m_new = b_last + jnp.maximum(m_prev, jnp.max(r, axis=0, keepdims=True))
    w_state = jnp.exp(r + (b_last - m_new))
    decay = jnp.exp(b_last + m_prev - m_new)
    r_rows = r.T

    for h in range(M_HEADS):
        r_row = r_rows[h:h + 1, :]
        dmat = jnp.where(causal, jnp.broadcast_to(r_row, (steps, steps)), -jnp.inf)
        m_h = m_prev[:, h:h + 1]
        gmax = jnp.maximum(jnp.max(dmat, axis=-1, keepdims=True), m_h)
        w_intra = jnp.exp(dmat - gmax)
        w_inter = jnp.exp(m_h - gmax)
        inv_floor = jnp.exp(-(b[:, h:h + 1] + gmax))

        q = q_ref[0, :, h * M_QK_DIM:(h + 1) * M_QK_DIM] * (M_QK_DIM ** -0.5)
        k = k_ref[0, :, h * M_QK_DIM:(h + 1) * M_QK_DIM]
        v = v_ref[0, :, h * M_V_DIM:(h + 1) * M_V_DIM]
        c_old = c_ref[0, h]
        n_old = n_ref[0, h:h + 1, :]

        qk = lax.dot_general(q, k, (((1,), (1,)), ((), ())), preferred_element_type=F32)
        s = qk * w_intra
        intra = jnp.dot(s.astype(BF16), v, preferred_element_type=F32)
        inter = lax.dot_general(q, c_old.astype(BF16), (((1,), (1,)), ((), ())),
                                preferred_element_type=F32)
        nq = jnp.sum(q.astype(F32) * n_old, axis=-1, keepdims=True)
        num = intra + w_inter * inter
        den = jnp.sum(s, axis=-1, keepdims=True) + w_inter * nq
        hval = num / jnp.maximum(jnp.abs(den), inv_floor)
        og = jax.nn.sigmoid(o_ref[0, :, h * M_V_DIM:(h + 1) * M_V_DIM].astype(F32))
        y_ref[0, :, h * M_V_DIM:(h + 1) * M_V_DIM] = (og * hval).astype(y_ref.dtype)

        ws = w_state[:, h:h + 1]
        vw = (v.astype(F32) * ws).astype(BF16)
        dc = lax.dot_general(vw, k, (((0,), (0,)), ((), ())), preferred_element_type=F32)
        dec = decay[:, h:h + 1]
        c_ref[0, h] = dec * c_old + dc
        n_ref[0, h:h + 1, :] = dec * n_old + jnp.sum(k.astype(F32) * ws, axis=0, keepdims=True)

    m_ref[0] = m_new


def _mlstm(za, zb, gates, bias, c0, n0, m0):
    batch, steps, _ = za.shape
    rows = min(steps, MLSTM_ROWS)
    state_specs = [pl.BlockSpec((1, M_HEADS, M_V_DIM, M_QK_DIM), lambda b, c: (b, 0, 0, 0)),
                   pl.BlockSpec((1, M_HEADS, M_QK_DIM), lambda b, c: (b, 0, 0)),
                   pl.BlockSpec((1, 1, LANES), lambda b, c: (b, 0, 0))]
    return pl.pallas_call(
        _mlstm_body,
        grid=(batch, steps // rows),
        in_specs=[pl.BlockSpec((1, rows, M_QK_WIDTH), lambda b, c: (b, c, 0)),
                  pl.BlockSpec((1, rows, M_QK_WIDTH), lambda b, c: (b, c, 1)),
                  pl.BlockSpec((1, rows, M_V_WIDTH), lambda b, c: (b, c, 1)),
                  pl.BlockSpec((1, rows, M_V_WIDTH), lambda b, c: (b, c, 0)),
                  pl.BlockSpec((1, rows, GATE_PAD), lambda b, c: (b, c, 0)),
                  pl.BlockSpec((1, GATE_PAD), lambda b, c: (0, 0))] + state_specs,
        out_specs=[pl.BlockSpec((1, rows, M_V_WIDTH), lambda b, c: (b, c, 0))] + state_specs,
        out_shape=[jax.ShapeDtypeStruct((batch, steps, M_V_WIDTH), BF16),
                   jax.ShapeDtypeStruct(c0.shape, F32),
                   jax.ShapeDtypeStruct(n0.shape, F32),
                   jax.ShapeDtypeStruct(m0.shape, F32)],
        compiler_params=_params(("parallel", "arbitrary")),
        name="mlstm",
    )(za, za, za, zb, gates, bias, c0, n0, m0)


def _ln_rows(x, g, b):
    mu = jnp.mean(x, axis=-1, keepdims=True)
    xc = x - mu
    var = jnp.mean(xc * xc, axis=-1, keepdims=True)
    return xc * lax.rsqrt(var + LN_EPS) * g + b


def _ln_mod_body(r_ref, g_ref, b_ref, sc_ref, sh_ref, x_ref, u_ref):
    y = _ln_rows(r_ref[...], g_ref[...], b_ref[...])
    x_ref[...] = y
    u_ref[...] = (y * (1.0 + sc_ref[...]) + sh_ref[...]).astype(u_ref.dtype)


def _ln_body(r_ref, g_ref, b_ref, x_ref):
    x_ref[...] = _ln_rows(r_ref[...], g_ref[...], b_ref[...])


def _layer_norm(r, g, b, mod=None, sc_idx=None, sh_idx=None):
    batch, steps, d = r.shape
    bb, tt = _row_blocks(batch, steps, 256)
    act = pl.BlockSpec((bb, tt, d), lambda i, t: (i, t, 0))
    vec = pl.BlockSpec((1, 1, d), lambda i, t: (0, 0, 0))
    g3, b3 = g.reshape(1, 1, d), b.reshape(1, 1, d)
    if mod is None:
        return pl.pallas_call(
            _ln_body, grid=(batch // bb, steps // tt),
            in_specs=[act, vec, vec], out_specs=act,
            out_shape=jax.ShapeDtypeStruct(r.shape, F32),
            compiler_params=_params(("parallel", "parallel")), name="layer_norm",
        )(r, g3, b3)
    return pl.pallas_call(
        _ln_mod_body, grid=(batch // bb, steps // tt),
        in_specs=[act, vec, vec,
                  pl.BlockSpec((bb, 1, d), lambda i, t: (i, 0, sc_idx)),
                  pl.BlockSpec((bb, 1, d), lambda i, t: (i, 0, sh_idx))],
        out_specs=[act, act],
        out_shape=[jax.ShapeDtypeStruct(r.shape, F32), jax.ShapeDtypeStruct(r.shape, BF16)],
        compiler_params=_params(("parallel", "parallel")), name="layer_norm_modulate",
    )(r, g3, b3, mod, mod)


SH1, SC1, GT1, SH2, SC2, GT2 = range(6)


def _rope_tables(pos):
    half = ROT_DIM // 2
    inv = ROPE_THETA ** (-2.0 * jnp.arange(half, dtype=F32) / ROT_DIM)
    ang = pos.astype(F32)[:, None] * inv[None, :]
    cos, sin = jnp.cos(ang), jnp.sin(ang)
    zeros = jnp.zeros_like(cos)
    rest = HEAD_DIM - ROT_DIM
    steps = pos.shape[0]
    one_head = lambda first, second, fill: jnp.concatenate(
        [first, second, jnp.full((steps, rest), fill, F32)], axis=1)
    cos_t = one_head(cos, cos, 1.0)
    up_t = one_head(-sin, zeros, 0.0)
    dn_t = one_head(zeros, sin, 0.0)
    reps = LANES // HEAD_DIM
    return tuple(jnp.tile(t, (1, reps)) for t in (cos_t, up_t, dn_t))


def _group_layer(x, mod, pos, wts, k_prev, v_prev, state, *, attn_bb, attn_tq, mask_first):
    batch, steps, d = x.shape
    ropes = _rope_tables(pos)
    w_a, w_b = wts["w_a"], wts["w_b"]
    tile = COL_TILE

    u = _modulate(x, mod, SC1, SH1)

    kv_block = ATT_WIDTH // tile
    za_block = kv_block + 2 * KV_WIDTH // tile
    qa = _matmul(_mm_rope_body, u, [(w_a, tile, lambda j: j)], ropes, _rope_specs(),
                 [(ATT_WIDTH, BF16, tile)], n_tiles=ATT_WIDTH // tile, name="proj_q")
    ka, va, gates = _matmul(
        _mm_kvg_body, u, [(w_a, 2 * KV_WIDTH, lambda j: kv_block), (wts["w_g"], GATE_PAD, lambda j: 0)],
        ropes, _rope_specs(),
        [(KV_WIDTH, F32, KV_WIDTH), (KV_WIDTH, F32, KV_WIDTH), (GATE_PAD, F32, GATE_PAD)],
        n_tiles=1, name="proj_kvg")
    za_width = 2 * M_QK_WIDTH + M_V_WIDTH
    za = _matmul(_mm_plain_body, u, [(w_a, tile, lambda j: za_block + j)], [], [],
                 [(za_width, BF16, tile)], n_tiles=za_width // tile, name="proj_mem")
    zb_width = w_b.shape[1]
    zb = _matmul(_mm_plain_body, u, [(w_b, tile, lambda j: j)], [], [],
                 [(zb_width, BF16, tile)], n_tiles=zb_width // tile, name="proj_gates")

    if k_prev is None:
        k_prev, v_prev = ka, va
    ya = _attention(qa, ka, va, k_prev, v_prev, wts["sink_tab"], bb=attn_bb, tq=attn_tq, mask_first=mask_first)

    yb, c_new, n_new, m_new = _mlstm(za, zb, gates, wts["gate_bias"], *state)

    ga_col = M_V_WIDTH // tile
    gb_col = ga_col + d // tile
    t = _matmul(_mm_gate_first_body, ya, [(wts["w_up_a"], tile, lambda j: j)], [zb],
                [lambda bb, tt: _act_spec(bb, tt, tile, ga_col)],
                [(d, F32, tile)], n_tiles=d // tile, name="merge_a")
    merged = _matmul(_mm_gate_second_body, yb, [(wts["w_up_b"], tile, lambda j: j)], [zb, t],
                     [lambda bb, tt: _act_spec(bb, tt, tile, gb_col),
                      lambda bb, tt: _act_spec(bb, tt, tile)],
                     [(d, BF16, tile)], n_tiles=d // tile, name="merge_b")

    def mod_spec(idx, width):
        per_vec = d // width
        return lambda bb, tt: pl.BlockSpec((bb, 1, width), lambda b, t_, j: (b, 0, idx * per_vec + j))

    r1 = _matmul(_mm_residual_body, merged, [(wts["w_o"], tile, lambda j: j)], [x, mod],
                 [lambda bb, tt: _act_spec(bb, tt, tile), mod_spec(GT1, tile)],
                 [(d, F32, tile)], n_tiles=d // tile, name="out_proj")
    x1, u2 = _layer_norm(r1, wts["ln1_g"], wts["ln1_b"], mod, SC2, SH2)

    ff_tile = tile // 2
    half_tile = ff_tile // 2
    up_first = D_FF // half_tile
    up_last = 2 * D_FF // half_tile - 1
    ff_steps = -(-D_FF // ff_tile)
    w_ffn = wts["w_ffn"]
    hidden = _matmul(
        _mm_swiglu_body, u2,
        [(w_ffn, ff_tile, lambda j: j),
         (w_ffn, half_tile, lambda j: jnp.minimum(up_first + 2 * j, up_last)),
         (w_ffn, half_tile, lambda j: jnp.minimum(up_first + 2 * j + 1, up_last))],
        [], [], [(D_FF, BF16, ff_tile)], n_tiles=ff_steps, name="ffn_in")

    r2 = _matmul(_mm_residual_body, hidden, [(wts["w_ffn_out"], ff_tile, lambda j: j)], [x1, mod],
                 [lambda bb, tt: _act_spec(bb, tt, ff_tile), mod_spec(GT2, ff_tile)],
                 [(d, F32, ff_tile)], n_tiles=d // ff_tile, rows=ROW_TILE // 2, name="ffn_out")
    y = _layer_norm(r2, wts["ln2_g"], wts["ln2_b"])
    return y, ka, va, c_new, n_new, m_new


CAST_TILE = 512


def _cast_body(w_ref, o_ref):
    o_ref[...] = w_ref[0].T.astype(o_ref.dtype)


def _cast_gates_body(w_ref, o_ref):
    x = w_ref[0].T
    valid = lax.broadcasted_iota(jnp.int32, (1, LANES), 1) < M_HEADS
    o_ref[:, :LANES] = jnp.where(valid, x, 0.0).astype(o_ref.dtype)
    f_first = pltpu.roll(x, LANES - M_HEADS, axis=1)
    o_ref[:, LANES:] = jnp.where(valid, f_first, 0.0).astype(o_ref.dtype)


def _cast_shifted_body(main_ref, next_ref, o_ref, *, shift):
    x = jnp.concatenate([main_ref[0], next_ref[0]], axis=0)
    o_ref[...] = x[shift:shift + o_ref.shape[1]].T.astype(o_ref.dtype)


def _cast_projection(w_in_t, layer):
    _, p, k = w_in_t.shape
    head_cols = ATT_WIDTH + 2 * KV_WIDTH + 2 * M_QK_WIDTH + M_V_WIDTH
    gate_cols = 2 * M_HEADS
    tail_start = head_cols + gate_cols
    tail_cols = p - tail_start
    assert head_cols % CAST_TILE == 0 and tail_cols % CAST_TILE == 0 and CAST_TILE % gate_cols == 0
    params = _params(("parallel",))
    w_a = pl.pallas_call(
        _cast_body, grid=(head_cols // CAST_TILE,),
        in_specs=[pl.BlockSpec((1, CAST_TILE, k), lambda j: (layer, j, 0))],
        out_specs=pl.BlockSpec((k, CAST_TILE), lambda j: (0, j)),
        out_shape=jax.ShapeDtypeStruct((k, head_cols), BF16),
        compiler_params=params, name="cast_proj_head",
    )(w_in_t)
    w_g = pl.pallas_call(
        _cast_gates_body, grid=(1,),
        in_specs=[pl.BlockSpec((1, LANES, k), lambda j: (layer, head_cols // LANES, 0))],
        out_specs=pl.BlockSpec((k, GATE_PAD), lambda j: (0, 0)),
        out_shape=jax.ShapeDtypeStruct((k, GATE_PAD), BF16),
        compiler_params=params, name="cast_proj_gates",
    )(w_in_t)
    first = head_cols // CAST_TILE
    per_tile = CAST_TILE // gate_cols
    w_b = pl.pallas_call(
        functools.partial(_cast_shifted_body, shift=gate_cols),
        grid=(tail_cols // CAST_TILE,),
        in_specs=[pl.BlockSpec((1, CAST_TILE, k), lambda j: (layer, first + j, 0)),
                  pl.BlockSpec((1, gate_cols, k), lambda j: (layer, (first + j + 1) * per_tile, 0))],
        out_specs=pl.BlockSpec((k, CAST_TILE), lambda j: (0, j)),
        out_shape=jax.ShapeDtypeStruct((k, tail_cols), BF16),
        compiler_params=params, name="cast_proj_tail",
    )(w_in_t, w_in_t)
    return w_a, w_g, w_b


def kernel(x_prompt, x_sample, cache_k_win, cache_v_win, state_C, state_n, state_m, c_prompt, c_sample, w_ada, b_ada, w_in, b_if, attn_sinks, w_up_a, w_up_b, w_o, ln1_g, ln1_b, w_ffn_in, w_ffn_out, ln2_g, ln2_b):
    bp, sp, d = x_prompt.shape
    bs, ts, _ = x_sample.shape
    keep = cache_k_win.shape[2]
    l = 0

    w_a, w_g, w_b = _cast_projection(jnp.swapaxes(w_in, 1, 2), l)
    pad_bias = jnp.zeros((LANES - M_HEADS,), F32)
    wts = {
        "w_a": w_a, "w_g": w_g, "w_b": w_b,
        "gate_bias": jnp.concatenate([b_if[l, :M_HEADS], pad_bias, b_if[l, M_HEADS:], pad_bias]).reshape(1, GATE_PAD),
        "sink_tab": jnp.broadcast_to(attn_sinks[l].astype(F32).reshape(N_KV_HEADS, 1, GROUP, 1),
                                     (N_KV_HEADS, 1, GROUP, CHUNK)).reshape(N_KV_HEADS, 1, GROUP * CHUNK),
        "w_up_a": w_up_a[l].astype(BF16), "w_up_b": w_up_b[l].astype(BF16), "w_o": w_o[l].astype(BF16),
        "w_ffn": w_ffn_in[l].astype(BF16), "w_ffn_out": w_ffn_out[l].astype(BF16),
        "ln1_g": ln1_g[l], "ln1_b": ln1_b[l], "ln2_g": ln2_g[l], "ln2_b": ln2_b[l],
    }

    c_all = jnp.concatenate([c_prompt, c_sample], axis=0)
    rows = c_all.shape[0]
    rows_pad = -(-rows // 8) * 8
    c_all = jnp.pad(c_all, ((0, rows_pad - rows), (0, 0)))
    mod = _adaln(c_all, w_ada[l], b_ada[l])
    mod_p = mod[:bp].reshape(bp, 1, 6 * d)
    mod_s = mod[bp:bp + bs].reshape(bs, 1, 6 * d)

    def pad_lanes(m):
        return jnp.pad(m, ((0, 0), (0, LANES - M_HEADS))).reshape(m.shape[0], 1, LANES)

    state_p = (jnp.zeros((bp, M_HEADS, M_V_DIM, M_QK_DIM), F32),
               jnp.zeros((bp, M_HEADS, M_QK_DIM), F32),
               pad_lanes(jnp.full((bp, M_HEADS), M_INIT, F32)))
    yp, kap, vap, cp, np_, mp = _group_layer(
        x_prompt, mod_p, jnp.arange(sp), wts, None, None, state_p,
        attn_bb=1, attn_tq=min(ATTN_ROWS, sp), mask_first=True)

    ck = cache_k_win[l].reshape(bs, keep, KV_WIDTH)
    cv = cache_v_win[l].reshape(bs, keep, KV_WIDTH)
    state_s = (state_C[l], state_n[l], pad_lanes(state_m[l]))
    ys, kas, vas, cs, ns, ms = _group_layer(
        x_sample, mod_s, PAST_LEN + jnp.arange(ts), wts, ck, cv, state_s,
        attn_bb=min(bs, ATTN_ROWS // ts), attn_tq=ts, mask_first=False)

    def heads(a):
        return a.reshape(a.shape[0], a.shape[1], N_KV_HEADS, HEAD_DIM)

    kws = jnp.concatenate([ck, kas], axis=1)[:, -keep:]
    vws = jnp.concatenate([cv, vas], axis=1)[:, -keep:]
    return (yp, ys,
            heads(kap[:, -keep:])[None], heads(vap[:, -keep:])[None],
            cp[None], np_[None], mp[:, 0, :M_HEADS][None],
            heads(kws)[None], heads(vws)[None],
            cs[None], ns[None], ms[:, 0, :M_HEADS][None])
```

```python
import functools

import jax
import jax.numpy as jnp
from jax import lax
from jax.experimental import pallas as pl
from jax.experimental.pallas import tpu as pltpu

F32 = jnp.float32
BF16 = jnp.bfloat16

D_MODEL = 4096
CHUNK = 64
N_Q_HEADS = 64
N_KV_HEADS = 8
HEAD_DIM = 64
GROUP = N_Q_HEADS // N_KV_HEADS
WINDOW = 128
ROT_DIM = HEAD_DIM // 4
ROPE_THETA = 500000.0
M_HEADS = 8
M_QK_DIM = 256
M_V_DIM = 512
ATT_WIDTH = N_Q_HEADS * HEAD_DIM
KV_WIDTH = N_KV_HEADS * HEAD_DIM
M_QK_WIDTH = M_HEADS * M_QK_DIM
M_V_WIDTH = M_HEADS * M_V_DIM
D_FF = 11008
DEPTH = 1
PAST_LEN = 2048
ALPHA = (2.0 * DEPTH) ** 0.25
LN_EPS = 1e-5
M_INIT = -1e30

LANES = 128
ROW_TILE = 1024
COL_TILE = 1024
ATTN_ROWS = 512
MLSTM_ROWS = 256
GATE_PAD = 2 * LANES
VMEM_LIMIT = 60 * 1024 * 1024


def _params(semantics):
    return pltpu.CompilerParams(dimension_semantics=semantics, vmem_limit_bytes=VMEM_LIMIT)


def _row_blocks(batch, steps, rows):
    if steps >= rows:
        assert steps % rows == 0
        return 1, rows
    bb = min(batch, rows // steps)
    assert batch % bb == 0
    return bb, steps


def _adaln_body(c_ref, w_ref, b_ref, o_ref):
    c = c_ref[...]
    s = c * jax.nn.sigmoid(c)
    o_ref[...] = jnp.dot(s, w_ref[...], preferred_element_type=F32) + b_ref[...]


def _adaln(c, w_ada, b_ada):
    rows, d = c.shape
    n = w_ada.shape[1]
    tn = 512
    return pl.pallas_call(
        _adaln_body,
        grid=(n // tn,),
        in_specs=[pl.BlockSpec((rows, d), lambda j: (0, 0)),
                  pl.BlockSpec((d, tn), lambda j: (0, j)),
                  pl.BlockSpec((1, tn), lambda j: (0, j))],
        out_specs=pl.BlockSpec((rows, tn), lambda j: (0, j)),
        out_shape=jax.ShapeDtypeStruct((rows, n), F32),
        compiler_params=_params(("parallel",)),
        name="adaln",
    )(c, w_ada, b_ada.reshape(1, n))


def _modulate_body(x_ref, sc_ref, sh_ref, o_ref):
    o_ref[...] = (x_ref[...] * (1.0 + sc_ref[...]) + sh_ref[...]).astype(o_ref.dtype)


def _modulate(x, mod, sc_idx, sh_idx):
    batch, steps, d = x.shape
    bb, tt = _row_blocks(batch, steps, 512)
    return pl.pallas_call(
        _modulate_body,
        grid=(batch // bb, steps // tt),
        in_specs=[pl.BlockSpec((bb, tt, d), lambda b, t: (b, t, 0)),
                  pl.BlockSpec((bb, 1, d), lambda b, t: (b, 0, sc_idx)),
                  pl.BlockSpec((bb, 1, d), lambda b, t: (b, 0, sh_idx))],
        out_specs=pl.BlockSpec((bb, tt, d), lambda b, t: (b, t, 0)),
        out_shape=jax.ShapeDtypeStruct((batch, steps, d), BF16),
        compiler_params=_params(("parallel", "parallel")),
        name="modulate",
    )(x, mod, mod)


def _load_rows(a_ref):
    bb, tt, k = a_ref.shape
    return a_ref[...].reshape(bb * tt, k)


def _rope_rows(x, cos_ref, sa_ref, sb_ref, bb):
    rows, n = x.shape
    tt = rows // bb

    def table(ref):
        t = ref[...]
        if bb == 1:
            return t
        return jnp.broadcast_to(t[None], (bb, tt, LANES)).reshape(rows, LANES)

    cos, sa, sb = table(cos_ref), table(sa_ref), table(sb_ref)
    outs = []
    for s in range(n // LANES):
        xs = x[:, s * LANES:(s + 1) * LANES]
        up = pltpu.roll(xs, LANES - ROT_DIM // 2, axis=1)
        dn = pltpu.roll(xs, ROT_DIM // 2, axis=1)
        outs.append(xs * cos + up * sa + dn * sb)
    return jnp.concatenate(outs, axis=1)


def _mm_plain_body(a_ref, w_ref, o_ref):
    acc = jnp.dot(_load_rows(a_ref), w_ref[...], preferred_element_type=F32)
    o_ref[...] = acc.reshape(o_ref.shape).astype(o_ref.dtype)


def _mm_rope_body(a_ref, w_ref, cos_ref, sa_ref, sb_ref, o_ref):
    acc = jnp.dot(_load_rows(a_ref), w_ref[...], preferred_element_type=F32)
    out = _rope_rows(acc, cos_ref, sa_ref, sb_ref, a_ref.shape[0])
    o_ref[...] = out.reshape(o_ref.shape).astype(o_ref.dtype)


def _mm_kvg_body(a_ref, w_ref, wg_ref, cos_ref, sa_ref, sb_ref, k_ref, v_ref, g_ref):
    a = _load_rows(a_ref)
    acc = jnp.dot(a, w_ref[...], preferred_element_type=F32)
    k = _rope_rows(acc[:, :KV_WIDTH], cos_ref, sa_ref, sb_ref, a_ref.shape[0])
    k_ref[...] = k.reshape(k_ref.shape)
    v_ref[...] = acc[:, KV_WIDTH:].reshape(v_ref.shape)
    g_ref[...] = jnp.dot(a, wg_ref[...], preferred_element_type=F32).reshape(g_ref.shape)


def _mm_gate_first_body(a_ref, w_ref, g_ref, o_ref):
    acc = jnp.dot(_load_rows(a_ref), w_ref[...], preferred_element_type=F32)
    gate = jax.nn.sigmoid(g_ref[...].astype(F32))
    o_ref[...] = gate * acc.reshape(o_ref.shape)


def _mm_gate_second_body(a_ref, w_ref, g_ref, t_ref, o_ref):
    acc = jnp.dot(_load_rows(a_ref), w_ref[...], preferred_element_type=F32)
    gate = jax.nn.sigmoid(g_ref[...].astype(F32))
    o_ref[...] = (t_ref[...] + gate * acc.reshape(o_ref.shape)).astype(o_ref.dtype)


def _mm_residual_body(a_ref, w_ref, x_ref, g_ref, o_ref):
    acc = jnp.dot(_load_rows(a_ref), w_ref[...], preferred_element_type=F32)
    o_ref[...] = ALPHA * x_ref[...] + g_ref[...] * acc.reshape(o_ref.shape)


def _mm_swiglu_body(a_ref, wg_ref, wu_ref, o_ref):
    a = _load_rows(a_ref)
    gate = jnp.dot(a, wg_ref[...], preferred_element_type=F32)
    up = jnp.dot(a, wu_ref[...], preferred_element_type=F32)
    out = gate * jax.nn.sigmoid(gate) * up
    o_ref[...] = out.reshape(o_ref.shape).astype(o_ref.dtype)


def _act_spec(bb, tt, width, col=None):
    if col is None:
        return pl.BlockSpec((bb, tt, width), lambda b, t, j: (b, t, j))
    return pl.BlockSpec((bb, tt, width), lambda b, t, j: (b, t, j + col))


def _matmul(body, a, weights, extras, extra_specs, outs, *, n_tiles, rows=ROW_TILE, name):
    batch, steps, k = a.shape
    bb, tt = _row_blocks(batch, steps, rows)
    grid = (batch // bb, steps // tt, n_tiles)
    in_specs = [pl.BlockSpec((bb, tt, k), lambda b, t, j: (b, t, 0))]
    in_specs += [pl.BlockSpec((k, cols), functools.partial(lambda b, t, j, col: (0, col(j)), col=col))
                 for _, cols, col in weights]
    in_specs += [spec(bb, tt) for spec in extra_specs]
    weights = [w for w, _, _ in weights]
    out_specs = [pl.BlockSpec((bb, tt, cols_tile), lambda b, t, j: (b, t, j)) for _, _, cols_tile in outs]
    out_shape = [jax.ShapeDtypeStruct((batch, steps, cols), dt) for cols, dt, _ in outs]
    single = len(outs) == 1
    return pl.pallas_call(
        body,
        grid=grid,
        in_specs=in_specs,
        out_specs=out_specs[0] if single else out_specs,
        out_shape=out_shape[0] if single else out_shape,
        compiler_params=_params(("parallel", "parallel", "parallel")),
        name=name,
    )(a, *weights, *extras)


def _matmul_casting(body, a, casts, extras, extra_specs, outs, *, n_tiles, rows=ROW_TILE, name):
    batch, steps, k = a.shape
    bb, tt = _row_blocks(batch, steps, rows)
    grid = (n_tiles, batch // bb, steps // tt)

    def reorder(spec):
        return pl.BlockSpec(spec.block_shape, lambda j, b, t: spec.index_map(b, t, j))

    in_specs = [pl.BlockSpec((bb, tt, k), lambda j, b, t: (b, t, 0))]
    sources = []
    for srcs, _, _, _ in casts:
        for arr, shape, index in srcs:
            in_specs.append(pl.BlockSpec(shape, functools.partial(lambda j, b, t, index: index(j), index=index)))
            sources.append(arr)
    in_specs += [reorder(spec(bb, tt)) for spec in extra_specs]
    out_specs = [pl.BlockSpec((bb, tt, cols_tile), lambda j, b, t: (b, t, j)) for _, _, cols_tile in outs]
    out_shape = [jax.ShapeDtypeStruct((batch, steps, cols), dt) for cols, dt, _ in outs]
    out_specs += [pl.BlockSpec((k, tile), lambda j, b, t: (0, j)) for _, _, _, tile in casts]
    out_shape += [jax.ShapeDtypeStruct((k, total), BF16) for _, _, total, _ in casts]
    counts = [len(srcs) for srcs, _, _, _ in casts]
    transforms = [tr for _, tr, _, _ in casts]
    n_extra, n_out = len(extras), len(outs)

    def casting_body(a_ref, *refs):
        src_refs, pos = [], 0
        for n in counts:
            src_refs.append(refs[pos:pos + n])
            pos += n
        extra_refs = refs[pos:pos + n_extra]
        out_refs = refs[pos + n_extra:pos + n_extra + n_out]
        w_refs = refs[pos + n_extra + n_out:]

        @pl.when((pl.program_id(1) == 0) & (pl.program_id(2) == 0))
        def _():
            for w_ref, srcs, transform in zip(w_refs, src_refs, transforms):
                w_ref[...] = transform(*[s[...] for s in srcs]).astype(w_ref.dtype)

        body(a_ref, *w_refs, *extra_refs, *out_refs)

    return pl.pallas_call(
        casting_body,
        grid=grid,
        in_specs=in_specs,
        out_specs=out_specs,
        out_shape=out_shape,
        compiler_params=_params(("arbitrary", "arbitrary", "arbitrary")),
        name=name,
    )(a, *sources, *extras)


def _rope_specs():
    return [lambda bb, tt: pl.BlockSpec((tt, LANES), lambda b, t, j: (t, 0))] * 3


def _attn_body(q_ref, kc_ref, kp_ref, vc_ref, vp_ref, sink_ref, o_ref, *, mask_first):
    bb, tq, _ = q_ref.shape
    n_chunks = tq // CHUNK
    n_keys = WINDOW + CHUNK
    i = pl.program_id(1)
    h = pl.program_id(2)

    lane = lax.broadcasted_iota(jnp.int32, (1, LANES), 1)
    keep = (lane // HEAD_DIM) == (h % 2)
    sink = sink_ref[0]
    lane_head = lax.broadcasted_iota(jnp.int32, (CHUNK, 2 * LANES), 1) // HEAD_DIM
    out_low = lax.broadcasted_iota(jnp.int32, (CHUNK, LANES), 1) < HEAD_DIM
    key_row = lax.broadcasted_iota(jnp.int32, (n_keys, 1), 0)
    ones = jnp.ones((16, n_keys), BF16)

    def both_halves(cur_ref, prev_ref, bi):
        pair = jnp.concatenate([prev_ref[bi].astype(F32), cur_ref[bi].astype(F32)], axis=0)
        swapped = pltpu.roll(pair, HEAD_DIM, axis=1)
        return jnp.where(keep, pair, swapped)

    def scores(bi, j, k4):
        q = q_ref[bi, j * CHUNK:(j + 1) * CHUNK, :] * (HEAD_DIM ** -0.5)
        parts = []
        for cg in range(GROUP // 4):
            qc = q[:, cg * 2 * LANES:(cg + 1) * 2 * LANES]
            for r in range(4):
                parts.append(jnp.where(lane_head == r, qc, jnp.zeros_like(qc)))
        qrows = jnp.concatenate(parts, axis=0)
        keys = k4[j * CHUNK:j * CHUNK + n_keys]
        st = lax.dot_general(keys, qrows, (((1,), (1,)), ((), ())), preferred_element_type=F32)
        if mask_first and j < WINDOW // CHUNK:
            first_valid = WINDOW - (i * tq + j * CHUNK)
            st = jnp.where(key_row >= first_valid, st, -jnp.inf)
        return st

    def values_t(j, v2):
        vwin = v2[j * CHUNK:j * CHUNK + n_keys]
        head = vwin[:LANES].T
        tail = vwin[LANES:]
        tail = jnp.concatenate([tail, tail], axis=0).T
        vt = jnp.concatenate([head[:HEAD_DIM], tail[:HEAD_DIM, :CHUNK]], axis=1).astype(BF16)
        return jnp.concatenate([vt, ones], axis=0)

    def finish(st, lhs):
        m = jnp.maximum(jnp.max(st, axis=0, keepdims=True), sink)
        pt = jnp.exp(st - m).astype(BF16)
        ot = jnp.dot(lhs, pt, preferred_element_type=F32)
        den = ot[HEAD_DIM:HEAD_DIM + 1] + jnp.exp(sink - m)
        return ot[:HEAD_DIM] / den

    def emit(bi, j, on):
        for pr in range(GROUP // 2):
            x = on[:, pr * LANES:(pr + 1) * LANES]
            xt = jnp.concatenate([x, x], axis=0).T
            o_ref[bi, j * CHUNK:(j + 1) * CHUNK, pr * LANES:(pr + 1) * LANES] = (
                jnp.where(out_low, xt[:CHUNK], xt[CHUNK:]).astype(o_ref.dtype))

    units, sts, lhss = [], [], []
    for bi in range(bb):
        k2 = both_halves(kc_ref, kp_ref, bi)
        v2 = both_halves(vc_ref, vp_ref, bi)
        k4 = jnp.concatenate([k2, k2], axis=1).astype(BF16)
        for j in range(n_chunks):
            units.append((bi, j))
            sts.append(scores(bi, j, k4))
            lhss.append(values_t(j, v2))
    for (bi, j), st, lhs in zip(units, sts, lhss):
        emit(bi, j, finish(st, lhs))


def _attention(q, k_cur, v_cur, k_prev, v_prev, sink_tab, *, bb, tq, mask_first):
    batch, steps, _ = q.shape
    qw = GROUP * HEAD_DIM
    same_array = k_prev is k_cur
    per_tile = tq // WINDOW

    def prev_map(b, i, h):
        if same_array:
            return (b, jnp.maximum(i * per_tile - 1, 0), h // 2)
        return (b, 0, h // 2)

    return pl.pallas_call(
        functools.partial(_attn_body, mask_first=mask_first),
        grid=(batch // bb, steps // tq, N_KV_HEADS),
        in_specs=[pl.BlockSpec((bb, tq, qw), lambda b, i, h: (b, i, h)),
                  pl.BlockSpec((bb, tq, LANES), lambda b, i, h: (b, i, h // 2)),
                  pl.BlockSpec((bb, WINDOW, LANES), prev_map),
                  pl.BlockSpec((bb, tq, LANES), lambda b, i, h: (b, i, h // 2)),
                  pl.BlockSpec((bb, WINDOW, LANES), prev_map),
                  pl.BlockSpec((1, 1, GROUP * CHUNK), lambda b, i, h: (h, 0, 0))],
        out_specs=pl.BlockSpec((bb, tq, qw), lambda b, i, h: (b, i, h)),
        out_shape=jax.ShapeDtypeStruct((batch, steps, ATT_WIDTH), BF16),
        compiler_params=_params(("parallel", "parallel", "parallel")),
        name="attention",
    )(q, k_cur, k_prev, v_cur, v_prev, sink_tab)


def _log_sigmoid(x):
    return jnp.minimum(x, 0.0) - jnp.log1p(jnp.exp(-jnp.abs(x)))


def _mlstm_body(q_ref, k_ref, v_ref, o_ref, g_ref, bias_ref, c0_ref, n0_ref, m0_ref,
                y_ref, c_ref, n_ref, m_ref):
    @pl.when(pl.program_id(1) == 0)
    def _():
        c_ref[...] = c0_ref[...]
        n_ref[...] = n0_ref[...]
        m_ref[...] = m0_ref[...]

    steps = q_ref.shape[1]
    g = g_ref[0] + bias_ref[...]
    li = g[:, :LANES]
    lf = _log_sigmoid(g[:, LANES:])
    row = lax.broadcasted_iota(jnp.int32, (steps, steps), 0)
    col = lax.broadcasted_iota(jnp.int32, (steps, steps), 1)
    causal = col <= row
    b = jnp.dot(causal.astype(F32), lf, preferred_element_type=F32,
                precision=lax.Precision.HIGHEST)
    r = li - b
    m_prev = m_ref[0]
    b_last = b[steps - 1:steps, :]
    m_new = b_last + jnp.maximum(m_prev, jnp.max(r, axis=0, keepdims=True))
    w_state = jnp.exp(r + (b_last - m_new))
    decay = jnp.exp(b_last + m_prev - m_new)
    r_rows = r.T

    for h in range(M_HEADS):
        r_row = r_rows[h:h + 1, :]
        dmat = jnp.where(causal, jnp.broadcast_to(r_row, (steps, steps)), -jnp.inf)
        m_h = m_prev[:, h:h + 1]
        gmax = jnp.maximum(jnp.max(dmat, axis=-1, keepdims=True), m_h)
        w_intra = jnp.exp(dmat - gmax)
        w_inter = jnp.exp(m_h - gmax)
        inv_floor = jnp.exp(-(b[:, h:h + 1] + gmax))

        q = q_ref[0, :, h * M_QK_DIM:(h + 1) * M_QK_DIM] * (M_QK_DIM ** -0.5)
        k = k_ref[0, :, h * M_QK_DIM:(h + 1) * M_QK_DIM]
        v = v_ref[0, :, h * M_V_DIM:(h + 1) * M_V_DIM]
        c_old = c_ref[0, h]
        n_old = n_ref[0, h:h + 1, :]

        qk = lax.dot_general(q, k, (((1,), (1,)), ((), ())), preferred_element_type=F32)
        s = qk * w_intra
        intra = jnp.dot(s.astype(BF16), v, preferred_element_type=F32)
        inter = lax.dot_general(q, c_old.astype(BF16), (((1,), (1,)), ((), ())),
                                preferred_element_type=F32)
        nq = jnp.sum(q.astype(F32) * n_old, axis=-1, keepdims=True)
        num = intra + w_inter * inter
        den = jnp.sum(s, axis=-1, keepdims=True) + w_inter * nq
        hval = num / jnp.maximum(jnp.abs(den), inv_floor)
        og = jax.nn.sigmoid(o_ref[0, :, h * M_V_DIM:(h + 1) * M_V_DIM].astype(F32))
        y_ref[0, :, h * M_V_DIM:(h + 1) * M_V_DIM] = (og * hval).astype(y_ref.dtype)

        ws = w_state[:, h:h + 1]
        vw = (v.astype(F32) * ws).astype(BF16)
        dc = lax.dot_general(vw, k, (((0,), (0,)), ((), ())), preferred_element_type=F32)
        dec = decay[:, h:h + 1]
        c_ref[0, h] = dec * c_old + dc
        n_ref[0, h:h + 1, :] = dec * n_old + jnp.sum(k.astype(F32) * ws, axis=0, keepdims=True)

    m_ref[0] = m_new


def _mlstm(za, zb, gates, bias, c0, n0, m0):
    batch, steps, _ = za.shape
    rows = min(steps, MLSTM_ROWS)
    state_specs = [pl.BlockSpec((1, M_HEADS, M_V_DIM, M_QK_DIM), lambda b, c: (b, 0, 0, 0)),
                   pl.BlockSpec((1, M_HEADS, M_QK_DIM), lambda b, c: (b, 0, 0)),
                   pl.BlockSpec((1, 1, LANES), lambda b, c: (b, 0, 0))]
    return pl.pallas_call(
        _mlstm_body,
        grid=(batch, steps // rows),
        in_specs=[pl.BlockSpec((1, rows, M_QK_WIDTH), lambda b, c: (b, c, 0)),
                  pl.BlockSpec((1, rows, M_QK_WIDTH), lambda b, c: (b, c, 1)),
                  pl.BlockSpec((1, rows, M_V_WIDTH), lambda b, c: (b, c, 1)),
                  pl.BlockSpec((1, rows, M_V_WIDTH), lambda b, c: (b, c, 0)),
                  pl.BlockSpec((1, rows, GATE_PAD), lambda b, c: (b, c, 0)),
                  pl.BlockSpec((1, GATE_PAD), lambda b, c: (0, 0))] + state_specs,
        out_specs=[pl.BlockSpec((1, rows, M_V_WIDTH), lambda b, c: (b, c, 0))] + state_specs,
        out_shape=[jax.ShapeDtypeStruct((batch, steps, M_V_WIDTH), BF16),
                   jax.ShapeDtypeStruct(c0.shape, F32),
                   jax.ShapeDtypeStruct(n0.shape, F32),
                   jax.ShapeDtypeStruct(m0.shape, F32)],
        compiler_params=_params(("parallel", "arbitrary")),
        name="mlstm",
    )(za, za, za, zb, gates, bias, c0, n0, m0)


def _ln_rows(x, g, b):
    mu = jnp.mean(x, axis=-1, keepdims=True)
    xc = x - mu
    var = jnp.mean(xc * xc, axis=-1, keepdims=True)
    return xc * lax.rsqrt(var + LN_EPS) * g + b


def _ln_mod_body(r_ref, g_ref, b_ref, sc_ref, sh_ref, x_ref, u_ref):
    y = _ln_rows(r_ref[...], g_ref[...], b_ref[...])
    x_ref[...] = y
    u_ref[...] = (y * (1.0 + sc_ref[...]) + sh_ref[...]).astype(u_ref.dtype)


def _ln_body(r_ref, g_ref, b_ref, x_ref):
    x_ref[...] = _ln_rows(r_ref[...], g_ref[...], b_ref[...])


def _layer_norm(r, g, b, mod=None, sc_idx=None, sh_idx=None):
    batch, steps, d = r.shape
    bb, tt = _row_blocks(batch, steps, 256)
    act = pl.BlockSpec((bb, tt, d), lambda i, t: (i, t, 0))
    vec = pl.BlockSpec((1, 1, d), lambda i, t: (0, 0, 0))
    g3, b3 = g.reshape(1, 1, d), b.reshape(1, 1, d)
    if mod is None:
        return pl.pallas_call(
            _ln_body, grid=(batch // bb, steps // tt),
            in_specs=[act, vec, vec], out_specs=act,
            out_shape=jax.ShapeDtypeStruct(r.shape, F32),
            compiler_params=_params(("parallel", "parallel")), name="layer_norm",
        )(r, g3, b3)
    return pl.pallas_call(
        _ln_mod_body, grid=(batch // bb, steps // tt),
        in_specs=[act, vec, vec,
                  pl.BlockSpec((bb, 1, d), lambda i, t: (i, 0, sc_idx)),
                  pl.BlockSpec((bb, 1, d), lambda i, t: (i, 0, sh_idx))],
        out_specs=[act, act],
        out_shape=[jax.ShapeDtypeStruct(r.shape, F32), jax.ShapeDtypeStruct(r.shape, BF16)],
        compiler_params=_params(("parallel", "parallel")), name="layer_norm_modulate",
    )(r, g3, b3, mod, mod)


SH1, SC1, GT1, SH2, SC2, GT2 = range(6)


def _rope_tables(pos):
    half = ROT_DIM // 2
    inv = ROPE_THETA ** (-2.0 * jnp.arange(half, dtype=F32) / ROT_DIM)
    ang = pos.astype(F32)[:, None] * inv[None, :]
    cos, sin = jnp.cos(ang), jnp.sin(ang)
    zeros = jnp.zeros_like(cos)
    rest = HEAD_DIM - ROT_DIM
    steps = pos.shape[0]
    one_head = lambda first, second, fill: jnp.concatenate(
        [first, second, jnp.full((steps, rest), fill, F32)], axis=1)
    cos_t = one_head(cos, cos, 1.0)
    up_t = one_head(-sin, zeros, 0.0)
    dn_t = one_head(zeros, sin, 0.0)
    reps = LANES // HEAD_DIM
    return tuple(jnp.tile(t, (1, reps)) for t in (cos_t, up_t, dn_t))


CAST_TILE = 512
FFN_CAST_TILE = 256
GATE_COLS = 2 * M_HEADS
PROJ_HEAD = ATT_WIDTH + 2 * KV_WIDTH + 2 * M_QK_WIDTH + M_V_WIDTH


def _group_layer(x, mod, pos, wts, raw, k_prev, v_prev, state, *, attn_bb, attn_tq, mask_first):
    batch, steps, d = x.shape
    ropes = _rope_tables(pos)
    casting = raw is not None
    tile = CAST_TILE if casting else COL_TILE

    def dense(name, body, act, keys, casts, extras, extra_specs, outs, n_cols, wtile, rows=ROW_TILE):
        n_tiles = -(-n_cols // wtile)
        if not casting:
            weights = [(wts[key], wtile, lambda j: j) for key in keys]
            return _matmul(body, act, weights, extras, extra_specs, outs, n_tiles=n_tiles, rows=rows, name=name)
        res = _matmul_casting(body, act, casts, extras, extra_specs, outs, n_tiles=n_tiles, rows=rows, name=name)
        for key, w in zip(keys, res[len(outs):]):
            wts[key] = w
        return res[0] if len(outs) == 1 else res[:len(outs)]

    def proj_rows(first, total):
        srcs = [(raw["w_in_t"], (1, tile, d), lambda j: (0, first // tile + j, 0))]
        return [(srcs, lambda w: w[0].T, total, tile)]

    def proj_rows_shifted(first, total):
        base = first - GATE_COLS
        srcs = [(raw["w_in_t"], (1, tile, d), lambda j: (0, base // tile + j, 0)),
                (raw["w_in_t"], (1, GATE_COLS, d), lambda j: (0, (base + tile * (j + 1)) // GATE_COLS, 0))]

        def transform(main, nxt):
            rows = jnp.concatenate([main[0], nxt[0]], axis=0)
            return rows[GATE_COLS:GATE_COLS + tile].T

        return [(srcs, transform, total, tile)]

    def columns(key, first, total, width):
        srcs = [(raw[key], (1, d, width), lambda j: (0, 0, first // width + j))]
        return (srcs, lambda w: w[0], total, width)

    u = _modulate(x, mod, SC1, SH1)

    mem_first = ATT_WIDTH + 2 * KV_WIDTH
    za_width = 2 * M_QK_WIDTH + M_V_WIDTH
    zb_width = M_V_WIDTH + 2 * d
    qa = dense("proj_q", _mm_rope_body, u, ["w_qa"], casting and proj_rows(0, ATT_WIDTH),
               ropes, _rope_specs(), [(ATT_WIDTH, BF16, tile)], ATT_WIDTH, tile)
    ka, va, gates = _matmul(
        _mm_kvg_body, u, [(wts["w_kv"], 2 * KV_WIDTH, lambda j: 0), (wts["w_g"], GATE_PAD, lambda j: 0)],
        ropes, _rope_specs(),
        [(KV_WIDTH, F32, KV_WIDTH), (KV_WIDTH, F32, KV_WIDTH), (GATE_PAD, F32, GATE_PAD)],
        n_tiles=1, name="proj_kvg")
    za = dense("proj_mem", _mm_plain_body, u, ["w_mem"], casting and proj_rows(mem_first, za_width),
               [], [], [(za_width, BF16, tile)], za_width, tile)
    zb = dense("proj_gates", _mm_plain_body, u, ["w_gates"],
               casting and proj_rows_shifted(PROJ_HEAD + GATE_COLS, zb_width),
               [], [], [(zb_width, BF16, tile)], zb_width, tile)

    if k_prev is None:
        k_prev, v_prev = ka, va
    ya = _attention(qa, ka, va, k_prev, v_prev, wts["sink_tab"], bb=attn_bb, tq=attn_tq, mask_first=mask_first)

    yb, c_new, n_new, m_new = _mlstm(za, zb, gates, wts["gate_bias"], *state)

    ga_col = M_V_WIDTH // tile
    gb_col = ga_col + d // tile
    t = dense("merge_a", _mm_gate_first_body, ya, ["w_up_a"], casting and [columns("w_up_a", 0, d, tile)],
              [zb], [lambda bb, tt: _act_spec(bb, tt, tile, ga_col)], [(d, F32, tile)], d, tile)
    merged = dense("merge_b", _mm_gate_second_body, yb, ["w_up_b"], casting and [columns("w_up_b", 0, d, tile)],
                   [zb, t], [lambda bb, tt: _act_spec(bb, tt, tile, gb_col),
                             lambda bb, tt: _act_spec(bb, tt, tile)], [(d, BF16, tile)], d, tile)

    def mod_spec(idx, width):
        per_vec = d // width
        return lambda bb, tt: pl.BlockSpec((bb, 1, width), lambda b, t_, j: (b, 0, idx * per_vec + j))

    r1 = dense("out_proj", _mm_residual_body, merged, ["w_o"], casting and [columns("w_o", 0, d, tile)],
               [x, mod], [lambda bb, tt: _act_spec(bb, tt, tile), mod_spec(GT1, tile)],
               [(d, F32, tile)], d, tile)
    x1, u2 = _layer_norm(r1, wts["ln1_g"], wts["ln1_b"], mod, SC2, SH2)

    ff_tile = FFN_CAST_TILE if casting else COL_TILE // 2
    ffn_casts = casting and [columns("w_ffn_in", 0, D_FF, ff_tile), columns("w_ffn_in", D_FF, D_FF, ff_tile)]
    hidden = dense("ffn_in", _mm_swiglu_body, u2, ["w_ffn_gate", "w_ffn_up"], ffn_casts,
                   [], [], [(D_FF, BF16, ff_tile)], D_FF, ff_tile)

    out_tile = COL_TILE // 2
    r2 = _matmul(_mm_residual_body, hidden, [(wts["w_ffn_out"], out_tile, lambda j: j)], [x1, mod],
                 [lambda bb, tt: _act_spec(bb, tt, out_tile), mod_spec(GT2, out_tile)],
                 [(d, F32, out_tile)], n_tiles=d // out_tile, rows=ROW_TILE // 2, name="ffn_out")
    y = _layer_norm(r2, wts["ln2_g"], wts["ln2_b"])
    return y, ka, va, c_new, n_new, m_new


def _cast_body(w_ref, o_ref):
    o_ref[...] = w_ref[0].T.astype(o_ref.dtype)


def _cast_gates_body(w_ref, o_ref):
    x = w_ref[0].T
    valid = lax.broadcasted_iota(jnp.int32, (1, LANES), 1) < M_HEADS
    o_ref[:, :LANES] = jnp.where(valid, x, 0.0).astype(o_ref.dtype)
    f_first = pltpu.roll(x, LANES - M_HEADS, axis=1)
    o_ref[:, LANES:] = jnp.where(valid, f_first, 0.0).astype(o_ref.dtype)


def _cast_kv_gates(w_in_t):
    _, _, k = w_in_t.shape
    params = _params(("parallel",))
    first = ATT_WIDTH // CAST_TILE
    w_kv = pl.pallas_call(
        _cast_body, grid=(2 * KV_WIDTH // CAST_TILE,),
        in_specs=[pl.BlockSpec((1, CAST_TILE, k), lambda j: (0, first + j, 0))],
        out_specs=pl.BlockSpec((k, CAST_TILE), lambda j: (0, j)),
        out_shape=jax.ShapeDtypeStruct((k, 2 * KV_WIDTH), BF16),
        compiler_params=params, name="cast_proj_kv",
    )(w_in_t)
    w_g = pl.pallas_call(
        _cast_gates_body, grid=(1,),
        in_specs=[pl.BlockSpec((1, LANES, k), lambda j: (0, PROJ_HEAD // LANES, 0))],
        out_specs=pl.BlockSpec((k, GATE_PAD), lambda j: (0, 0)),
        out_shape=jax.ShapeDtypeStruct((k, GATE_PAD), BF16),
        compiler_params=params, name="cast_proj_gates",
    )(w_in_t)
    return w_kv, w_g


def kernel(x_prompt, x_sample, cache_k_win, cache_v_win, state_C, state_n, state_m, c_prompt, c_sample, w_ada, b_ada, w_in, b_if, attn_sinks, w_up_a, w_up_b, w_o, ln1_g, ln1_b, w_ffn_in, w_ffn_out, ln2_g, ln2_b):
    bp, sp, d = x_prompt.shape
    bs, ts, _ = x_sample.shape
    keep = cache_k_win.shape[2]
    assert w_in.shape[0] == DEPTH == 1
    l = 0

    raw = {"w_in_t": jnp.swapaxes(w_in, 1, 2), "w_up_a": w_up_a, "w_up_b": w_up_b, "w_o": w_o,
           "w_ffn_in": w_ffn_in}
    w_kv, w_g = _cast_kv_gates(raw["w_in_t"])
    pad_bias = jnp.zeros((LANES - M_HEADS,), F32)
    wts = {
        "w_kv": w_kv, "w_g": w_g,
        "gate_bias": jnp.concatenate([b_if[l, :M_HEADS], pad_bias, b_if[l, M_HEADS:], pad_bias]).reshape(1, GATE_PAD),
        "sink_tab": jnp.broadcast_to(attn_sinks[l].astype(F32).reshape(N_KV_HEADS, 1, GROUP, 1),
                                     (N_KV_HEADS, 1, GROUP, CHUNK)).reshape(N_KV_HEADS, 1, GROUP * CHUNK),
        "w_ffn_out": w_ffn_out[l].astype(BF16),
        "ln1_g": ln1_g[l], "ln1_b": ln1_b[l], "ln2_g": ln2_g[l], "ln2_b": ln2_b[l],
    }

    c_all = jnp.concatenate([c_prompt, c_sample], axis=0)
    rows = c_all.shape[0]
    rows_pad = -(-rows // 8) * 8
    c_all = jnp.pad(c_all, ((0, rows_pad - rows), (0, 0)))
    mod = _adaln(c_all, w_ada[l], b_ada[l])
    mod_p = mod[:bp].reshape(bp, 1, 6 * d)
    mod_s = mod[bp:bp + bs].reshape(bs, 1, 6 * d)

    def pad_lanes(m):
        return jnp.pad(m, ((0, 0), (0, LANES - M_HEADS))).reshape(m.shape[0], 1, LANES)

    ck = cache_k_win[l].reshape(bs, keep, KV_WIDTH)
    cv = cache_v_win[l].reshape(bs, keep, KV_WIDTH)
    state_s = (state_C[l], state_n[l], pad_lanes(state_m[l]))
    ys, kas, vas, cs, ns, ms = _group_layer(
        x_sample, mod_s, PAST_LEN + jnp.arange(ts), wts, raw, ck, cv, state_s,
        attn_bb=min(bs, ATTN_ROWS // ts), attn_tq=ts, mask_first=False)

    state_p = (jnp.zeros((bp, M_HEADS, M_V_DIM, M_QK_DIM), F32),
               jnp.zeros((bp, M_HEADS, M_QK_DIM), F32),
               pad_lanes(jnp.full((bp, M_HEADS), M_INIT, F32)))
    yp, kap, vap, cp, np_, mp = _group_layer(
        x_prompt, mod_p, jnp.arange(sp), wts, None, None, None, state_p,
        attn_bb=1, attn_tq=min(ATTN_ROWS, sp), mask_first=True)

    def heads(a):
        return a.reshape(a.shape[0], a.shape[1], N_KV_HEADS, HEAD_DIM)

    kws = jnp.concatenate([ck, kas], axis=1)[:, -keep:]
    vws = jnp.concatenate([cv, vas], axis=1)[:, -keep:]
    return (yp, ys,
            heads(kap[:, -keep:])[None], heads(vap[:, -keep:])[None],
            cp[None], np_[None], mp[:, 0, :M_HEADS][None],
            heads(kws)[None], heads(vws)[None],
            cs[None], ns[None], ms[:, 0, :M_HEADS][None])
```

```python
import functools

import jax
import jax.numpy as jnp
from jax import lax
from jax.experimental import pallas as pl
from jax.experimental.pallas import tpu as pltpu

F32 = jnp.float32
BF16 = jnp.bfloat16

D_MODEL = 4096
CHUNK = 64
N_Q_HEADS = 64
N_KV_HEADS = 8
HEAD_DIM = 64
GROUP = N_Q_HEADS // N_KV_HEADS
WINDOW = 128
ROT_DIM = HEAD_DIM // 4
ROPE_THETA = 500000.0
M_HEADS = 8
M_QK_DIM = 256
M_V_DIM = 512
ATT_WIDTH = N_Q_HEADS * HEAD_DIM
KV_WIDTH = N_KV_HEADS * HEAD_DIM
M_QK_WIDTH = M_HEADS * M_QK_DIM
M_V_WIDTH = M_HEADS * M_V_DIM
D_FF = 11008
DEPTH = 1
PAST_LEN = 2048
ALPHA = (2.0 * DEPTH) ** 0.25
LN_EPS = 1e-5
M_INIT = -1e30

LANES = 128
ROW_TILE = 1024
COL_TILE = 1024
ATTN_ROWS = 1024
MLSTM_ROWS = 256
GATE_PAD = 2 * LANES
VMEM_LIMIT = 60 * 1024 * 1024


def _params(semantics):
    return pltpu.CompilerParams(dimension_semantics=semantics, vmem_limit_bytes=VMEM_LIMIT)


def _row_blocks(batch, steps, rows):
    if steps >= rows:
        assert steps % rows == 0
        return 1, rows
    bb = min(batch, rows // steps)
    assert batch % bb == 0
    return bb, steps


def _adaln_body(c_ref, w_ref, b_ref, o_ref):
    c = c_ref[...]
    s = c * jax.nn.sigmoid(c)
    o_ref[...] = jnp.dot(s, w_ref[...], preferred_element_type=F32) + b_ref[...]


def _adaln(c, w_ada, b_ada):
    rows, d = c.shape
    n = w_ada.shape[1]
    tn = 512
    return pl.pallas_call(
        _adaln_body,
        grid=(n // tn,),
        in_specs=[pl.BlockSpec((rows, d), lambda j: (0, 0)),
                  pl.BlockSpec((d, tn), lambda j: (0, j)),
                  pl.BlockSpec((1, tn), lambda j: (0, j))],
        out_specs=pl.BlockSpec((rows, tn), lambda j: (0, j)),
        out_shape=jax.ShapeDtypeStruct((rows, n), F32),
        compiler_params=_params(("parallel",)),
        name="adaln",
    )(c, w_ada, b_ada.reshape(1, n))


def _modulate_body(x_ref, sc_ref, sh_ref, o_ref):
    o_ref[...] = (x_ref[...] * (1.0 + sc_ref[...]) + sh_ref[...]).astype(o_ref.dtype)


def _modulate(x, mod, sc_idx, sh_idx):
    batch, steps, d = x.shape
    bb, tt = _row_blocks(batch, steps, 512)
    return pl.pallas_call(
        _modulate_body,
        grid=(batch // bb, steps // tt),
        in_specs=[pl.BlockSpec((bb, tt, d), lambda b, t: (b, t, 0)),
                  pl.BlockSpec((bb, 1, d), lambda b, t: (b, 0, sc_idx)),
                  pl.BlockSpec((bb, 1, d), lambda b, t: (b, 0, sh_idx))],
        out_specs=pl.BlockSpec((bb, tt, d), lambda b, t: (b, t, 0)),
        out_shape=jax.ShapeDtypeStruct((batch, steps, d), BF16),
        compiler_params=_params(("parallel", "parallel")),
        name="modulate",
    )(x, mod, mod)


def _load_rows(a_ref):
    bb, tt, k = a_ref.shape
    return a_ref[...].reshape(bb * tt, k)


def _rope_rows(x, cos_ref, sa_ref, sb_ref, bb):
    rows, n = x.shape
    tt = rows // bb

    def table(ref):
        t = ref[...]
        if bb == 1:
            return t
        return jnp.broadcast_to(t[None], (bb, tt, LANES)).reshape(rows, LANES)

    cos, sa, sb = table(cos_ref), table(sa_ref), table(sb_ref)
    outs = []
    for s in range(n // LANES):
        xs = x[:, s * LANES:(s + 1) * LANES]
        up = pltpu.roll(xs, LANES - ROT_DIM // 2, axis=1)
        dn = pltpu.roll(xs, ROT_DIM // 2, axis=1)
        outs.append(xs * cos + up * sa + dn * sb)
    return jnp.concatenate(outs, axis=1)


def _mm_plain_body(a_ref, w_ref, o_ref):
    acc = jnp.dot(_load_rows(a_ref), w_ref[...], preferred_element_type=F32)
    o_ref[...] = acc.reshape(o_ref.shape).astype(o_ref.dtype)


def _mm_rope_body(a_ref, w_ref, cos_ref, sa_ref, sb_ref, o_ref):
    acc = jnp.dot(_load_rows(a_ref), w_ref[...], preferred_element_type=F32)
    out = _rope_rows(acc, cos_ref, sa_ref, sb_ref, a_ref.shape[0])
    o_ref[...] = out.reshape(o_ref.shape).astype(o_ref.dtype)


def _mm_kvg_body(a_ref, w_ref, wg_ref, cos_ref, sa_ref, sb_ref, k_ref, v_ref, g_ref):
    a = _load_rows(a_ref)
    acc = jnp.dot(a, w_ref[...], preferred_element_type=F32)
    k = _rope_rows(acc[:, :KV_WIDTH], cos_ref, sa_ref, sb_ref, a_ref.shape[0])
    k_ref[...] = k.reshape(k_ref.shape)
    v_ref[...] = acc[:, KV_WIDTH:].reshape(v_ref.shape)
    g_ref[...] = jnp.dot(a, wg_ref[...], preferred_element_type=F32).reshape(g_ref.shape)


def _mm_gate_first_body(a_ref, w_ref, g_ref, o_ref):
    acc = jnp.dot(_load_rows(a_ref), w_ref[...], preferred_element_type=F32)
    gate = jax.nn.sigmoid(g_ref[...].astype(F32))
    o_ref[...] = gate * acc.reshape(o_ref.shape)


def _mm_gate_second_body(a_ref, w_ref, g_ref, t_ref, o_ref):
    acc = jnp.dot(_load_rows(a_ref), w_ref[...], preferred_element_type=F32)
    gate = jax.nn.sigmoid(g_ref[...].astype(F32))
    o_ref[...] = (t_ref[...] + gate * acc.reshape(o_ref.shape)).astype(o_ref.dtype)


def _mm_residual_body(a_ref, w_ref, x_ref, g_ref, o_ref):
    acc = jnp.dot(_load_rows(a_ref), w_ref[...], preferred_element_type=F32)
    o_ref[...] = ALPHA * x_ref[...] + g_ref[...] * acc.reshape(o_ref.shape)


def _mm_swiglu_body(a_ref, wg_ref, wu_ref, o_ref):
    a = _load_rows(a_ref)
    gate = jnp.dot(a, wg_ref[...], preferred_element_type=F32)
    up = jnp.dot(a, wu_ref[...], preferred_element_type=F32)
    out = gate * jax.nn.sigmoid(gate) * up
    o_ref[...] = out.reshape(o_ref.shape).astype(o_ref.dtype)


def _act_spec(bb, tt, width, col=None):
    if col is None:
        return pl.BlockSpec((bb, tt, width), lambda b, t, j: (b, t, j))
    return pl.BlockSpec((bb, tt, width), lambda b, t, j: (b, t, j + col))


def _matmul(body, a, weights, extras, extra_specs, outs, *, n_tiles, rows=ROW_TILE, name):
    batch, steps, k = a.shape
    bb, tt = _row_blocks(batch, steps, rows)
    grid = (batch // bb, steps // tt, n_tiles)
    in_specs = [pl.BlockSpec((bb, tt, k), lambda b, t, j: (b, t, 0))]
    in_specs += [pl.BlockSpec((k, cols), functools.partial(lambda b, t, j, col: (0, col(j)), col=col))
                 for _, cols, col in weights]
    in_specs += [spec(bb, tt) for spec in extra_specs]
    weights = [w for w, _, _ in weights]
    out_specs = [pl.BlockSpec((bb, tt, cols_tile), lambda b, t, j: (b, t, j)) for _, _, cols_tile in outs]
    out_shape = [jax.ShapeDtypeStruct((batch, steps, cols), dt) for cols, dt, _ in outs]
    single = len(outs) == 1
    return pl.pallas_call(
        body,
        grid=grid,
        in_specs=in_specs,
        out_specs=out_specs[0] if single else out_specs,
        out_shape=out_shape[0] if single else out_shape,
        compiler_params=_params(("parallel", "parallel", "parallel")),
        name=name,
    )(a, *weights, *extras)


def _matmul_casting(body, a, casts, extras, extra_specs, outs, *, n_tiles, name):
    batch, steps, k = a.shape

    def per_tile(spec):
        return pl.BlockSpec(spec.block_shape, lambda j: spec.index_map(0, 0, j))

    in_specs = [pl.BlockSpec((batch, steps, k), lambda j: (0, 0, 0), pipeline_mode=pl.Buffered(1))]
    sources = []
    for srcs, _, _, _ in casts:
        for arr, shape, index in srcs:
            in_specs.append(pl.BlockSpec(shape, index))
            sources.append(arr)
    in_specs += [per_tile(spec(batch, steps)) for spec in extra_specs]
    out_specs = [pl.BlockSpec((batch, steps, cols_tile), lambda j: (0, 0, j)) for _, _, cols_tile in outs]
    out_shape = [jax.ShapeDtypeStruct((batch, steps, cols), dt) for cols, dt, _ in outs]
    out_specs += [pl.BlockSpec((k, tile), lambda j: (0, j)) for _, _, _, tile in casts]
    out_shape += [jax.ShapeDtypeStruct((k, total), BF16) for _, _, total, _ in casts]
    counts = [len(srcs) for srcs, _, _, _ in casts]
    transforms = [tr for _, tr, _, _ in casts]
    n_extra, n_out = len(extras), len(outs)

    def casting_body(a_ref, *refs):
        src_refs, pos = [], 0
        for n in counts:
            src_refs.append(refs[pos:pos + n])
            pos += n
        extra_refs = refs[pos:pos + n_extra]
        out_refs = refs[pos + n_extra:pos + n_extra + n_out]
        w_refs = refs[pos + n_extra + n_out:]
        for w_ref, srcs, transform in zip(w_refs, src_refs, transforms):
            w_ref[...] = transform(*[s[...] for s in srcs]).astype(w_ref.dtype)
        body(a_ref, *w_refs, *extra_refs, *out_refs)

    return pl.pallas_call(
        casting_body,
        grid=(n_tiles,),
        in_specs=in_specs,
        out_specs=out_specs,
        out_shape=out_shape,
        compiler_params=_params(("arbitrary",)),
        name=name,
    )(a, *sources, *extras)


def _rope_specs():
    return [lambda bb, tt: pl.BlockSpec((tt, LANES), lambda b, t, j: (t, 0))] * 3


def _attn_body(q_ref, kc_ref, kp_ref, vc_ref, vp_ref, sink_ref, o_ref, *, mask_first):
    bb, tq, _ = q_ref.shape
    n_chunks = tq // CHUNK
    n_keys = WINDOW + CHUNK
    i = pl.program_id(1)
    h = pl.program_id(2)

    lane = lax.broadcasted_iota(jnp.int32, (1, LANES), 1)
    keep = (lane // HEAD_DIM) == (h % 2)
    sink = sink_ref[0]
    lane_head = lax.broadcasted_iota(jnp.int32, (CHUNK, 2 * LANES), 1) // HEAD_DIM
    out_low = lax.broadcasted_iota(jnp.int32, (CHUNK, LANES), 1) < HEAD_DIM
    key_row = lax.broadcasted_iota(jnp.int32, (n_keys, 1), 0)
    ones = jnp.ones((16, n_keys), BF16)

    def both_halves(cur_ref, prev_ref, bi):
        pair = jnp.concatenate([prev_ref[bi].astype(F32), cur_ref[bi].astype(F32)], axis=0)
        swapped = pltpu.roll(pair, HEAD_DIM, axis=1)
        return jnp.where(keep, pair, swapped)

    def scores(bi, j, k4):
        q = q_ref[bi, j * CHUNK:(j + 1) * CHUNK, :] * (HEAD_DIM ** -0.5)
        parts = []
        for cg in range(GROUP // 4):
            qc = q[:, cg * 2 * LANES:(cg + 1) * 2 * LANES]
            for r in range(4):
                parts.append(jnp.where(lane_head == r, qc, jnp.zeros_like(qc)))
        qrows = jnp.concatenate(parts, axis=0)
        keys = k4[j * CHUNK:j * CHUNK + n_keys]
        st = lax.dot_general(keys, qrows, (((1,), (1,)), ((), ())), preferred_element_type=F32)
        if mask_first and j < WINDOW // CHUNK:
            first_valid = WINDOW - (i * tq + j * CHUNK)
            st = jnp.where(key_row >= first_valid, st, -jnp.inf)
        return st

    def values_t(j, v2):
        vwin = v2[j * CHUNK:j * CHUNK + n_keys]
        head = vwin[:LANES].T
        tail = vwin[LANES:]
        tail = jnp.concatenate([tail, tail], axis=0).T
        vt = jnp.concatenate([head[:HEAD_DIM], tail[:HEAD_DIM, :CHUNK]], axis=1).astype(BF16)
        return jnp.concatenate([vt, ones], axis=0)

    def finish(st, lhs):
        m = jnp.maximum(jnp.max(st, axis=0, keepdims=True), sink)
        pt = jnp.exp(st - m).astype(BF16)
        ot = jnp.dot(lhs, pt, preferred_element_type=F32)
        den = ot[HEAD_DIM:HEAD_DIM + 1] + jnp.exp(sink - m)
        return ot[:HEAD_DIM] / den

    def emit(bi, j, on):
        for pr in range(GROUP // 2):
            x = on[:, pr * LANES:(pr + 1) * LANES]
            xt = jnp.concatenate([x, x], axis=0).T
            o_ref[bi, j * CHUNK:(j + 1) * CHUNK, pr * LANES:(pr + 1) * LANES] = (
                jnp.where(out_low, xt[:CHUNK], xt[CHUNK:]).astype(o_ref.dtype))

    units, sts, lhss = [], [], []
    for bi in range(bb):
        k2 = both_halves(kc_ref, kp_ref, bi)
        v2 = both_halves(vc_ref, vp_ref, bi)
        k4 = jnp.concatenate([k2, k2], axis=1).astype(BF16)
        for j in range(n_chunks):
            units.append((bi, j))
            sts.append(scores(bi, j, k4))
            lhss.append(values_t(j, v2))
    for (bi, j), st, lhs in zip(units, sts, lhss):
        emit(bi, j, finish(st, lhs))


def _attention(q, k_cur, v_cur, k_prev, v_prev, sink_tab, *, bb, tq, mask_first):
    batch, steps, _ = q.shape
    qw = GROUP * HEAD_DIM
    same_array = k_prev is k_cur
    per_tile = tq // WINDOW

    def prev_map(b, i, h):
        if same_array:
            return (b, jnp.maximum(i * per_tile - 1, 0), h // 2)
        return (b, 0, h // 2)

    return pl.pallas_call(
        functools.partial(_attn_body, mask_first=mask_first),
        grid=(batch // bb, steps // tq, N_KV_HEADS),
        in_specs=[pl.BlockSpec((bb, tq, qw), lambda b, i, h: (b, i, h)),
                  pl.BlockSpec((bb, tq, LANES), lambda b, i, h: (b, i, h // 2)),
                  pl.BlockSpec((bb, WINDOW, LANES), prev_map),
                  pl.BlockSpec((bb, tq, LANES), lambda b, i, h: (b, i, h // 2)),
                  pl.BlockSpec((bb, WINDOW, LANES), prev_map),
                  pl.BlockSpec((1, 1, GROUP * CHUNK), lambda b, i, h: (h, 0, 0))],
        out_specs=pl.BlockSpec((bb, tq, qw), lambda b, i, h: (b, i, h)),
        out_shape=jax.ShapeDtypeStruct((batch, steps, ATT_WIDTH), BF16),
        compiler_params=_params(("parallel", "parallel", "parallel")),
        name="attention",
    )(q, k_cur, k_prev, v_cur, v_prev, sink_tab)


def _log_sigmoid(x):
    return jnp.minimum(x, 0.0) - jnp.log1p(jnp.exp(-jnp.abs(x)))


def _mlstm_body(q_ref, k_ref, v_ref, o_ref, g_ref, bias_ref, c0_ref, n0_ref, m0_ref,
                y_ref, c_ref, n_ref, m_ref):
    @pl.when(pl.program_id(1) == 0)
    def _():
        c_ref[...] = c0_ref[...]
        n_ref[...] = n0_ref[...]
        m_ref[...] = m0_ref[...]

    steps = q_ref.shape[1]
    g = g_ref[0] + bias_ref[...]
    li = g[:, :LANES]
    lf = _log_sigmoid(g[:, LANES:])
    row = lax.broadcasted_iota(jnp.int32, (steps, steps), 0)
    col = lax.broadcasted_iota(jnp.int32, (steps, steps), 1)
    causal = col <= row
    b = jnp.dot(causal.astype(F32), lf, preferred_element_type=F32,
                precision=lax.Precision.HIGHEST)
    r = li - b
    m_prev = m_ref[0]
    b_last = b[steps - 1:steps, :]
    m_new = b_last + jnp.maximum(m_prev, jnp.max(r, axis=0, keepdims=True))
    w_state = jnp.exp(r + (b_last - m_new))
    decay = jnp.exp(b_last + m_prev - m_new)
    r_rows = r.T

    for h in range(M_HEADS):
        r_row = r_rows[h:h + 1, :]
        dmat = jnp.where(causal, jnp.broadcast_to(r_row, (steps, steps)), -jnp.inf)
        m_h = m_prev[:, h:h + 1]
        gmax = jnp.maximum(jnp.max(dmat, axis=-1, keepdims=True), m_h)
        w_intra = jnp.exp(dmat - gmax)
        w_inter = jnp.exp(m_h - gmax)
        inv_floor = jnp.exp(-(b[:, h:h + 1] + gmax))

        q = q_ref[0, :, h * M_QK_DIM:(h + 1) * M_QK_DIM] * (M_QK_DIM ** -0.5)
        k = k_ref[0, :, h * M_QK_DIM:(h + 1) * M_QK_DIM]
        v = v_ref[0, :, h * M_V_DIM:(h + 1) * M_V_DIM]
        c_old = c_ref[0, h]
        n_old = n_ref[0, h:h + 1, :]

        qk = lax.dot_general(q, k, (((1,), (1,)), ((), ())), preferred_element_type=F32)
        s = qk * w_intra
        intra = jnp.dot(s.astype(BF16), v, preferred_element_type=F32)
        inter = lax.dot_general(q, c_old.astype(BF16), (((1,), (1,)), ((), ())),
                                preferred_element_type=F32)
        nq = jnp.sum(q.astype(F32) * n_old, axis=-1, keepdims=True)
        num = intra + w_inter * inter
        den = jnp.sum(s, axis=-1, keepdims=True) + w_inter * nq
        hval = num / jnp.maximum(jnp.abs(den), inv_floor)
        og = jax.nn.sigmoid(o_ref[0, :, h * M_V_DIM:(h + 1) * M_V_DIM].astype(F32))
        y_ref[0, :, h * M_V_DIM:(h + 1) * M_V_DIM] = (og * hval).astype(y_ref.dtype)

        ws = w_state[:, h:h + 1]
        vw = (v.astype(F32) * ws).astype(BF16)
        dc = lax.dot_general(vw, k, (((0,), (0,)), ((), ())), preferred_element_type=F32)
        dec = decay[:, h:h + 1]
        c_ref[0, h] = dec * c_old + dc
        n_ref[0, h:h + 1, :] = dec * n_old + jnp.sum(k.astype(F32) * ws, axis=0, keepdims=True)

    m_ref[0] = m_new


def _mlstm(za, zb, gates, bias, c0, n0, m0):
    batch, steps, _ = za.shape
    rows = min(steps, MLSTM_ROWS)
    state_specs = [pl.BlockSpec((1, M_HEADS, M_V_DIM, M_QK_DIM), lambda b, c: (b, 0, 0, 0)),
                   pl.BlockSpec((1, M_HEADS, M_QK_DIM), lambda b, c: (b, 0, 0)),
                   pl.BlockSpec((1, 1, LANES), lambda b, c: (b, 0, 0))]
    return pl.pallas_call(
        _mlstm_body,
        grid=(batch, steps // rows),
        in_specs=[pl.BlockSpec((1, rows, M_QK_WIDTH), lambda b, c: (b, c, 0)),
                  pl.BlockSpec((1, rows, M_QK_WIDTH), lambda b, c: (b, c, 1)),
                  pl.BlockSpec((1, rows, M_V_WIDTH), lambda b, c: (b, c, 1)),
                  pl.BlockSpec((1, rows, M_V_WIDTH), lambda b, c: (b, c, 0)),
                  pl.BlockSpec((1, rows, GATE_PAD), lambda b, c: (b, c, 0)),
                  pl.BlockSpec((1, GATE_PAD), lambda b, c: (0, 0))] + state_specs,
        out_specs=[pl.BlockSpec((1, rows, M_V_WIDTH), lambda b, c: (b, c, 0))] + state_specs,
        out_shape=[jax.ShapeDtypeStruct((batch, steps, M_V_WIDTH), BF16),
                   jax.ShapeDtypeStruct(c0.shape, F32),
                   jax.ShapeDtypeStruct(n0.shape, F32),
                   jax.ShapeDtypeStruct(m0.shape, F32)],
        compiler_params=_params(("parallel", "arbitrary")),
        name="mlstm",
    )(za, za, za, zb, gates, bias, c0, n0, m0)


def _ln_rows(x, g, b):
    mu = jnp.mean(x, axis=-1, keepdims=True)
    xc = x - mu
    var = jnp.mean(xc * xc, axis=-1, keepdims=True)
    return xc * lax.rsqrt(var + LN_EPS) * g + b


def _ln_mod_body(r_ref, g_ref, b_ref, sc_ref, sh_ref, x_ref, u_ref):
    y = _ln_rows(r_ref[...], g_ref[...], b_ref[...])
    x_ref[...] = y
    u_ref[...] = (y * (1.0 + sc_ref[...]) + sh_ref[...]).astype(u_ref.dtype)


def _ln_body(r_ref, g_ref, b_ref, x_ref):
    x_ref[...] = _ln_rows(r_ref[...], g_ref[...], b_ref[...])


def _layer_norm(r, g, b, mod=None, sc_idx=None, sh_idx=None):
    batch, steps, d = r.shape
    bb, tt = _row_blocks(batch, steps, 256)
    act = pl.BlockSpec((bb, tt, d), lambda i, t: (i, t, 0))
    vec = pl.BlockSpec((1, 1, d), lambda i, t: (0, 0, 0))
    g3, b3 = g.reshape(1, 1, d), b.reshape(1, 1, d)
    if mod is None:
        return pl.pallas_call(
            _ln_body, grid=(batch // bb, steps // tt),
            in_specs=[act, vec, vec], out_specs=act,
            out_shape=jax.ShapeDtypeStruct(r.shape, F32),
            compiler_params=_params(("parallel", "parallel")), name="layer_norm",
        )(r, g3, b3)
    return pl.pallas_call(
        _ln_mod_body, grid=(batch // bb, steps // tt),
        in_specs=[act, vec, vec,
                  pl.BlockSpec((bb, 1, d), lambda i, t: (i, 0, sc_idx)),
                  pl.BlockSpec((bb, 1, d), lambda i, t: (i, 0, sh_idx))],
        out_specs=[act, act],
        out_shape=[jax.ShapeDtypeStruct(r.shape, F32), jax.ShapeDtypeStruct(r.shape, BF16)],
        compiler_params=_params(("parallel", "parallel")), name="layer_norm_modulate",
    )(r, g3, b3, mod, mod)


SH1, SC1, GT1, SH2, SC2, GT2 = range(6)


def _rope_tables(pos):
    half = ROT_DIM // 2
    inv = ROPE_THETA ** (-2.0 * jnp.arange(half, dtype=F32) / ROT_DIM)
    ang = pos.astype(F32)[:, None] * inv[None, :]
    cos, sin = jnp.cos(ang), jnp.sin(ang)
    zeros = jnp.zeros_like(cos)
    rest = HEAD_DIM - ROT_DIM
    steps = pos.shape[0]
    one_head = lambda first, second, fill: jnp.concatenate(
        [first, second, jnp.full((steps, rest), fill, F32)], axis=1)
    cos_t = one_head(cos, cos, 1.0)
    up_t = one_head(-sin, zeros, 0.0)
    dn_t = one_head(zeros, sin, 0.0)
    reps = LANES // HEAD_DIM
    return tuple(jnp.tile(t, (1, reps)) for t in (cos_t, up_t, dn_t))


CAST_TILE = 512
FFN_CAST_TILE = 256
MERGE_CAST_TILE = 256
GATE_COLS = 2 * M_HEADS
PROJ_HEAD = ATT_WIDTH + 2 * KV_WIDTH + 2 * M_QK_WIDTH + M_V_WIDTH


def _group_layer(x, mod, pos, wts, raw, k_prev, v_prev, state, *, attn_bb, attn_tq, mask_first):
    batch, steps, d = x.shape
    ropes = _rope_tables(pos)
    casting = raw is not None
    tile = CAST_TILE if casting else COL_TILE

    def dense(name, body, act, keys, casts, extras, extra_specs, outs, n_cols, wtile, rows=ROW_TILE):
        n_tiles = -(-n_cols // wtile)
        if not casting:
            weights = [(wts[key], wtile, lambda j: j) for key in keys]
            return _matmul(body, act, weights, extras, extra_specs, outs, n_tiles=n_tiles, rows=rows, name=name)
        res = _matmul_casting(body, act, casts, extras, extra_specs, outs, n_tiles=n_tiles, name=name)
        for key, w in zip(keys, res[len(outs):]):
            wts[key] = w
        return res[0] if len(outs) == 1 else res[:len(outs)]

    def proj_rows(first, total):
        srcs = [(raw["w_in_t"], (1, tile, d), lambda j: (0, first // tile + j, 0))]
        return [(srcs, lambda w: w[0].T, total, tile)]

    def proj_rows_shifted(first, total):
        base = first - GATE_COLS
        srcs = [(raw["w_in_t"], (1, tile, d), lambda j: (0, base // tile + j, 0)),
                (raw["w_in_t"], (1, GATE_COLS, d), lambda j: (0, (base + tile * (j + 1)) // GATE_COLS, 0))]

        def transform(main, nxt):
            rows = jnp.concatenate([main[0], nxt[0]], axis=0)
            return rows[GATE_COLS:GATE_COLS + tile].T

        return [(srcs, transform, total, tile)]

    def columns(key, first, total, width):
        srcs = [(raw[key], (1, d, width), lambda j: (0, 0, first // width + j))]
        return (srcs, lambda w: w[0], total, width)

    u = _modulate(x, mod, SC1, SH1)

    mem_first = ATT_WIDTH + 2 * KV_WIDTH
    za_width = 2 * M_QK_WIDTH + M_V_WIDTH
    zb_width = M_V_WIDTH + 2 * d
    qa = dense("proj_q", _mm_rope_body, u, ["w_qa"], casting and proj_rows(0, ATT_WIDTH),
               ropes, _rope_specs(), [(ATT_WIDTH, BF16, tile)], ATT_WIDTH, tile)
    ka, va, gates = _matmul(
        _mm_kvg_body, u, [(wts["w_kv"], 2 * KV_WIDTH, lambda j: 0), (wts["w_g"], GATE_PAD, lambda j: 0)],
        ropes, _rope_specs(),
        [(KV_WIDTH, F32, KV_WIDTH), (KV_WIDTH, F32, KV_WIDTH), (GATE_PAD, F32, GATE_PAD)],
        n_tiles=1, name="proj_kvg")
    za = dense("proj_mem", _mm_plain_body, u, ["w_mem"], casting and proj_rows(mem_first, za_width),
               [], [], [(za_width, BF16, tile)], za_width, tile)
    zb = dense("proj_gates", _mm_plain_body, u, ["w_gates"],
               casting and proj_rows_shifted(PROJ_HEAD + GATE_COLS, zb_width),
               [], [], [(zb_width, BF16, tile)], zb_width, tile)

    if k_prev is None:
        k_prev, v_prev = ka, va
    ya = _attention(qa, ka, va, k_prev, v_prev, wts["sink_tab"], bb=attn_bb, tq=attn_tq, mask_first=mask_first)

    yb, c_new, n_new, m_new = _mlstm(za, zb, gates, wts["gate_bias"], *state)

    mtile = MERGE_CAST_TILE if casting else COL_TILE
    ga_col = M_V_WIDTH // mtile
    gb_col = ga_col + d // mtile
    t = dense("merge_a", _mm_gate_first_body, ya, ["w_up_a"], casting and [columns("w_up_a", 0, d, mtile)],
              [zb], [lambda bb, tt: _act_spec(bb, tt, mtile, ga_col)], [(d, F32, mtile)], d, mtile)
    merged = dense("merge_b", _mm_gate_second_body, yb, ["w_up_b"], casting and [columns("w_up_b", 0, d, mtile)],
                   [zb, t], [lambda bb, tt: _act_spec(bb, tt, mtile, gb_col),
                             lambda bb, tt: _act_spec(bb, tt, mtile)], [(d, BF16, mtile)], d, mtile)

    def mod_spec(idx, width):
        per_vec = d // width
        return lambda bb, tt: pl.BlockSpec((bb, 1, width), lambda b, t_, j: (b, 0, idx * per_vec + j))

    r1 = dense("out_proj", _mm_residual_body, merged, ["w_o"], casting and [columns("w_o", 0, d, mtile)],
               [x, mod], [lambda bb, tt: _act_spec(bb, tt, mtile), mod_spec(GT1, mtile)],
               [(d, F32, mtile)], d, mtile)
    x1, u2 = _layer_norm(r1, wts["ln1_g"], wts["ln1_b"], mod, SC2, SH2)

    ff_tile = FFN_CAST_TILE if casting else COL_TILE // 2
    ffn_casts = casting and [columns("w_ffn_in", 0, D_FF, ff_tile), columns("w_ffn_in", D_FF, D_FF, ff_tile)]
    hidden = dense("ffn_in", _mm_swiglu_body, u2, ["w_ffn_gate", "w_ffn_up"], ffn_casts,
                   [], [], [(D_FF, BF16, ff_tile)], D_FF, ff_tile)

    out_tile = COL_TILE // 2
    r2 = _matmul(_mm_residual_body, hidden, [(wts["w_ffn_out"], out_tile, lambda j: j)], [x1, mod],
                 [lambda bb, tt: _act_spec(bb, tt, out_tile), mod_spec(GT2, out_tile)],
                 [(d, F32, out_tile)], n_tiles=d // out_tile, rows=ROW_TILE // 2, name="ffn_out")
    y = _layer_norm(r2, wts["ln2_g"], wts["ln2_b"])
    return y, ka, va, c_new, n_new, m_new


def _cast_body(w_ref, o_ref):
    o_ref[...] = w_ref[0].T.astype(o_ref.dtype)


def _cast_gates_body(w_ref, o_ref):
    x = w_ref[0].T
    valid = lax.broadcasted_iota(jnp.int32, (1, LANES), 1) < M_HEADS
    o_ref[:, :LANES] = jnp.where(valid, x, 0.0).astype(o_ref.dtype)
    f_first = pltpu.roll(x, LANES - M_HEADS, axis=1)
    o_ref[:, LANES:] = jnp.where(valid, f_first, 0.0).astype(o_ref.dtype)


def _cast_kv_gates(w_in_t):
    _, _, k = w_in_t.shape
    params = _params(("parallel",))
    first = ATT_WIDTH // CAST_TILE
    w_kv = pl.pallas_call(
        _cast_body, grid=(2 * KV_WIDTH // CAST_TILE,),
        in_specs=[pl.BlockSpec((1, CAST_TILE, k), lambda j: (0, first + j, 0))],
        out_specs=pl.BlockSpec((k, CAST_TILE), lambda j: (0, j)),
        out_shape=jax.ShapeDtypeStruct((k, 2 * KV_WIDTH), BF16),
        compiler_params=params, name="cast_proj_kv",
    )(w_in_t)
    w_g = pl.pallas_call(
        _cast_gates_body, grid=(1,),
        in_specs=[pl.BlockSpec((1, LANES, k), lambda j: (0, PROJ_HEAD // LANES, 0))],
        out_specs=pl.BlockSpec((k, GATE_PAD), lambda j: (0, 0)),
        out_shape=jax.ShapeDtypeStruct((k, GATE_PAD), BF16),
        compiler_params=params, name="cast_proj_gates",
    )(w_in_t)
    return w_kv, w_g


def kernel(x_prompt, x_sample, cache_k_win, cache_v_win, state_C, state_n, state_m, c_prompt, c_sample, w_ada, b_ada, w_in, b_if, attn_sinks, w_up_a, w_up_b, w_o, ln1_g, ln1_b, w_ffn_in, w_ffn_out, ln2_g, ln2_b):
    bp, sp, d = x_prompt.shape
    bs, ts, _ = x_sample.shape
    keep = cache_k_win.shape[2]
    assert w_in.shape[0] == DEPTH == 1
    l = 0

    raw = {"w_in_t": jnp.swapaxes(w_in, 1, 2), "w_up_a": w_up_a, "w_up_b": w_up_b, "w_o": w_o,
           "w_ffn_in": w_ffn_in}
    w_kv, w_g = _cast_kv_gates(raw["w_in_t"])
    pad_bias = jnp.zeros((LANES - M_HEADS,), F32)
    wts = {
        "w_kv": w_kv, "w_g": w_g,
        "gate_bias": jnp.concatenate([b_if[l, :M_HEADS], pad_bias, b_if[l, M_HEADS:], pad_bias]).reshape(1, GATE_PAD),
        "sink_tab": jnp.broadcast_to(attn_sinks[l].astype(F32).reshape(N_KV_HEADS, 1, GROUP, 1),
                                     (N_KV_HEADS, 1, GROUP, CHUNK)).reshape(N_KV_HEADS, 1, GROUP * CHUNK),
        "w_ffn_out": w_ffn_out[l].astype(BF16),
        "ln1_g": ln1_g[l], "ln1_b": ln1_b[l], "ln2_g": ln2_g[l], "ln2_b": ln2_b[l],
    }

    c_all = jnp.concatenate([c_prompt, c_sample], axis=0)
    rows = c_all.shape[0]
    rows_pad = -(-rows // 8) * 8
    c_all = jnp.pad(c_all, ((0, rows_pad - rows), (0, 0)))
    mod = _adaln(c_all, w_ada[l], b_ada[l])
    mod_p = mod[:bp].reshape(bp, 1, 6 * d)
    mod_s = mod[bp:bp + bs].reshape(bs, 1, 6 * d)

    def pad_lanes(m):
        return jnp.pad(m, ((0, 0), (0, LANES - M_HEADS))).reshape(m.shape[0], 1, LANES)

    ck = cache_k_win[l].reshape(bs, keep, KV_WIDTH)
    cv = cache_v_win[l].reshape(bs, keep, KV_WIDTH)
    state_s = (state_C[l], state_n[l], pad_lanes(state_m[l]))
    ys, kas, vas, cs, ns, ms = _group_layer(
        x_sample, mod_s, PAST_LEN + jnp.arange(ts), wts, raw, ck, cv, state_s,
        attn_bb=min(bs, ATTN_ROWS // ts), attn_tq=ts, mask_first=False)

    state_p = (jnp.zeros((bp, M_HEADS, M_V_DIM, M_QK_DIM), F32),
               jnp.zeros((bp, M_HEADS, M_QK_DIM), F32),
               pad_lanes(jnp.full((bp, M_HEADS), M_INIT, F32)))
    yp, kap, vap, cp, np_, mp = _group_layer(
        x_prompt, mod_p, jnp.arange(sp), wts, None, None, None, state_p,
        attn_bb=1, attn_tq=min(ATTN_ROWS, sp), mask_first=True)

    def heads(a):
        return a.reshape(a.shape[0], a.shape[1], N_KV_HEADS, HEAD_DIM)

    kws = jnp.concatenate([ck, kas], axis=1)[:, -keep:]
    vws = jnp.concatenate([cv, vas], axis=1)[:, -keep:]
    return (yp, ys,
            heads(kap[:, -keep:])[None], heads(vap[:, -keep:])[None],
            cp[None], np_[None], mp[:, 0, :M_HEADS][None],
            heads(kws)[None], heads(vws)[None],
            cs[None], ns[None], ms[:, 0, :M_HEADS][None])
```

```python
import functools

import jax
import jax.numpy as jnp
from jax import lax
from jax.experimental import pallas as pl
from jax.experimental.pallas import tpu as pltpu

F32 = jnp.float32
BF16 = jnp.bfloat16

D_MODEL = 4096
CHUNK = 64
N_Q_HEADS = 64
N_KV_HEADS = 8
HEAD_DIM = 64
GROUP = N_Q_HEADS // N_KV_HEADS
WINDOW = 128
ROT_DIM = HEAD_DIM // 4
ROPE_THETA = 500000.0
M_HEADS = 8
M_QK_DIM = 256
M_V_DIM = 512
ATT_WIDTH = N_Q_HEADS * HEAD_DIM
KV_WIDTH = N_KV_HEADS * HEAD_DIM
M_QK_WIDTH = M_HEADS * M_QK_DIM
M_V_WIDTH = M_HEADS * M_V_DIM
D_FF = 11008
DEPTH = 1
PAST_LEN = 2048
ALPHA = (2.0 * DEPTH) ** 0.25
LN_EPS = 1e-5
M_INIT = -1e30

LANES = 128
ROW_TILE = 1024
COL_TILE = 1024
ATTN_ROWS = 1024
MLSTM_ROWS = 256
GATE_PAD = 2 * LANES
VMEM_LIMIT = 60 * 1024 * 1024


def _params(semantics):
    return pltpu.CompilerParams(dimension_semantics=semantics, vmem_limit_bytes=VMEM_LIMIT)


def _row_blocks(batch, steps, rows):
    if steps >= rows:
        assert steps % rows == 0
        return 1, rows
    bb = min(batch, rows // steps)
    assert batch % bb == 0
    return bb, steps


def _adaln_body(c_ref, w_ref, b_ref, o_ref):
    c = c_ref[...]
    s = c * jax.nn.sigmoid(c)
    o_ref[...] = jnp.dot(s, w_ref[...], preferred_element_type=F32) + b_ref[...]


def _adaln(c, w_ada, b_ada):
    rows, d = c.shape
    n = w_ada.shape[1]
    tn = 512
    return pl.pallas_call(
        _adaln_body,
        grid=(n // tn,),
        in_specs=[pl.BlockSpec((rows, d), lambda j: (0, 0)),
                  pl.BlockSpec((d, tn), lambda j: (0, j)),
                  pl.BlockSpec((1, tn), lambda j: (0, j))],
        out_specs=pl.BlockSpec((rows, tn), lambda j: (0, j)),
        out_shape=jax.ShapeDtypeStruct((rows, n), F32),
        compiler_params=_params(("parallel",)),
        name="adaln",
    )(c, w_ada, b_ada.reshape(1, n))


def _load_rows(a_ref):
    bb, tt, k = a_ref.shape
    return a_ref[...].reshape(bb * tt, k)


def _rope_rows(x, cos_ref, sa_ref, sb_ref, bb):
    rows, n = x.shape
    tt = rows // bb

    def table(ref):
        t = ref[...]
        if bb == 1:
            return t
        return jnp.broadcast_to(t[None], (bb, tt, LANES)).reshape(rows, LANES)

    cos, sa, sb = table(cos_ref), table(sa_ref), table(sb_ref)
    outs = []
    for s in range(n // LANES):
        xs = x[:, s * LANES:(s + 1) * LANES]
        up = pltpu.roll(xs, LANES - ROT_DIM // 2, axis=1)
        dn = pltpu.roll(xs, ROT_DIM // 2, axis=1)
        outs.append(xs * cos + up * sa + dn * sb)
    return jnp.concatenate(outs, axis=1)


def _mm_plain_body(a_ref, w_ref, o_ref):
    acc = jnp.dot(_load_rows(a_ref), w_ref[...], preferred_element_type=F32)
    o_ref[...] = acc.reshape(o_ref.shape).astype(o_ref.dtype)


def _mm_rope_body(a_ref, w_ref, cos_ref, sa_ref, sb_ref, o_ref):
    acc = jnp.dot(_load_rows(a_ref), w_ref[...], preferred_element_type=F32)
    out = _rope_rows(acc, cos_ref, sa_ref, sb_ref, a_ref.shape[0])
    o_ref[...] = out.reshape(o_ref.shape).astype(o_ref.dtype)


def _mm_mod_kvg_body(x_ref, w_ref, wg_ref, sc_ref, sh_ref, cos_ref, sa_ref, sb_ref, u_ref, k_ref, v_ref, g_ref):
    u_ref[...] = (x_ref[...] * (1.0 + sc_ref[...]) + sh_ref[...]).astype(u_ref.dtype)
    a = _load_rows(u_ref)
    acc = jnp.dot(a, w_ref[...], preferred_element_type=F32)
    k = _rope_rows(acc[:, :KV_WIDTH], cos_ref, sa_ref, sb_ref, x_ref.shape[0])
    k_ref[...] = k.reshape(k_ref.shape)
    v_ref[...] = acc[:, KV_WIDTH:].reshape(v_ref.shape)
    g_ref[...] = jnp.dot(a, wg_ref[...], preferred_element_type=F32).reshape(g_ref.shape)


def _mm_gate_first_body(a_ref, w_ref, g_ref, o_ref):
    acc = jnp.dot(_load_rows(a_ref), w_ref[...], preferred_element_type=F32)
    gate = jax.nn.sigmoid(g_ref[...].astype(F32))
    o_ref[...] = gate * acc.reshape(o_ref.shape)


def _mm_gate_second_body(a_ref, w_ref, g_ref, t_ref, o_ref):
    acc = jnp.dot(_load_rows(a_ref), w_ref[...], preferred_element_type=F32)
    gate = jax.nn.sigmoid(g_ref[...].astype(F32))
    o_ref[...] = (t_ref[...] + gate * acc.reshape(o_ref.shape)).astype(o_ref.dtype)


def _mm_residual_body(a_ref, w_ref, x_ref, g_ref, o_ref):
    acc = jnp.dot(_load_rows(a_ref), w_ref[...], preferred_element_type=F32)
    o_ref[...] = ALPHA * x_ref[...] + g_ref[...] * acc.reshape(o_ref.shape)


def _mm_swiglu_body(a_ref, wg_ref, wu_ref, o_ref):
    a = _load_rows(a_ref)
    gate = jnp.dot(a, wg_ref[...], preferred_element_type=F32)
    up = jnp.dot(a, wu_ref[...], preferred_element_type=F32)
    out = gate * jax.nn.sigmoid(gate) * up
    o_ref[...] = out.reshape(o_ref.shape).astype(o_ref.dtype)


def _act_spec(bb, tt, width, col=None):
    if col is None:
        return pl.BlockSpec((bb, tt, width), lambda b, t, j: (b, t, j))
    return pl.BlockSpec((bb, tt, width), lambda b, t, j: (b, t, j + col))


def _matmul(body, a, weights, extras, extra_specs, outs, *, n_tiles, rows=ROW_TILE, name):
    batch, steps, k = a.shape
    bb, tt = _row_blocks(batch, steps, rows)
    grid = (batch // bb, steps // tt, n_tiles)
    in_specs = [pl.BlockSpec((bb, tt, k), lambda b, t, j: (b, t, 0))]
    in_specs += [pl.BlockSpec((k, cols), functools.partial(lambda b, t, j, col: (0, col(j)), col=col))
                 for _, cols, col in weights]
    in_specs += [spec(bb, tt) for spec in extra_specs]
    weights = [w for w, _, _ in weights]
    out_specs = [pl.BlockSpec((bb, tt, cols_tile), lambda b, t, j: (b, t, j)) for _, _, cols_tile in outs]
    out_shape = [jax.ShapeDtypeStruct((batch, steps, cols), dt) for cols, dt, _ in outs]
    single = len(outs) == 1
    return pl.pallas_call(
        body,
        grid=grid,
        in_specs=in_specs,
        out_specs=out_specs[0] if single else out_specs,
        out_shape=out_shape[0] if single else out_shape,
        compiler_params=_params(("parallel", "parallel", "parallel")),
        name=name,
    )(a, *weights, *extras)


def _matmul_casting(body, a, casts, extras, extra_specs, outs, *, n_tiles, name, n_side=0):
    batch, steps, k = a.shape

    def per_tile(spec):
        return pl.BlockSpec(spec.block_shape, lambda j: spec.index_map(0, 0, j))

    in_specs = [pl.BlockSpec((batch, steps, k), lambda j: (0, 0, 0), pipeline_mode=pl.Buffered(1))]
    sources = []
    for srcs, *_ in casts:
        for arr, shape, index in srcs:
            in_specs.append(pl.BlockSpec(shape, index))
            sources.append(arr)
    in_specs += [per_tile(spec(batch, steps)) for spec in extra_specs]
    out_specs = [pl.BlockSpec((batch, steps, cols_tile), lambda j: (0, 0, j)) for _, _, cols_tile in outs]
    out_shape = [jax.ShapeDtypeStruct((batch, steps, cols), dt) for cols, dt, _ in outs]
    out_specs += [pl.BlockSpec(block, index) for _, _, _, block, index in casts]
    out_shape += [jax.ShapeDtypeStruct(shape, BF16) for _, _, shape, _, _ in casts]
    counts = [len(srcs) for srcs, *_ in casts]
    transforms = [tr for _, tr, *_ in casts]
    n_used = len(casts) - n_side
    n_extra, n_out = len(extras), len(outs)

    def casting_body(a_ref, *refs):
        src_refs, pos = [], 0
        for n in counts:
            src_refs.append(refs[pos:pos + n])
            pos += n
        extra_refs = refs[pos:pos + n_extra]
        out_refs = refs[pos + n_extra:pos + n_extra + n_out]
        w_refs = refs[pos + n_extra + n_out:]
        for w_ref, srcs, transform in zip(w_refs, src_refs, transforms):
            w_ref[...] = transform(*[s[...] for s in srcs]).astype(w_ref.dtype)
        body(a_ref, *w_refs[:n_used], *extra_refs, *out_refs)

    return pl.pallas_call(
        casting_body,
        grid=(n_tiles,),
        in_specs=in_specs,
        out_specs=out_specs,
        out_shape=out_shape,
        compiler_params=_params(("arbitrary",)),
        name=name,
    )(a, *sources, *extras)


def _rope_specs():
    return [lambda bb, tt: pl.BlockSpec((tt, LANES), lambda b, t, j: (t, 0))] * 3


def _attn_body(q_ref, kc_ref, kp_ref, vc_ref, vp_ref, sink_ref, o_ref, *, mask_first):
    bb, tq, _ = q_ref.shape
    n_chunks = tq // CHUNK
    n_keys = WINDOW + CHUNK
    i = pl.program_id(1)
    h = pl.program_id(2)

    lane = lax.broadcasted_iota(jnp.int32, (1, LANES), 1)
    keep = (lane // HEAD_DIM) == (h % 2)
    sink = sink_ref[0]
    lane_head = lax.broadcasted_iota(jnp.int32, (CHUNK, 2 * LANES), 1) // HEAD_DIM
    out_low = lax.broadcasted_iota(jnp.int32, (CHUNK, LANES), 1) < HEAD_DIM
    key_row = lax.broadcasted_iota(jnp.int32, (n_keys, 1), 0)
    ones = jnp.ones((16, n_keys), BF16)

    def both_halves(cur_ref, prev_ref, bi):
        pair = jnp.concatenate([prev_ref[bi].astype(F32), cur_ref[bi].astype(F32)], axis=0)
        swapped = pltpu.roll(pair, HEAD_DIM, axis=1)
        return jnp.where(keep, pair, swapped)

    def scores(bi, j, k4):
        q = q_ref[bi, j * CHUNK:(j + 1) * CHUNK, :] * (HEAD_DIM ** -0.5)
        parts = []
        for cg in range(GROUP // 4):
            qc = q[:, cg * 2 * LANES:(cg + 1) * 2 * LANES]
            for r in range(4):
                parts.append(jnp.where(lane_head == r, qc, jnp.zeros_like(qc)))
        qrows = jnp.concatenate(parts, axis=0)
        keys = k4[j * CHUNK:j * CHUNK + n_keys]
        st = lax.dot_general(keys, qrows, (((1,), (1,)), ((), ())), preferred_element_type=F32)
        if mask_first and j < WINDOW // CHUNK:
            first_valid = WINDOW - (i * tq + j * CHUNK)
            st = jnp.where(key_row >= first_valid, st, -jnp.inf)
        return st

    def values_t(j, v2):
        vwin = v2[j * CHUNK:j * CHUNK + n_keys]
        head = vwin[:LANES].T
        tail = vwin[LANES:]
        tail = jnp.concatenate([tail, tail], axis=0).T
        vt = jnp.concatenate([head[:HEAD_DIM], tail[:HEAD_DIM, :CHUNK]], axis=1).astype(BF16)
        return jnp.concatenate([vt, ones], axis=0)

    def finish(st, lhs):
        m = jnp.maximum(jnp.max(st, axis=0, keepdims=True), sink)
        pt = jnp.exp(st - m).astype(BF16)
        ot = jnp.dot(lhs, pt, preferred_element_type=F32)
        den = ot[HEAD_DIM:HEAD_DIM + 1] + jnp.exp(sink - m)
        return ot[:HEAD_DIM] / den

    def emit(bi, j, on):
        for pr in range(GROUP // 2):
            x = on[:, pr * LANES:(pr + 1) * LANES]
            xt = jnp.concatenate([x, x], axis=0).T
            o_ref[bi, j * CHUNK:(j + 1) * CHUNK, pr * LANES:(pr + 1) * LANES] = (
                jnp.where(out_low, xt[:CHUNK], xt[CHUNK:]).astype(o_ref.dtype))

    units, sts, lhss = [], [], []
    for bi in range(bb):
        k2 = both_halves(kc_ref, kp_ref, bi)
        v2 = both_halves(vc_ref, vp_ref, bi)
        k4 = jnp.concatenate([k2, k2], axis=1).astype(BF16)
        for j in range(n_chunks):
            units.append((bi, j))
            sts.append(scores(bi, j, k4))
            lhss.append(values_t(j, v2))
    for (bi, j), st, lhs in zip(units, sts, lhss):
        emit(bi, j, finish(st, lhs))


def _attention(q, k_cur, v_cur, k_prev, v_prev, sink_tab, *, bb, tq, mask_first):
    batch, steps, _ = q.shape
    qw = GROUP * HEAD_DIM
    same_array = k_prev is k_cur
    per_tile = tq // WINDOW

    def prev_map(b, i, h):
        if same_array:
            return (b, jnp.maximum(i * per_tile - 1, 0), h // 2)
        return (b, 0, h // 2)

    return pl.pallas_call(
        functools.partial(_attn_body, mask_first=mask_first),
        grid=(batch // bb, steps // tq, N_KV_HEADS),
        in_specs=[pl.BlockSpec((bb, tq, qw), lambda b, i, h: (b, i, h)),
                  pl.BlockSpec((bb, tq, LANES), lambda b, i, h: (b, i, h // 2)),
                  pl.BlockSpec((bb, WINDOW, LANES), prev_map),
                  pl.BlockSpec((bb, tq, LANES), lambda b, i, h: (b, i, h // 2)),
                  pl.BlockSpec((bb, WINDOW, LANES), prev_map),
                  pl.BlockSpec((1, 1, GROUP * CHUNK), lambda b, i, h: (h, 0, 0))],
        out_specs=pl.BlockSpec((bb, tq, qw), lambda b, i, h: (b, i, h)),
        out_shape=jax.ShapeDtypeStruct((batch, steps, ATT_WIDTH), BF16),
        compiler_params=_params(("parallel", "parallel", "parallel")),
        name="attention",
    )(q, k_cur, k_prev, v_cur, v_prev, sink_tab)


def _log_sigmoid(x):
    return jnp.minimum(x, 0.0) - jnp.log1p(jnp.exp(-jnp.abs(x)))


def _mlstm_body(q_ref, k_ref, v_ref, o_ref, g_ref, bias_ref, c0_ref, n0_ref, m0_ref,
                y_ref, c_ref, n_ref, m_ref):
    @pl.when(pl.program_id(1) == 0)
    def _():
        c_ref[...] = c0_ref[...]
        n_ref[...] = n0_ref[...]
        m_ref[...] = m0_ref[...]

    steps = q_ref.shape[1]
    g = g_ref[0] + bias_ref[...]
    li = g[:, :LANES]
    lf = _log_sigmoid(g[:, LANES:])
    row = lax.broadcasted_iota(jnp.int32, (steps, steps), 0)
    col = lax.broadcasted_iota(jnp.int32, (steps, steps), 1)
    causal = col <= row
    b = jnp.dot(causal.astype(F32), lf, preferred_element_type=F32,
                precision=lax.Precision.HIGHEST)
    r = li - b
    m_prev = m_ref[0]
    b_last = b[steps - 1:steps, :]
    m_new = b_last + jnp.maximum(m_prev, jnp.max(r, axis=0, keepdims=True))
    w_state = jnp.exp(r + (b_last - m_new))
    decay = jnp.exp(b_last + m_prev - m_new)
    r_rows = r.T

    for h in range(M_HEADS):
        r_row = r_rows[h:h + 1, :]
        dmat = jnp.where(causal, jnp.broadcast_to(r_row, (steps, steps)), -jnp.inf)
        m_h = m_prev[:, h:h + 1]
        gmax = jnp.maximum(jnp.max(dmat, axis=-1, keepdims=True), m_h)
        w_intra = jnp.exp(dmat - gmax)
        w_inter = jnp.exp(m_h - gmax)
        inv_floor = jnp.exp(-(b[:, h:h + 1] + gmax))

        q = q_ref[0, :, h * M_QK_DIM:(h + 1) * M_QK_DIM] * (M_QK_DIM ** -0.5)
        k = k_ref[0, :, h * M_QK_DIM:(h + 1) * M_QK_DIM]
        v = v_ref[0, :, h * M_V_DIM:(h + 1) * M_V_DIM]
        c_old = c_ref[0, h]
        n_old = n_ref[0, h:h + 1, :]

        qk = lax.dot_general(q, k, (((1,), (1,)), ((), ())), preferred_element_type=F32)
        s = qk * w_intra
        intra = jnp.dot(s.astype(BF16), v, preferred_element_type=F32)
        inter = lax.dot_general(q, c_old.astype(BF16), (((1,), (1,)), ((), ())),
                                preferred_element_type=F32)
        nq = jnp.sum(q.astype(F32) * n_old, axis=-1, keepdims=True)
        num = intra + w_inter * inter
        den = jnp.sum(s, axis=-1, keepdims=True) + w_inter * nq
        hval = num / jnp.maximum(jnp.abs(den), inv_floor)
        og = jax.nn.sigmoid(o_ref[0, :, h * M_V_DIM:(h + 1) * M_V_DIM].astype(F32))
        y_ref[0, :, h * M_V_DIM:(h + 1) * M_V_DIM] = (og * hval).astype(y_ref.dtype)

        ws = w_state[:, h:h + 1]
        vw = (v.astype(F32) * ws).astype(BF16)
        dc = lax.dot_general(vw, k, (((0,), (0,)), ((), ())), preferred_element_type=F32)
        dec = decay[:, h:h + 1]
        c_ref[0, h] = dec * c_old + dc
        n_ref[0, h:h + 1, :] = dec * n_old + jnp.sum(k.astype(F32) * ws, axis=0, keepdims=True)

    m_ref[0] = m_new


def _mlstm(za, zb, gates, bias, c0, n0, m0):
    batch, steps, _ = za.shape
    rows = min(steps, MLSTM_ROWS)
    state_specs = [pl.BlockSpec((1, M_HEADS, M_V_DIM, M_QK_DIM), lambda b, c: (b, 0, 0, 0)),
                   pl.BlockSpec((1, M_HEADS, M_QK_DIM), lambda b, c: (b, 0, 0)),
                   pl.BlockSpec((1, 1, LANES), lambda b, c: (b, 0, 0))]
    return pl.pallas_call(
        _mlstm_body,
        grid=(batch, steps // rows),
        in_specs=[pl.BlockSpec((1, rows, M_QK_WIDTH), lambda b, c: (b, c, 0)),
                  pl.BlockSpec((1, rows, M_QK_WIDTH), lambda b, c: (b, c, 1)),
                  pl.BlockSpec((1, rows, M_V_WIDTH), lambda b, c: (b, c, 1)),
                  pl.BlockSpec((1, rows, M_V_WIDTH), lambda b, c: (b, c, 0)),
                  pl.BlockSpec((1, rows, GATE_PAD), lambda b, c: (b, c, 0)),
                  pl.BlockSpec((1, GATE_PAD), lambda b, c: (0, 0))] + state_specs,
        out_specs=[pl.BlockSpec((1, rows, M_V_WIDTH), lambda b, c: (b, c, 0))] + state_specs,
        out_shape=[jax.ShapeDtypeStruct((batch, steps, M_V_WIDTH), BF16),
                   jax.ShapeDtypeStruct(c0.shape, F32),
                   jax.ShapeDtypeStruct(n0.shape, F32),
                   jax.ShapeDtypeStruct(m0.shape, F32)],
        compiler_params=_params(("parallel", "arbitrary")),
        name="mlstm",
    )(za, za, za, zb, gates, bias, c0, n0, m0)


def _ln_rows(x, g, b):
    mu = jnp.mean(x, axis=-1, keepdims=True)
    xc = x - mu
    var = jnp.mean(xc * xc, axis=-1, keepdims=True)
    return xc * lax.rsqrt(var + LN_EPS) * g + b


def _ln_mod_body(r_ref, g_ref, b_ref, sc_ref, sh_ref, x_ref, u_ref):
    y = _ln_rows(r_ref[...], g_ref[...], b_ref[...])
    x_ref[...] = y
    u_ref[...] = (y * (1.0 + sc_ref[...]) + sh_ref[...]).astype(u_ref.dtype)


def _ln_body(r_ref, g_ref, b_ref, x_ref):
    x_ref[...] = _ln_rows(r_ref[...], g_ref[...], b_ref[...])


def _layer_norm(r, g, b, mod=None, sc_idx=None, sh_idx=None):
    batch, steps, d = r.shape
    bb, tt = _row_blocks(batch, steps, 256)
    act = pl.BlockSpec((bb, tt, d), lambda i, t: (i, t, 0))
    vec = pl.BlockSpec((1, 1, d), lambda i, t: (0, 0, 0))
    g3, b3 = g.reshape(1, 1, d), b.reshape(1, 1, d)
    if mod is None:
        return pl.pallas_call(
            _ln_body, grid=(batch // bb, steps // tt),
            in_specs=[act, vec, vec], out_specs=act,
            out_shape=jax.ShapeDtypeStruct(r.shape, F32),
            compiler_params=_params(("parallel", "parallel")), name="layer_norm",
        )(r, g3, b3)
    return pl.pallas_call(
        _ln_mod_body, grid=(batch // bb, steps // tt),
        in_specs=[act, vec, vec,
                  pl.BlockSpec((bb, 1, d), lambda i, t: (i, 0, sc_idx)),
                  pl.BlockSpec((bb, 1, d), lambda i, t: (i, 0, sh_idx))],
        out_specs=[act, act],
        out_shape=[jax.ShapeDtypeStruct(r.shape, F32), jax.ShapeDtypeStruct(r.shape, BF16)],
        compiler_params=_params(("parallel", "parallel")), name="layer_norm_modulate",
    )(r, g3, b3, mod, mod)


SH1, SC1, GT1, SH2, SC2, GT2 = range(6)


def _rope_tables(pos):
    half = ROT_DIM // 2
    inv = ROPE_THETA ** (-2.0 * jnp.arange(half, dtype=F32) / ROT_DIM)
    ang = pos.astype(F32)[:, None] * inv[None, :]
    cos, sin = jnp.cos(ang), jnp.sin(ang)
    zeros = jnp.zeros_like(cos)
    rest = HEAD_DIM - ROT_DIM
    steps = pos.shape[0]
    one_head = lambda first, second, fill: jnp.concatenate(
        [first, second, jnp.full((steps, rest), fill, F32)], axis=1)
    cos_t = one_head(cos, cos, 1.0)
    up_t = one_head(-sin, zeros, 0.0)
    dn_t = one_head(zeros, sin, 0.0)
    reps = LANES // HEAD_DIM
    return tuple(jnp.tile(t, (1, reps)) for t in (cos_t, up_t, dn_t))


CAST_TILE = 512
FFN_CAST_TILE = 256
MERGE_CAST_TILE = 256
GATE_COLS = 2 * M_HEADS
PROJ_HEAD = ATT_WIDTH + 2 * KV_WIDTH + 2 * M_QK_WIDTH + M_V_WIDTH


def _group_layer(x, mod, pos, wts, raw, k_prev, v_prev, state, *, attn_bb, attn_tq, mask_first):
    batch, steps, d = x.shape
    ropes = _rope_tables(pos)
    casting = raw is not None
    tile = CAST_TILE if casting else COL_TILE

    def dense(name, body, act, keys, casts, extras, extra_specs, outs, n_cols, wtile, side_keys=()):
        n_tiles = -(-n_cols // wtile)
        if not casting:
            weights = [(wts[key], wtile, lambda j: j) for key in keys]
            return _matmul(body, act, weights, extras, extra_specs, outs, n_tiles=n_tiles, name=name)
        res = _matmul_casting(body, act, casts, extras, extra_specs, outs, n_tiles=n_tiles, name=name,
                              n_side=len(side_keys))
        for key, w in zip(list(keys) + list(side_keys), res[len(outs):]):
            wts[key] = w
        return res[0] if len(outs) == 1 else res[:len(outs)]

    def proj_rows(first, total):
        srcs = [(raw["w_in_t"], (1, tile, d), lambda j: (0, first // tile + j, 0))]
        return [(srcs, lambda w: w[0].T, (d, total), (d, tile), lambda j: (0, j))]

    def proj_rows_shifted(first, total):
        base = first - GATE_COLS
        srcs = [(raw["w_in_t"], (1, tile, d), lambda j: (0, base // tile + j, 0)),
                (raw["w_in_t"], (1, GATE_COLS, d), lambda j: (0, (base + tile * (j + 1)) // GATE_COLS, 0))]

        def transform(main, nxt):
            rows = jnp.concatenate([main[0], nxt[0]], axis=0)
            return rows[GATE_COLS:GATE_COLS + tile].T

        return [(srcs, transform, (d, total), (d, tile), lambda j: (0, j))]

    def columns(key, first, total, width):
        srcs = [(raw[key], (1, d, width), lambda j: (0, 0, first // width + j))]
        return (srcs, lambda w: w[0], (d, total), (d, width), lambda j: (0, j))

    def mod_full(idx):
        return lambda bb, tt: pl.BlockSpec((bb, 1, d), lambda b, t_, j: (b, 0, idx))

    u, ka, va, gates = _matmul(
        _mm_mod_kvg_body, x, [(wts["w_kv"], 2 * KV_WIDTH, lambda j: 0), (wts["w_g"], GATE_PAD, lambda j: 0)],
        [mod, mod, *ropes], [mod_full(SC1), mod_full(SH1)] + _rope_specs(),
        [(d, BF16, d), (KV_WIDTH, F32, KV_WIDTH), (KV_WIDTH, F32, KV_WIDTH), (GATE_PAD, F32, GATE_PAD)],
        n_tiles=1, rows=ROW_TILE // 2, name="proj_kvg")

    mem_first = ATT_WIDTH + 2 * KV_WIDTH
    za_width = 2 * M_QK_WIDTH + M_V_WIDTH
    zb_width = M_V_WIDTH + 2 * d
    qa = dense("proj_q", _mm_rope_body, u, ["w_qa"], casting and proj_rows(0, ATT_WIDTH),
               ropes, _rope_specs(), [(ATT_WIDTH, BF16, tile)], ATT_WIDTH, tile)
    za = dense("proj_mem", _mm_plain_body, u, ["w_mem"], casting and proj_rows(mem_first, za_width),
               [], [], [(za_width, BF16, tile)], za_width, tile)
    zb = dense("proj_gates", _mm_plain_body, u, ["w_gates"],
               casting and proj_rows_shifted(PROJ_HEAD + GATE_COLS, zb_width),
               [], [], [(zb_width, BF16, tile)], zb_width, tile)

    if k_prev is None:
        k_prev, v_prev = ka, va
    ya = _attention(qa, ka, va, k_prev, v_prev, wts["sink_tab"], bb=attn_bb, tq=attn_tq, mask_first=mask_first)

    yb, c_new, n_new, m_new = _mlstm(za, zb, gates, wts["gate_bias"], *state)

    mtile = MERGE_CAST_TILE if casting else COL_TILE
    ga_col = M_V_WIDTH // mtile
    gb_col = ga_col + d // mtile
    t = dense("merge_a", _mm_gate_first_body, ya, ["w_up_a"], casting and [columns("w_up_a", 0, d, mtile)],
              [zb], [lambda bb, tt: _act_spec(bb, tt, mtile, ga_col)], [(d, F32, mtile)], d, mtile)
    merged = dense("merge_b", _mm_gate_second_body, yb, ["w_up_b"], casting and [columns("w_up_b", 0, d, mtile)],
                   [zb, t], [lambda bb, tt: _act_spec(bb, tt, mtile, gb_col),
                             lambda bb, tt: _act_spec(bb, tt, mtile)], [(d, BF16, mtile)], d, mtile)

    def mod_spec(idx, width):
        per_vec = d // width
        return lambda bb, tt: pl.BlockSpec((bb, 1, width), lambda b, t_, j: (b, 0, idx * per_vec + j))

    r1 = dense("out_proj", _mm_residual_body, merged, ["w_o"], casting and [columns("w_o", 0, d, mtile)],
               [x, mod], [lambda bb, tt: _act_spec(bb, tt, mtile), mod_spec(GT1, mtile)],
               [(d, F32, mtile)], d, mtile)
    x1, u2 = _layer_norm(r1, wts["ln1_g"], wts["ln1_b"], mod, SC2, SH2)

    ff_tile = FFN_CAST_TILE if casting else COL_TILE // 2
    ffn_casts = casting and [
        columns("w_ffn_in", 0, D_FF, ff_tile), columns("w_ffn_in", D_FF, D_FF, ff_tile),
        ([(raw["w_ffn_out"], (1, ff_tile, d), lambda j: (0, j, 0))], lambda w: w[0],
         (D_FF, d), (ff_tile, d), lambda j: (j, 0))]
    hidden = dense("ffn_in", _mm_swiglu_body, u2, ["w_ffn_gate", "w_ffn_up"], ffn_casts,
                   [], [], [(D_FF, BF16, ff_tile)], D_FF, ff_tile, side_keys=["w_ffn_out"] if casting else ())

    out_tile = COL_TILE // 2
    r2 = _matmul(_mm_residual_body, hidden, [(wts["w_ffn_out"], out_tile, lambda j: j)], [x1, mod],
                 [lambda bb, tt: _act_spec(bb, tt, out_tile), mod_spec(GT2, out_tile)],
                 [(d, F32, out_tile)], n_tiles=d // out_tile, rows=ROW_TILE // 2, name="ffn_out")
    y = _layer_norm(r2, wts["ln2_g"], wts["ln2_b"])
    return y, ka, va, c_new, n_new, m_new


def _cast_body(w_ref, o_ref):
    o_ref[...] = w_ref[0].T.astype(o_ref.dtype)


def _cast_gates_body(w_ref, o_ref):
    x = w_ref[0].T
    valid = lax.broadcasted_iota(jnp.int32, (1, LANES), 1) < M_HEADS
    o_ref[:, :LANES] = jnp.where(valid, x, 0.0).astype(o_ref.dtype)
    f_first = pltpu.roll(x, LANES - M_HEADS, axis=1)
    o_ref[:, LANES:] = jnp.where(valid, f_first, 0.0).astype(o_ref.dtype)


def _cast_kv_gates(w_in_t):
    _, _, k = w_in_t.shape
    params = _params(("parallel",))
    first = ATT_WIDTH // CAST_TILE
    w_kv = pl.pallas_call(
        _cast_body, grid=(2 * KV_WIDTH // CAST_TILE,),
        in_specs=[pl.BlockSpec((1, CAST_TILE, k), lambda j: (0, first + j, 0))],
        out_specs=pl.BlockSpec((k, CAST_TILE), lambda j: (0, j)),
        out_shape=jax.ShapeDtypeStruct((k, 2 * KV_WIDTH), BF16),
        compiler_params=params, name="cast_proj_kv",
    )(w_in_t)
    w_g = pl.pallas_call(
        _cast_gates_body, grid=(1,),
        in_specs=[pl.BlockSpec((1, LANES, k), lambda j: (0, PROJ_HEAD // LANES, 0))],
        out_specs=pl.BlockSpec((k, GATE_PAD), lambda j: (0, 0)),
        out_shape=jax.ShapeDtypeStruct((k, GATE_PAD), BF16),
        compiler_params=params, name="cast_proj_gates",
    )(w_in_t)
    return w_kv, w_g


def kernel(x_prompt, x_sample, cache_k_win, cache_v_win, state_C, state_n, state_m, c_prompt, c_sample, w_ada, b_ada, w_in, b_if, attn_sinks, w_up_a, w_up_b, w_o, ln1_g, ln1_b, w_ffn_in, w_ffn_out, ln2_g, ln2_b):
    bp, sp, d = x_prompt.shape
    bs, ts, _ = x_sample.shape
    keep = cache_k_win.shape[2]
    assert w_in.shape[0] == DEPTH == 1
    l = 0

    raw = {"w_in_t": jnp.swapaxes(w_in, 1, 2), "w_up_a": w_up_a, "w_up_b": w_up_b, "w_o": w_o,
           "w_ffn_in": w_ffn_in, "w_ffn_out": w_ffn_out}
    w_kv, w_g = _cast_kv_gates(raw["w_in_t"])
    pad_bias = jnp.zeros((LANES - M_HEADS,), F32)
    wts = {
        "w_kv": w_kv, "w_g": w_g,
        "gate_bias": jnp.concatenate([b_if[l, :M_HEADS], pad_bias, b_if[l, M_HEADS:], pad_bias]).reshape(1, GATE_PAD),
        "sink_tab": jnp.broadcast_to(attn_sinks[l].astype(F32).reshape(N_KV_HEADS, 1, GROUP, 1),
                                     (N_KV_HEADS, 1, GROUP, CHUNK)).reshape(N_KV_HEADS, 1, GROUP * CHUNK),
        "ln1_g": ln1_g[l], "ln1_b": ln1_b[l], "ln2_g": ln2_g[l], "ln2_b": ln2_b[l],
    }

    c_all = jnp.concatenate([c_prompt, c_sample], axis=0)
    rows = c_all.shape[0]
    rows_pad = -(-rows // 8) * 8
    c_all = jnp.pad(c_all, ((0, rows_pad - rows), (0, 0)))
    mod = _adaln(c_all, w_ada[l], b_ada[l])
    mod_p = mod[:bp].reshape(bp, 1, 6 * d)
    mod_s = mod[bp:bp + bs].reshape(bs, 1, 6 * d)

    def pad_lanes(m):
        return jnp.pad(m, ((0, 0), (0, LANES - M_HEADS))).reshape(m.shape[0], 1, LANES)

    ck = cache_k_win[l].reshape(bs, keep, KV_WIDTH)
    cv = cache_v_win[l].reshape(bs, keep, KV_WIDTH)
    state_s = (state_C[l], state_n[l], pad_lanes(state_m[l]))
    ys, kas, vas, cs, ns, ms = _group_layer(
        x_sample, mod_s, PAST_LEN + jnp.arange(ts), wts, raw, ck, cv, state_s,
        attn_bb=min(bs, ATTN_ROWS // ts), attn_tq=ts, mask_first=False)

    state_p = (jnp.zeros((bp, M_HEADS, M_V_DIM, M_QK_DIM), F32),
               jnp.zeros((bp, M_HEADS, M_QK_DIM), F32),
               pad_lanes(jnp.full((bp, M_HEADS), M_INIT, F32)))
    yp, kap, vap, cp, np_, mp = _group_layer(
        x_prompt, mod_p, jnp.arange(sp), wts, None, None, None, state_p,
        attn_bb=1, attn_tq=min(ATTN_ROWS, sp), mask_first=True)

    def heads(a):
        return a.reshape(a.shape[0], a.shape[1], N_KV_HEADS, HEAD_DIM)

    kws = jnp.concatenate([ck, kas], axis=1)[:, -keep:]
    vws = jnp.concatenate([cv, vas], axis=1)[:, -keep:]
    return (yp, ys,
            heads(kap[:, -keep:])[None], heads(vap[:, -keep:])[None],
            cp[None], np_[None], mp[:, 0, :M_HEADS][None],
            heads(kws)[None], heads(vws)[None],
            cs[None], ns[None], ms[:, 0, :M_HEADS][None])
```

```python
import functools

import jax
import jax.numpy as jnp
from jax import lax
from jax.experimental import pallas as pl
from jax.experimental.pallas import tpu as pltpu

F32 = jnp.float32
BF16 = jnp.bfloat16

D_MODEL = 4096
CHUNK = 64
N_Q_HEADS = 64
N_KV_HEADS = 8
HEAD_DIM = 64
GROUP = N_Q_HEADS // N_KV_HEADS
WINDOW = 128
ROT_DIM = HEAD_DIM // 4
ROPE_THETA = 500000.0
M_HEADS = 8
M_QK_DIM = 256
M_V_DIM = 512
ATT_WIDTH = N_Q_HEADS * HEAD_DIM
KV_WIDTH = N_KV_HEADS * HEAD_DIM
M_QK_WIDTH = M_HEADS * M_QK_DIM
M_V_WIDTH = M_HEADS * M_V_DIM
D_FF = 11008
DEPTH = 1
PAST_LEN = 2048
ALPHA = (2.0 * DEPTH) ** 0.25
LN_EPS = 1e-5
M_INIT = -1e30

LANES = 128
ROW_TILE = 1024
COL_TILE = 1024
ATTN_ROWS = 1024
MLSTM_ROWS = 256
GATE_PAD = 2 * LANES
VMEM_LIMIT = 60 * 1024 * 1024


def _params(semantics):
    return pltpu.CompilerParams(dimension_semantics=semantics, vmem_limit_bytes=VMEM_LIMIT)


def _row_blocks(batch, steps, rows):
    if steps >= rows:
        assert steps % rows == 0
        return 1, rows
    bb = min(batch, rows // steps)
    assert batch % bb == 0
    return bb, steps


def _adaln_body(c_ref, w_ref, b_ref, o_ref):
    c = c_ref[...]
    s = c * jax.nn.sigmoid(c)
    o_ref[...] = jnp.dot(s, w_ref[...], preferred_element_type=F32) + b_ref[...]


def _adaln(c, w_ada, b_ada):
    rows, d = c.shape
    n = w_ada.shape[1]
    tn = 512
    return pl.pallas_call(
        _adaln_body,
        grid=(n // tn,),
        in_specs=[pl.BlockSpec((rows, d), lambda j: (0, 0)),
                  pl.BlockSpec((d, tn), lambda j: (0, j)),
                  pl.BlockSpec((1, tn), lambda j: (0, j))],
        out_specs=pl.BlockSpec((rows, tn), lambda j: (0, j)),
        out_shape=jax.ShapeDtypeStruct((rows, n), F32),
        compiler_params=_params(("parallel",)),
        name="adaln",
    )(c, w_ada, b_ada.reshape(1, n))


def _load_rows(a_ref):
    bb, tt, k = a_ref.shape
    return a_ref[...].reshape(bb * tt, k)


def _rope_rows(x, cos_ref, sa_ref, sb_ref, bb):
    rows, n = x.shape
    tt = rows // bb

    def table(ref):
        t = ref[...]
        if bb == 1:
            return t
        return jnp.broadcast_to(t[None], (bb, tt, LANES)).reshape(rows, LANES)

    cos, sa, sb = table(cos_ref), table(sa_ref), table(sb_ref)
    outs = []
    for s in range(n // LANES):
        xs = x[:, s * LANES:(s + 1) * LANES]
        up = pltpu.roll(xs, LANES - ROT_DIM // 2, axis=1)
        dn = pltpu.roll(xs, ROT_DIM // 2, axis=1)
        outs.append(xs * cos + up * sa + dn * sb)
    return jnp.concatenate(outs, axis=1)


def _mm_plain_body(a_ref, w_ref, o_ref):
    acc = jnp.dot(_load_rows(a_ref), w_ref[...], preferred_element_type=F32)
    o_ref[...] = acc.reshape(o_ref.shape).astype(o_ref.dtype)


def _mm_rope_body(a_ref, w_ref, cos_ref, sa_ref, sb_ref, o_ref):
    acc = jnp.dot(_load_rows(a_ref), w_ref[...], preferred_element_type=F32)
    out = _rope_rows(acc, cos_ref, sa_ref, sb_ref, a_ref.shape[0])
    o_ref[...] = out.reshape(o_ref.shape).astype(o_ref.dtype)


def _mm_mod_kvg_body(x_ref, w_ref, wg_ref, sc_ref, sh_ref, cos_ref, sa_ref, sb_ref, u_ref, k_ref, v_ref, g_ref):
    u_ref[...] = (x_ref[...] * (1.0 + sc_ref[...]) + sh_ref[...]).astype(u_ref.dtype)
    a = _load_rows(u_ref)
    acc = jnp.dot(a, w_ref[...], preferred_element_type=F32)
    k = _rope_rows(acc[:, :KV_WIDTH], cos_ref, sa_ref, sb_ref, x_ref.shape[0])
    k_ref[...] = k.reshape(k_ref.shape)
    v_ref[...] = acc[:, KV_WIDTH:].reshape(v_ref.shape)
    g_ref[...] = jnp.dot(a, wg_ref[...], preferred_element_type=F32).reshape(g_ref.shape)


def _mm_gate_first_body(a_ref, w_ref, g_ref, o_ref):
    acc = jnp.dot(_load_rows(a_ref), w_ref[...], preferred_element_type=F32)
    gate = jax.nn.sigmoid(g_ref[...].astype(F32))
    o_ref[...] = gate * acc.reshape(o_ref.shape)


def _mm_gate_second_body(a_ref, w_ref, g_ref, t_ref, o_ref):
    acc = jnp.dot(_load_rows(a_ref), w_ref[...], preferred_element_type=F32)
    gate = jax.nn.sigmoid(g_ref[...].astype(F32))
    o_ref[...] = (t_ref[...] + gate * acc.reshape(o_ref.shape)).astype(o_ref.dtype)


def _mm_residual_body(a_ref, w_ref, x_ref, g_ref, o_ref):
    acc = jnp.dot(_load_rows(a_ref), w_ref[...], preferred_element_type=F32)
    o_ref[...] = ALPHA * x_ref[...] + g_ref[...] * acc.reshape(o_ref.shape)


def _mm_swiglu_body(a_ref, wg_ref, wu_ref, o_ref):
    a = _load_rows(a_ref)
    gate = jnp.dot(a, wg_ref[...], preferred_element_type=F32)
    up = jnp.dot(a, wu_ref[...], preferred_element_type=F32)
    out = gate * jax.nn.sigmoid(gate) * up
    o_ref[...] = out.reshape(o_ref.shape).astype(o_ref.dtype)


def _act_spec(bb, tt, width, col=None):
    if col is None:
        return pl.BlockSpec((bb, tt, width), lambda b, t, j: (b, t, j))
    return pl.BlockSpec((bb, tt, width), lambda b, t, j: (b, t, j + col))


def _matmul(body, a, weights, extras, extra_specs, outs, *, n_tiles, rows=ROW_TILE, name):
    batch, steps, k = a.shape
    bb, tt = _row_blocks(batch, steps, rows)
    grid = (batch // bb, steps // tt, n_tiles)
    in_specs = [pl.BlockSpec((bb, tt, k), lambda b, t, j: (b, t, 0))]
    in_specs += [pl.BlockSpec((k, cols), functools.partial(lambda b, t, j, col: (0, col(j)), col=col))
                 for _, cols, col in weights]
    in_specs += [spec(bb, tt) for spec in extra_specs]
    weights = [w for w, _, _ in weights]
    out_specs = [pl.BlockSpec((bb, tt, cols_tile), lambda b, t, j: (b, t, j)) for _, _, cols_tile in outs]
    out_shape = [jax.ShapeDtypeStruct((batch, steps, cols), dt) for cols, dt, _ in outs]
    single = len(outs) == 1
    return pl.pallas_call(
        body,
        grid=grid,
        in_specs=in_specs,
        out_specs=out_specs[0] if single else out_specs,
        out_shape=out_shape[0] if single else out_shape,
        compiler_params=_params(("parallel", "parallel", "parallel")),
        name=name,
    )(a, *weights, *extras)


def _matmul_casting(body, a, casts, extras, extra_specs, outs, *, n_tiles, name, n_side=0):
    batch, steps, k = a.shape

    def per_tile(spec):
        return pl.BlockSpec(spec.block_shape, lambda j: spec.index_map(0, 0, j))

    in_specs = [pl.BlockSpec((batch, steps, k), lambda j: (0, 0, 0), pipeline_mode=pl.Buffered(1))]
    sources = []
    for srcs, *_ in casts:
        for arr, shape, index in srcs:
            in_specs.append(pl.BlockSpec(shape, index))
            sources.append(arr)
    in_specs += [per_tile(spec(batch, steps)) for spec in extra_specs]
    out_specs = [pl.BlockSpec((batch, steps, cols_tile), lambda j: (0, 0, j)) for _, _, cols_tile in outs]
    out_shape = [jax.ShapeDtypeStruct((batch, steps, cols), dt) for cols, dt, _ in outs]
    out_specs += [pl.BlockSpec(block, index) for _, _, _, block, index in casts]
    out_shape += [jax.ShapeDtypeStruct(shape, BF16) for _, _, shape, _, _ in casts]
    counts = [len(srcs) for srcs, *_ in casts]
    transforms = [tr for _, tr, *_ in casts]
    n_used = len(casts) - n_side
    n_extra, n_out = len(extras), len(outs)

    def casting_body(a_ref, *refs):
        src_refs, pos = [], 0
        for n in counts:
            src_refs.append(refs[pos:pos + n])
            pos += n
        extra_refs = refs[pos:pos + n_extra]
        out_refs = refs[pos + n_extra:pos + n_extra + n_out]
        w_refs = refs[pos + n_extra + n_out:]
        for w_ref, srcs, transform in zip(w_refs, src_refs, transforms):
            w_ref[...] = transform(*[s[...] for s in srcs]).astype(w_ref.dtype)
        body(a_ref, *w_refs[:n_used], *extra_refs, *out_refs)

    return pl.pallas_call(
        casting_body,
        grid=(n_tiles,),
        in_specs=in_specs,
        out_specs=out_specs,
        out_shape=out_shape,
        compiler_params=_params(("arbitrary",)),
        name=name,
    )(a, *sources, *extras)


def _rope_specs():
    return [lambda bb, tt: pl.BlockSpec((tt, LANES), lambda b, t, j: (t, 0))] * 3


def _attn_body(q_ref, kc_ref, kp_ref, vc_ref, vp_ref, sink_ref, o_ref, *, mask_first):
    bb, tq, _ = q_ref.shape
    n_chunks = tq // CHUNK
    n_keys = WINDOW + CHUNK
    i = pl.program_id(1)
    h = pl.program_id(2)

    lane = lax.broadcasted_iota(jnp.int32, (1, LANES), 1)
    keep = (lane // HEAD_DIM) == (h % 2)
    sink = sink_ref[0]
    lane_head = lax.broadcasted_iota(jnp.int32, (CHUNK, 2 * LANES), 1) // HEAD_DIM
    out_low = lax.broadcasted_iota(jnp.int32, (CHUNK, LANES), 1) < HEAD_DIM
    key_row = lax.broadcasted_iota(jnp.int32, (n_keys, 1), 0)
    ones = jnp.ones((16, n_keys), BF16)

    def both_halves(cur_ref, prev_ref, bi):
        pair = jnp.concatenate([prev_ref[bi].astype(F32), cur_ref[bi].astype(F32)], axis=0)
        swapped = pltpu.roll(pair, HEAD_DIM, axis=1)
        return jnp.where(keep, pair, swapped)

    def scores(bi, j, k4):
        q = q_ref[bi, j * CHUNK:(j + 1) * CHUNK, :] * (HEAD_DIM ** -0.5)
        parts = []
        for cg in range(GROUP // 4):
            qc = q[:, cg * 2 * LANES:(cg + 1) * 2 * LANES]
            for r in range(4):
                parts.append(jnp.where(lane_head == r, qc, jnp.zeros_like(qc)))
        qrows = jnp.concatenate(parts, axis=0)
        keys = k4[j * CHUNK:j * CHUNK + n_keys]
        st = lax.dot_general(keys, qrows, (((1,), (1,)), ((), ())), preferred_element_type=F32)
        if mask_first and j < WINDOW // CHUNK:
            first_valid = WINDOW - (i * tq + j * CHUNK)
            st = jnp.where(key_row >= first_valid, st, -jnp.inf)
        return st

    def values_t(j, v2):
        vwin = v2[j * CHUNK:j * CHUNK + n_keys]
        head = vwin[:LANES].T
        tail = vwin[LANES:]
        tail = jnp.concatenate([tail, tail], axis=0).T
        vt = jnp.concatenate([head[:HEAD_DIM], tail[:HEAD_DIM, :CHUNK]], axis=1).astype(BF16)
        return jnp.concatenate([vt, ones], axis=0)

    def finish(st, lhs):
        m = jnp.maximum(jnp.max(st, axis=0, keepdims=True), sink)
        pt = jnp.exp(st - m).astype(BF16)
        ot = jnp.dot(lhs, pt, preferred_element_type=F32)
        den = ot[HEAD_DIM:HEAD_DIM + 1] + jnp.exp(sink - m)
        return ot[:HEAD_DIM] / den

    def emit(bi, j, on):
        for pr in range(GROUP // 2):
            x = on[:, pr * LANES:(pr + 1) * LANES]
            xt = jnp.concatenate([x, x], axis=0).T
            o_ref[bi, j * CHUNK:(j + 1) * CHUNK, pr * LANES:(pr + 1) * LANES] = (
                jnp.where(out_low, xt[:CHUNK], xt[CHUNK:]).astype(o_ref.dtype))

    units, sts, lhss = [], [], []
    for bi in range(bb):
        k2 = both_halves(kc_ref, kp_ref, bi)
        v2 = both_halves(vc_ref, vp_ref, bi)
        k4 = jnp.concatenate([k2, k2], axis=1).astype(BF16)
        for j in range(n_chunks):
            units.append((bi, j))
            sts.append(scores(bi, j, k4))
            lhss.append(values_t(j, v2))
    for (bi, j), st, lhs in zip(units, sts, lhss):
        emit(bi, j, finish(st, lhs))


def _attention(q, k_cur, v_cur, k_prev, v_prev, sink_tab, *, bb, tq, mask_first):
    batch, steps, _ = q.shape
    qw = GROUP * HEAD_DIM
    same_array = k_prev is k_cur
    per_tile = tq // WINDOW

    def prev_map(b, i, h):
        if same_array:
            return (b, jnp.maximum(i * per_tile - 1, 0), h // 2)
        return (b, 0, h // 2)

    return pl.pallas_call(
        functools.partial(_attn_body, mask_first=mask_first),
        grid=(batch // bb, steps // tq, N_KV_HEADS),
        in_specs=[pl.BlockSpec((bb, tq, qw), lambda b, i, h: (b, i, h)),
                  pl.BlockSpec((bb, tq, LANES), lambda b, i, h: (b, i, h // 2)),
                  pl.BlockSpec((bb, WINDOW, LANES), prev_map),
                  pl.BlockSpec((bb, tq, LANES), lambda b, i, h: (b, i, h // 2)),
                  pl.BlockSpec((bb, WINDOW, LANES), prev_map),
                  pl.BlockSpec((1, 1, GROUP * CHUNK), lambda b, i, h: (h, 0, 0))],
        out_specs=pl.BlockSpec((bb, tq, qw), lambda b, i, h: (b, i, h)),
        out_shape=jax.ShapeDtypeStruct((batch, steps, ATT_WIDTH), BF16),
        compiler_params=_params(("parallel", "parallel", "parallel")),
        name="attention",
    )(q, k_cur, k_prev, v_cur, v_prev, sink_tab)


def _log_sigmoid(x):
    return jnp.minimum(x, 0.0) - jnp.log1p(jnp.exp(-jnp.abs(x)))


def _mlstm_body(q_ref, k_ref, v_ref, o_ref, g_ref, bias_ref, c0_ref, n0_ref, m0_ref,
                y_ref, c_ref, n_ref, m_ref, *, single_step):
    if single_step:
        c_in, n_in, m_in = c0_ref, n0_ref, m0_ref
    else:
        c_in, n_in, m_in = c_ref, n_ref, m_ref

        @pl.when(pl.program_id(1) == 0)
        def _():
            c_ref[...] = c0_ref[...]
            n_ref[...] = n0_ref[...]
            m_ref[...] = m0_ref[...]

    steps = q_ref.shape[1]
    g = g_ref[0] + bias_ref[...]
    li = g[:, :LANES]
    lf = _log_sigmoid(g[:, LANES:])
    row = lax.broadcasted_iota(jnp.int32, (steps, steps), 0)
    col = lax.broadcasted_iota(jnp.int32, (steps, steps), 1)
    causal = col <= row
    b = jnp.dot(causal.astype(F32), lf, preferred_element_type=F32,
                precision=lax.Precision.HIGHEST)
    r = li - b
    m_prev = m_in[0]
    b_last = b[steps - 1:steps, :]
    m_new = b_last + jnp.maximum(m_prev, jnp.max(r, axis=0, keepdims=True))
    w_state = jnp.exp(r + (b_last - m_new))
    decay = jnp.exp(b_last + m_prev - m_new)
    r_rows = r.T

    for h in range(M_HEADS):
        r_row = r_rows[h:h + 1, :]
        dmat = jnp.where(causal, jnp.broadcast_to(r_row, (steps, steps)), -jnp.inf)
        m_h = m_prev[:, h:h + 1]
        gmax = jnp.maximum(jnp.max(dmat, axis=-1, keepdims=True), m_h)
        w_intra = jnp.exp(dmat - gmax)
        w_inter = jnp.exp(m_h - gmax)
        inv_floor = jnp.exp(-(b[:, h:h + 1] + gmax))

        q = q_ref[0, :, h * M_QK_DIM:(h + 1) * M_QK_DIM] * (M_QK_DIM ** -0.5)
        k = k_ref[0, :, h * M_QK_DIM:(h + 1) * M_QK_DIM]
        v = v_ref[0, :, h * M_V_DIM:(h + 1) * M_V_DIM]
        c_old = c_in[0, h]
        n_old = n_in[0, h:h + 1, :]

        qk = lax.dot_general(q, k, (((1,), (1,)), ((), ())), preferred_element_type=F32)
        s = qk * w_intra
        intra = jnp.dot(s.astype(BF16), v, preferred_element_type=F32)
        inter = lax.dot_general(q, c_old.astype(BF16), (((1,), (1,)), ((), ())),
                                preferred_element_type=F32)
        nq = jnp.sum(q.astype(F32) * n_old, axis=-1, keepdims=True)
        num = intra + w_inter * inter
        den = jnp.sum(s, axis=-1, keepdims=True) + w_inter * nq
        hval = num / jnp.maximum(jnp.abs(den), inv_floor)
        og = jax.nn.sigmoid(o_ref[0, :, h * M_V_DIM:(h + 1) * M_V_DIM].astype(F32))
        y_ref[0, :, h * M_V_DIM:(h + 1) * M_V_DIM] = (og * hval).astype(y_ref.dtype)

        ws = w_state[:, h:h + 1]
        vw = (v.astype(F32) * ws).astype(BF16)
        dc = lax.dot_general(vw, k, (((0,), (0,)), ((), ())), preferred_element_type=F32)
        dec = decay[:, h:h + 1]
        c_ref[0, h] = dec * c_old + dc
        n_ref[0, h:h + 1, :] = dec * n_old + jnp.sum(k.astype(F32) * ws, axis=0, keepdims=True)

    m_ref[0] = m_new


def _mlstm(za, zb, gates, bias, c0, n0, m0):
    batch, steps, _ = za.shape
    rows = min(steps, MLSTM_ROWS)
    state_specs = [pl.BlockSpec((1, M_HEADS, M_V_DIM, M_QK_DIM), lambda b, c: (b, 0, 0, 0)),
                   pl.BlockSpec((1, M_HEADS, M_QK_DIM), lambda b, c: (b, 0, 0)),
                   pl.BlockSpec((1, 1, LANES), lambda b, c: (b, 0, 0))]
    return pl.pallas_call(
        functools.partial(_mlstm_body, single_step=steps == rows),
        grid=(batch, steps // rows),
        in_specs=[pl.BlockSpec((1, rows, M_QK_WIDTH), lambda b, c: (b, c, 0)),
                  pl.BlockSpec((1, rows, M_QK_WIDTH), lambda b, c: (b, c, 1)),
                  pl.BlockSpec((1, rows, M_V_WIDTH), lambda b, c: (b, c, 1)),
                  pl.BlockSpec((1, rows, M_V_WIDTH), lambda b, c: (b, c, 0)),
                  pl.BlockSpec((1, rows, GATE_PAD), lambda b, c: (b, c, 0)),
                  pl.BlockSpec((1, GATE_PAD), lambda b, c: (0, 0))] + state_specs,
        out_specs=[pl.BlockSpec((1, rows, M_V_WIDTH), lambda b, c: (b, c, 0))] + state_specs,
        out_shape=[jax.ShapeDtypeStruct((batch, steps, M_V_WIDTH), BF16),
                   jax.ShapeDtypeStruct(c0.shape, F32),
                   jax.ShapeDtypeStruct(n0.shape, F32),
                   jax.ShapeDtypeStruct(m0.shape, F32)],
        compiler_params=_params(("parallel", "arbitrary")),
        name="mlstm",
    )(za, za, za, zb, gates, bias, c0, n0, m0)


def _ln_rows(x, g, b):
    mu = jnp.mean(x, axis=-1, keepdims=True)
    xc = x - mu
    var = jnp.mean(xc * xc, axis=-1, keepdims=True)
    return xc * lax.rsqrt(var + LN_EPS) * g + b


def _ln_mod_body(r_ref, g_ref, b_ref, sc_ref, sh_ref, x_ref, u_ref):
    y = _ln_rows(r_ref[...], g_ref[...], b_ref[...])
    x_ref[...] = y
    u_ref[...] = (y * (1.0 + sc_ref[...]) + sh_ref[...]).astype(u_ref.dtype)


def _ln_body(r_ref, g_ref, b_ref, x_ref):
    x_ref[...] = _ln_rows(r_ref[...], g_ref[...], b_ref[...])


def _layer_norm(r, g, b, mod=None, sc_idx=None, sh_idx=None):
    batch, steps, d = r.shape
    bb, tt = _row_blocks(batch, steps, ROW_TILE // 2)
    act = pl.BlockSpec((bb, tt, d), lambda i, t: (i, t, 0))
    vec = pl.BlockSpec((1, 1, d), lambda i, t: (0, 0, 0))
    g3, b3 = g.reshape(1, 1, d), b.reshape(1, 1, d)
    if mod is None:
        return pl.pallas_call(
            _ln_body, grid=(batch // bb, steps // tt),
            in_specs=[act, vec, vec], out_specs=act,
            out_shape=jax.ShapeDtypeStruct(r.shape, F32),
            compiler_params=_params(("parallel", "parallel")), name="layer_norm",
        )(r, g3, b3)
    return pl.pallas_call(
        _ln_mod_body, grid=(batch // bb, steps // tt),
        in_specs=[act, vec, vec,
                  pl.BlockSpec((bb, 1, d), lambda i, t: (i, 0, sc_idx)),
                  pl.BlockSpec((bb, 1, d), lambda i, t: (i, 0, sh_idx))],
        out_specs=[act, act],
        out_shape=[jax.ShapeDtypeStruct(r.shape, F32), jax.ShapeDtypeStruct(r.shape, BF16)],
        compiler_params=_params(("parallel", "parallel")), name="layer_norm_modulate",
    )(r, g3, b3, mod, mod)


SH1, SC1, GT1, SH2, SC2, GT2 = range(6)


def _rope_tables(pos):
    half = ROT_DIM // 2
    inv = ROPE_THETA ** (-2.0 * jnp.arange(half, dtype=F32) / ROT_DIM)
    ang = pos.astype(F32)[:, None] * inv[None, :]
    cos, sin = jnp.cos(ang), jnp.sin(ang)
    zeros = jnp.zeros_like(cos)
    rest = HEAD_DIM - ROT_DIM
    steps = pos.shape[0]
    one_head = lambda first, second, fill: jnp.concatenate(
        [first, second, jnp.full((steps, rest), fill, F32)], axis=1)
    cos_t = one_head(cos, cos, 1.0)
    up_t = one_head(-sin, zeros, 0.0)
    dn_t = one_head(zeros, sin, 0.0)
    reps = LANES // HEAD_DIM
    return tuple(jnp.tile(t, (1, reps)) for t in (cos_t, up_t, dn_t))


CAST_TILE = 512
FFN_TILE = 256
MERGE_CAST_TILE = 256
GATE_COLS = 2 * M_HEADS
PROJ_HEAD = ATT_WIDTH + 2 * KV_WIDTH + 2 * M_QK_WIDTH + M_V_WIDTH


def _group_layer(x, mod, pos, wts, raw, k_prev, v_prev, state, *, attn_bb, attn_tq, mask_first):
    batch, steps, d = x.shape
    ropes = _rope_tables(pos)
    casting = raw is not None
    tile = CAST_TILE if casting else COL_TILE

    def dense(name, body, act, keys, casts, extras, extra_specs, outs, n_cols, wtile, side_keys=(), rows=ROW_TILE):
        n_tiles = -(-n_cols // wtile)
        if not casting:
            weights = [(wts[key], wtile, lambda j: j) for key in keys]
            return _matmul(body, act, weights, extras, extra_specs, outs, n_tiles=n_tiles, rows=rows, name=name)
        res = _matmul_casting(body, act, casts, extras, extra_specs, outs, n_tiles=n_tiles, name=name,
                              n_side=len(side_keys))
        for key, w in zip(list(keys) + list(side_keys), res[len(outs):]):
            wts[key] = w
        return res[0] if len(outs) == 1 else res[:len(outs)]

    def proj_rows(first, total):
        srcs = [(raw["w_in_t"], (1, tile, d), lambda j: (0, first // tile + j, 0))]
        return [(srcs, lambda w: w[0].T, (d, total), (d, tile), lambda j: (0, j))]

    def proj_rows_shifted(first, total):
        base = first - GATE_COLS
        srcs = [(raw["w_in_t"], (1, tile, d), lambda j: (0, base // tile + j, 0)),
                (raw["w_in_t"], (1, GATE_COLS, d), lambda j: (0, (base + tile * (j + 1)) // GATE_COLS, 0))]

        def transform(main, nxt):
            rows = jnp.concatenate([main[0], nxt[0]], axis=0)
            return rows[GATE_COLS:GATE_COLS + tile].T

        return [(srcs, transform, (d, total), (d, tile), lambda j: (0, j))]

    def columns(key, first, total, width):
        srcs = [(raw[key], (1, d, width), lambda j: (0, 0, first // width + j))]
        return (srcs, lambda w: w[0], (d, total), (d, width), lambda j: (0, j))

    def mod_full(idx):
        return lambda bb, tt: pl.BlockSpec((bb, 1, d), lambda b, t_, j: (b, 0, idx))

    u, ka, va, gates = _matmul(
        _mm_mod_kvg_body, x, [(wts["w_kv"], 2 * KV_WIDTH, lambda j: 0), (wts["w_g"], GATE_PAD, lambda j: 0)],
        [mod, mod, *ropes], [mod_full(SC1), mod_full(SH1)] + _rope_specs(),
        [(d, BF16, d), (KV_WIDTH, F32, KV_WIDTH), (KV_WIDTH, F32, KV_WIDTH), (GATE_PAD, F32, GATE_PAD)],
        n_tiles=1, rows=ROW_TILE // 2, name="proj_kvg")

    mem_first = ATT_WIDTH + 2 * KV_WIDTH
    za_width = 2 * M_QK_WIDTH + M_V_WIDTH
    zb_width = M_V_WIDTH + 2 * d
    qa = dense("proj_q", _mm_rope_body, u, ["w_qa"], casting and proj_rows(0, ATT_WIDTH),
               ropes, _rope_specs(), [(ATT_WIDTH, BF16, tile)], ATT_WIDTH, tile)
    za = dense("proj_mem", _mm_plain_body, u, ["w_mem"], casting and proj_rows(mem_first, za_width),
               [], [], [(za_width, BF16, tile)], za_width, tile)
    zb = dense("proj_gates", _mm_plain_body, u, ["w_gates"],
               casting and proj_rows_shifted(PROJ_HEAD + GATE_COLS, zb_width),
               [], [], [(zb_width, BF16, tile)], zb_width, tile)

    if k_prev is None:
        k_prev, v_prev = ka, va
    ya = _attention(qa, ka, va, k_prev, v_prev, wts["sink_tab"], bb=attn_bb, tq=attn_tq, mask_first=mask_first)

    yb, c_new, n_new, m_new = _mlstm(za, zb, gates, wts["gate_bias"], *state)

    mtile = MERGE_CAST_TILE if casting else COL_TILE
    ga_col = M_V_WIDTH // mtile
    gb_col = ga_col + d // mtile
    t = dense("merge_a", _mm_gate_first_body, ya, ["w_up_a"], casting and [columns("w_up_a", 0, d, mtile)],
              [zb], [lambda bb, tt: _act_spec(bb, tt, mtile, ga_col)], [(d, F32, mtile)], d, mtile)
    merged = dense("merge_b", _mm_gate_second_body, yb, ["w_up_b"], casting and [columns("w_up_b", 0, d, mtile)],
                   [zb, t], [lambda bb, tt: _act_spec(bb, tt, mtile, gb_col),
                             lambda bb, tt: _act_spec(bb, tt, mtile)], [(d, BF16, mtile)], d, mtile)

    def mod_spec(idx, width):
        per_vec = d // width
        return lambda bb, tt: pl.BlockSpec((bb, 1, width), lambda b, t_, j: (b, 0, idx * per_vec + j))

    r1 = dense("out_proj", _mm_residual_body, merged, ["w_o"], casting and [columns("w_o", 0, d, mtile)],
               [x, mod], [lambda bb, tt: _act_spec(bb, tt, mtile), mod_spec(GT1, mtile)],
               [(d, F32, mtile)], d, mtile)
    x1, u2 = _layer_norm(r1, wts["ln1_g"], wts["ln1_b"], mod, SC2, SH2)

    ff_tile = FFN_TILE
    ffn_casts = casting and [
        columns("w_ffn_in", 0, D_FF, ff_tile), columns("w_ffn_in", D_FF, D_FF, ff_tile),
        ([(raw["w_ffn_out"], (1, ff_tile, d), lambda j: (0, j, 0))], lambda w: w[0],
         (D_FF, d), (ff_tile, d), lambda j: (j, 0))]
    hidden = dense("ffn_in", _mm_swiglu_body, u2, ["w_ffn_gate", "w_ffn_up"], ffn_casts,
                   [], [], [(D_FF, BF16, ff_tile)], D_FF, ff_tile, side_keys=["w_ffn_out"] if casting else (),
                   rows=2 * ROW_TILE)

    out_tile = COL_TILE // 2
    r2 = _matmul(_mm_residual_body, hidden, [(wts["w_ffn_out"], out_tile, lambda j: j)], [x1, mod],
                 [lambda bb, tt: _act_spec(bb, tt, out_tile), mod_spec(GT2, out_tile)],
                 [(d, F32, out_tile)], n_tiles=d // out_tile, rows=ROW_TILE // 2, name="ffn_out")
    y = _layer_norm(r2, wts["ln2_g"], wts["ln2_b"])
    return y, ka, va, c_new, n_new, m_new


def _cast_body(w_ref, o_ref):
    o_ref[...] = w_ref[0].T.astype(o_ref.dtype)


def _cast_gates_body(w_ref, o_ref):
    x = w_ref[0].T
    valid = lax.broadcasted_iota(jnp.int32, (1, LANES), 1) < M_HEADS
    o_ref[:, :LANES] = jnp.where(valid, x, 0.0).astype(o_ref.dtype)
    f_first = pltpu.roll(x, LANES - M_HEADS, axis=1)
    o_ref[:, LANES:] = jnp.where(valid, f_first, 0.0).astype(o_ref.dtype)


def _cast_kv_gates(w_in_t):
    _, _, k = w_in_t.shape
    params = _params(("parallel",))
    first = ATT_WIDTH // CAST_TILE
    w_kv = pl.pallas_call(
        _cast_body, grid=(2 * KV_WIDTH // CAST_TILE,),
        in_specs=[pl.BlockSpec((1, CAST_TILE, k), lambda j: (0, first + j, 0))],
        out_specs=pl.BlockSpec((k, CAST_TILE), lambda j: (0, j)),
        out_shape=jax.ShapeDtypeStruct((k, 2 * KV_WIDTH), BF16),
        compiler_params=params, name="cast_proj_kv",
    )(w_in_t)
    w_g = pl.pallas_call(
        _cast_gates_body, grid=(1,),
        in_specs=[pl.BlockSpec((1, LANES, k), lambda j: (0, PROJ_HEAD // LANES, 0))],
        out_specs=pl.BlockSpec((k, GATE_PAD), lambda j: (0, 0)),
        out_shape=jax.ShapeDtypeStruct((k, GATE_PAD), BF16),
        compiler_params=params, name="cast_proj_gates",
    )(w_in_t)
    return w_kv, w_g


def kernel(x_prompt, x_sample, cache_k_win, cache_v_win, state_C, state_n, state_m, c_prompt, c_sample, w_ada, b_ada, w_in, b_if, attn_sinks, w_up_a, w_up_b, w_o, ln1_g, ln1_b, w_ffn_in, w_ffn_out, ln2_g, ln2_b):
    bp, sp, d = x_prompt.shape
    bs, ts, _ = x_sample.shape
    keep = cache_k_win.shape[2]
    assert w_in.shape[0] == DEPTH == 1
    l = 0

    raw = {"w_in_t": jnp.swapaxes(w_in, 1, 2), "w_up_a": w_up_a, "w_up_b": w_up_b, "w_o": w_o,
           "w_ffn_in": w_ffn_in, "w_ffn_out": w_ffn_out}
    w_kv, w_g = _cast_kv_gates(raw["w_in_t"])
    pad_bias = jnp.zeros((LANES - M_HEADS,), F32)
    wts = {
        "w_kv": w_kv, "w_g": w_g,
        "gate_bias": jnp.concatenate([b_if[l, :M_HEADS], pad_bias, b_if[l, M_HEADS:], pad_bias]).reshape(1, GATE_PAD),
        "sink_tab": jnp.broadcast_to(attn_sinks[l].astype(F32).reshape(N_KV_HEADS, 1, GROUP, 1),
                                     (N_KV_HEADS, 1, GROUP, CHUNK)).reshape(N_KV_HEADS, 1, GROUP * CHUNK),
        "ln1_g": ln1_g[l], "ln1_b": ln1_b[l], "ln2_g": ln2_g[l], "ln2_b": ln2_b[l],
    }

    c_all = jnp.concatenate([c_prompt, c_sample], axis=0)
    rows = c_all.shape[0]
    rows_pad = -(-rows // 8) * 8
    c_all = jnp.pad(c_all, ((0, rows_pad - rows), (0, 0)))
    mod = _adaln(c_all, w_ada[l], b_ada[l])
    mod_p = mod[:bp].reshape(bp, 1, 6 * d)
    mod_s = mod[bp:bp + bs].reshape(bs, 1, 6 * d)

    def pad_lanes(m):
        return jnp.pad(m, ((0, 0), (0, LANES - M_HEADS))).reshape(m.shape[0], 1, LANES)

    ck = cache_k_win[l].reshape(bs, keep, KV_WIDTH)
    cv = cache_v_win[l].reshape(bs, keep, KV_WIDTH)
    state_s = (state_C[l], state_n[l], pad_lanes(state_m[l]))
    ys, kas, vas, cs, ns, ms = _group_layer(
        x_sample, mod_s, PAST_LEN + jnp.arange(ts), wts, raw, ck, cv, state_s,
        attn_bb=min(bs, ATTN_ROWS // ts), attn_tq=ts, mask_first=False)

    state_p = (jnp.zeros((bp, M_HEADS, M_V_DIM, M_QK_DIM), F32),
               jnp.zeros((bp, M_HEADS, M_QK_DIM), F32),
               pad_lanes(jnp.full((bp, M_HEADS), M_INIT, F32)))
    yp, kap, vap, cp, np_, mp = _group_layer(
        x_prompt, mod_p, jnp.arange(sp), wts, None, None, None, state_p,
        attn_bb=1, attn_tq=min(ATTN_ROWS, sp), mask_first=True)

    def heads(a):
        return a.reshape(a.shape[0], a.shape[1], N_KV_HEADS, HEAD_DIM)

    kws = jnp.concatenate([ck, kas], axis=1)[:, -keep:]
    vws = jnp.concatenate([cv, vas], axis=1)[:, -keep:]
    return (yp, ys,
            heads(kap[:, -keep:])[None], heads(vap[:, -keep:])[None],
            cp[None], np_[None], mp[:, 0, :M_HEADS][None],
            heads(kws)[None], heads(vws)[None],
            cs[None], ns[None], ms[:, 0, :M_HEADS][None])
```

```python
import functools

import jax
import jax.numpy as jnp
from jax import lax
from jax.experimental import pallas as pl
from jax.experimental.pallas import tpu as pltpu

F32 = jnp.float32
BF16 = jnp.bfloat16

D_MODEL = 4096
CHUNK = 64
N_Q_HEADS = 64
N_KV_HEADS = 8
HEAD_DIM = 64
GROUP = N_Q_HEADS // N_KV_HEADS
WINDOW = 128
ROT_DIM = HEAD_DIM // 4
ROPE_THETA = 500000.0
M_HEADS = 8
M_QK_DIM = 256
M_V_DIM = 512
ATT_WIDTH = N_Q_HEADS * HEAD_DIM
KV_WIDTH = N_KV_HEADS * HEAD_DIM
M_QK_WIDTH = M_HEADS * M_QK_DIM
M_V_WIDTH = M_HEADS * M_V_DIM
D_FF = 11008
DEPTH = 1
PAST_LEN = 2048
ALPHA = (2.0 * DEPTH) ** 0.25
LN_EPS = 1e-5
M_INIT = -1e30

LANES = 128
ROW_TILE = 1024
COL_TILE = 1024
ROW_BANDS = 4
ADA_TILE = 512
ATTN_ROWS = 1024
MLSTM_ROWS = 256
GATE_PAD = 2 * LANES
VMEM_LIMIT = 60 * 1024 * 1024


def _params(semantics):
    return pltpu.CompilerParams(dimension_semantics=semantics, vmem_limit_bytes=VMEM_LIMIT)


def _row_blocks(batch, steps, rows):
    if steps >= rows:
        assert steps % rows == 0
        return 1, rows
    bb = min(batch, rows // steps)
    assert batch % bb == 0
    return bb, steps


def _adaln_body(c_ref, w_ref, b_ref, o_ref):
    c = c_ref[...]
    s = c * jax.nn.sigmoid(c)
    o_ref[...] = jnp.dot(s, w_ref[...], preferred_element_type=F32) + b_ref[...]


def _adaln(c, w_ada, b_ada):
    rows, d = c.shape
    n = w_ada.shape[1]
    tn = ADA_TILE
    return pl.pallas_call(
        _adaln_body,
        grid=(n // tn,),
        in_specs=[pl.BlockSpec((rows, d), lambda j: (0, 0)),
                  pl.BlockSpec((d, tn), lambda j: (0, j)),
                  pl.BlockSpec((1, tn), lambda j: (0, j))],
        out_specs=pl.BlockSpec((rows, tn), lambda j: (0, j)),
        out_shape=jax.ShapeDtypeStruct((rows, n), F32),
        compiler_params=_params(("parallel",)),
        name="adaln",
    )(c, w_ada, b_ada.reshape(1, n))


def _load_rows(a_ref):
    bb, tt, k = a_ref.shape
    return a_ref[...].reshape(bb * tt, k)


def _bands(ref):
    bb, tt = ref.shape[0], ref.shape[1]
    if bb > 1:
        n = min(ROW_BANDS, bb)
        assert bb % n == 0
        return [(slice(p * (bb // n), (p + 1) * (bb // n)), slice(0, tt)) for p in range(n)]
    assert tt % ROW_BANDS == 0
    return [(slice(0, 1), slice(p * (tt // ROW_BANDS), (p + 1) * (tt // ROW_BANDS))) for p in range(ROW_BANDS)]


def _band(ref, bs, ts):
    if ref.shape[1] == 1:
        return ref[bs, :, :]
    blk = ref[bs, ts, :]
    return blk.reshape(blk.shape[0] * blk.shape[1], blk.shape[2])


def _store_band(ref, bs, ts, rows):
    nb = bs.stop - bs.start
    ref[bs, ts, :] = rows.reshape(nb, ts.stop - ts.start, rows.shape[-1]).astype(ref.dtype)


def _rope_rows(x, cos_ref, sa_ref, sb_ref, bs, ts):
    rows, n = x.shape
    nb = bs.stop - bs.start

    def table(ref):
        t = ref[ts, :]
        if nb == 1:
            return t
        return jnp.broadcast_to(t[None], (nb, t.shape[0], LANES)).reshape(rows, LANES)

    cos, sa, sb = table(cos_ref), table(sa_ref), table(sb_ref)
    outs = []
    for s in range(n // LANES):
        xs = x[:, s * LANES:(s + 1) * LANES]
        up = pltpu.roll(xs, LANES - ROT_DIM // 2, axis=1)
        dn = pltpu.roll(xs, ROT_DIM // 2, axis=1)
        outs.append(xs * cos + up * sa + dn * sb)
    return jnp.concatenate(outs, axis=1)


def _mm_plain_body(a_ref, w_ref, o_ref):
    acc = jnp.dot(_load_rows(a_ref), w_ref[...], preferred_element_type=F32)
    o_ref[...] = acc.reshape(o_ref.shape).astype(o_ref.dtype)


def _mm_rope_body(a_ref, w_ref, cos_ref, sa_ref, sb_ref, o_ref):
    for bs, ts in _bands(a_ref):
        acc = jnp.dot(_band(a_ref, bs, ts), w_ref[...], preferred_element_type=F32)
        _store_band(o_ref, bs, ts, _rope_rows(acc, cos_ref, sa_ref, sb_ref, bs, ts))


def _mm_mod_kvg_body(x_ref, w_ref, wg_ref, sc_ref, sh_ref, cos_ref, sa_ref, sb_ref, u_ref, k_ref, v_ref, g_ref):
    u_ref[...] = (x_ref[...] * (1.0 + sc_ref[...]) + sh_ref[...]).astype(u_ref.dtype)
    for bs, ts in _bands(x_ref):
        a = _band(u_ref, bs, ts)
        acc = jnp.dot(a, w_ref[...], preferred_element_type=F32)
        _store_band(k_ref, bs, ts, _rope_rows(acc[:, :KV_WIDTH], cos_ref, sa_ref, sb_ref, bs, ts))
        _store_band(v_ref, bs, ts, acc[:, KV_WIDTH:])
        _store_band(g_ref, bs, ts, jnp.dot(a, wg_ref[...], preferred_element_type=F32))


def _gated(acc, g_ref, bs, ts):
    return jax.nn.sigmoid(_band(g_ref, bs, ts).astype(F32)) * acc


def _mm_gate_first_body(a_ref, w_ref, g_ref, o_ref):
    for bs, ts in _bands(a_ref):
        acc = jnp.dot(_band(a_ref, bs, ts), w_ref[...], preferred_element_type=F32)
        _store_band(o_ref, bs, ts, _gated(acc, g_ref, bs, ts))


def _mm_gate_second_body(a_ref, w_ref, g_ref, t_ref, o_ref):
    for bs, ts in _bands(a_ref):
        acc = jnp.dot(_band(a_ref, bs, ts), w_ref[...], preferred_element_type=F32)
        _store_band(o_ref, bs, ts, _band(t_ref, bs, ts) + _gated(acc, g_ref, bs, ts))


def _mm_residual_body(a_ref, w_ref, x_ref, g_ref, o_ref):
    for bs, ts in _bands(a_ref):
        acc = jnp.dot(_band(a_ref, bs, ts), w_ref[...], preferred_element_type=F32)
        nb, nt = bs.stop - bs.start, ts.stop - ts.start
        scaled = g_ref[bs, :, :] * acc.reshape(nb, nt, acc.shape[-1])
        o_ref[bs, ts, :] = ALPHA * x_ref[bs, ts, :] + scaled


def _mm_swiglu_body(a_ref, wg_ref, wu_ref, o_ref):
    for bs, ts in _bands(a_ref):
        a = _band(a_ref, bs, ts)
        gate = jnp.dot(a, wg_ref[...], preferred_element_type=F32)
        up = jnp.dot(a, wu_ref[...], preferred_element_type=F32)
        _store_band(o_ref, bs, ts, gate * jax.nn.sigmoid(gate) * up)


def _act_spec(bb, tt, width, col=None):
    if col is None:
        return pl.BlockSpec((bb, tt, width), lambda b, t, j: (b, t, j))
    return pl.BlockSpec((bb, tt, width), lambda b, t, j: (b, t, j + col))


def _matmul(body, a, weights, extras, extra_specs, outs, *, n_tiles, rows=ROW_TILE, name):
    batch, steps, k = a.shape
    bb, tt = _row_blocks(batch, steps, rows)
    grid = (batch // bb, steps // tt, n_tiles)
    in_specs = [pl.BlockSpec((bb, tt, k), lambda b, t, j: (b, t, 0))]
    in_specs += [pl.BlockSpec((k, cols), functools.partial(lambda b, t, j, col: (0, col(j)), col=col))
                 for _, cols, col in weights]
    in_specs += [spec(bb, tt) for spec in extra_specs]
    weights = [w for w, _, _ in weights]
    out_specs = [pl.BlockSpec((bb, tt, cols_tile), lambda b, t, j: (b, t, j)) for _, _, cols_tile in outs]
    out_shape = [jax.ShapeDtypeStruct((batch, steps, cols), dt) for cols, dt, _ in outs]
    single = len(outs) == 1
    return pl.pallas_call(
        body,
        grid=grid,
        in_specs=in_specs,
        out_specs=out_specs[0] if single else out_specs,
        out_shape=out_shape[0] if single else out_shape,
        compiler_params=_params(("parallel", "parallel", "parallel")),
        name=name,
    )(a, *weights, *extras)


def _matmul_casting(body, a, casts, extras, extra_specs, outs, *, n_tiles, name, n_side=0):
    batch, steps, k = a.shape

    def per_tile(spec):
        return pl.BlockSpec(spec.block_shape, lambda j: spec.index_map(0, 0, j))

    in_specs = [pl.BlockSpec((batch, steps, k), lambda j: (0, 0, 0), pipeline_mode=pl.Buffered(1))]
    sources = []
    for srcs, *_ in casts:
        for arr, shape, index in srcs:
            in_specs.append(pl.BlockSpec(shape, index))
            sources.append(arr)
    in_specs += [per_tile(spec(batch, steps)) for spec in extra_specs]
    out_specs = [pl.BlockSpec((batch, steps, cols_tile), lambda j: (0, 0, j)) for _, _, cols_tile in outs]
    out_shape = [jax.ShapeDtypeStruct((batch, steps, cols), dt) for cols, dt, _ in outs]
    out_specs += [pl.BlockSpec(block, index) for _, _, _, block, index in casts]
    out_shape += [jax.ShapeDtypeStruct(shape, BF16) for _, _, shape, _, _ in casts]
    counts = [len(srcs) for srcs, *_ in casts]
    transforms = [tr for _, tr, *_ in casts]
    n_used = len(casts) - n_side
    n_extra, n_out = len(extras), len(outs)

    def casting_body(a_ref, *refs):
        src_refs, pos = [], 0
        for n in counts:
            src_refs.append(refs[pos:pos + n])
            pos += n
        extra_refs = refs[pos:pos + n_extra]
        out_refs = refs[pos + n_extra:pos + n_extra + n_out]
        w_refs = refs[pos + n_extra + n_out:]
        for w_ref, srcs, transform in zip(w_refs, src_refs, transforms):
            w_ref[...] = transform(*[s[...] for s in srcs]).astype(w_ref.dtype)
        body(a_ref, *w_refs[:n_used], *extra_refs, *out_refs)

    return pl.pallas_call(
        casting_body,
        grid=(n_tiles,),
        in_specs=in_specs,
        out_specs=out_specs,
        out_shape=out_shape,
        compiler_params=_params(("arbitrary",)),
        name=name,
    )(a, *sources, *extras)


def _rope_specs():
    return [lambda bb, tt: pl.BlockSpec((tt, LANES), lambda b, t, j: (t, 0))] * 3


def _attn_body(q_ref, kc_ref, kp_ref, vc_ref, vp_ref, sink_ref, o_ref, *, mask_first):
    bb, tq, _ = q_ref.shape
    n_chunks = tq // CHUNK
    n_keys = WINDOW + CHUNK
    i = pl.program_id(1)
    h = pl.program_id(2)

    lane = lax.broadcasted_iota(jnp.int32, (1, LANES), 1)
    keep = (lane // HEAD_DIM) == (h % 2)
    sink = sink_ref[0]
    lane_head = lax.broadcasted_iota(jnp.int32, (CHUNK, 2 * LANES), 1) // HEAD_DIM
    out_low = lax.broadcasted_iota(jnp.int32, (CHUNK, LANES), 1) < HEAD_DIM
    key_row = lax.broadcasted_iota(jnp.int32, (n_keys, 1), 0)
    ones = jnp.ones((16, n_keys), BF16)

    def both_halves(cur_ref, prev_ref, bi):
        pair = jnp.concatenate([prev_ref[bi].astype(F32), cur_ref[bi].astype(F32)], axis=0)
        swapped = pltpu.roll(pair, HEAD_DIM, axis=1)
        return jnp.where(keep, pair, swapped)

    def scores(bi, j, k4):
        q = q_ref[bi, j * CHUNK:(j + 1) * CHUNK, :] * (HEAD_DIM ** -0.5)
        parts = []
        for cg in range(GROUP // 4):
            qc = q[:, cg * 2 * LANES:(cg + 1) * 2 * LANES]
            for r in range(4):
                parts.append(jnp.where(lane_head == r, qc, jnp.zeros_like(qc)))
        qrows = jnp.concatenate(parts, axis=0)
        keys = k4[j * CHUNK:j * CHUNK + n_keys]
        st = lax.dot_general(keys, qrows, (((1,), (1,)), ((), ())), preferred_element_type=F32)
        if mask_first and j < WINDOW // CHUNK:
            first_valid = WINDOW - (i * tq + j * CHUNK)
            st = jnp.where(key_row >= first_valid, st, -jnp.inf)
        return st

    def values_t(j, v2):
        vwin = v2[j * CHUNK:j * CHUNK + n_keys]
        head = vwin[:LANES].T
        tail = vwin[LANES:]
        tail = jnp.concatenate([tail, tail], axis=0).T
        vt = jnp.concatenate([head[:HEAD_DIM], tail[:HEAD_DIM, :CHUNK]], axis=1).astype(BF16)
        return jnp.concatenate([vt, ones], axis=0)

    def finish(st, lhs):
        m = jnp.maximum(jnp.max(st, axis=0, keepdims=True), sink)
        pt = jnp.exp(st - m).astype(BF16)
        ot = jnp.dot(lhs, pt, preferred_element_type=F32)
        den = ot[HEAD_DIM:HEAD_DIM + 1] + jnp.exp(sink - m)
        return ot[:HEAD_DIM] / den

    def emit(bi, j, on):
        for pr in range(GROUP // 2):
            x = on[:, pr * LANES:(pr + 1) * LANES]
            xt = jnp.concatenate([x, x], axis=0).T
            o_ref[bi, j * CHUNK:(j + 1) * CHUNK, pr * LANES:(pr + 1) * LANES] = (
                jnp.where(out_low, xt[:CHUNK], xt[CHUNK:]).astype(o_ref.dtype))

    units, sts, lhss = [], [], []
    for bi in range(bb):
        k2 = both_halves(kc_ref, kp_ref, bi)
        v2 = both_halves(vc_ref, vp_ref, bi)
        k4 = jnp.concatenate([k2, k2], axis=1).astype(BF16)
        for j in range(n_chunks):
            units.append((bi, j))
            sts.append(scores(bi, j, k4))
            lhss.append(values_t(j, v2))
    for (bi, j), st, lhs in zip(units, sts, lhss):
        emit(bi, j, finish(st, lhs))


def _attention(q, k_cur, v_cur, k_prev, v_prev, sink_tab, *, bb, tq, mask_first):
    batch, steps, _ = q.shape
    qw = GROUP * HEAD_DIM
    same_array = k_prev is k_cur
    per_tile = tq // WINDOW

    def prev_map(b, i, h):
        if same_array:
            return (b, jnp.maximum(i * per_tile - 1, 0), h // 2)
        return (b, 0, h // 2)

    return pl.pallas_call(
        functools.partial(_attn_body, mask_first=mask_first),
        grid=(batch // bb, steps // tq, N_KV_HEADS),
        in_specs=[pl.BlockSpec((bb, tq, qw), lambda b, i, h: (b, i, h)),
                  pl.BlockSpec((bb, tq, LANES), lambda b, i, h: (b, i, h // 2)),
                  pl.BlockSpec((bb, WINDOW, LANES), prev_map),
                  pl.BlockSpec((bb, tq, LANES), lambda b, i, h: (b, i, h // 2)),
                  pl.BlockSpec((bb, WINDOW, LANES), prev_map),
                  pl.BlockSpec((1, 1, GROUP * CHUNK), lambda b, i, h: (h, 0, 0))],
        out_specs=pl.BlockSpec((bb, tq, qw), lambda b, i, h: (b, i, h)),
        out_shape=jax.ShapeDtypeStruct((batch, steps, ATT_WIDTH), BF16),
        compiler_params=_params(("parallel", "parallel", "parallel")),
        name="attention",
    )(q, k_cur, k_prev, v_cur, v_prev, sink_tab)


def _log_sigmoid(x):
    return jnp.minimum(x, 0.0) - jnp.log1p(jnp.exp(-jnp.abs(x)))


def _mlstm_body(q_ref, k_ref, v_ref, o_ref, g_ref, bias_ref, c0_ref, n0_ref, m0_ref,
                y_ref, c_ref, n_ref, m_ref, *, single_step):
    if single_step:
        c_in, n_in, m_in = c0_ref, n0_ref, m0_ref
    else:
        c_in, n_in, m_in = c_ref, n_ref, m_ref

        @pl.when(pl.program_id(1) == 0)
        def _():
            c_ref[...] = c0_ref[...]
            n_ref[...] = n0_ref[...]
            m_ref[...] = m0_ref[...]

    steps = q_ref.shape[1]
    g = g_ref[0] + bias_ref[...]
    li = g[:, :LANES]
    lf = _log_sigmoid(g[:, LANES:])
    row = lax.broadcasted_iota(jnp.int32, (steps, steps), 0)
    col = lax.broadcasted_iota(jnp.int32, (steps, steps), 1)
    causal = col <= row
    b = jnp.dot(causal.astype(F32), lf, preferred_element_type=F32,
                precision=lax.Precision.HIGHEST)
    r = li - b
    m_prev = m_in[0]
    b_last = b[steps - 1:steps, :]
    m_new = b_last + jnp.maximum(m_prev, jnp.max(r, axis=0, keepdims=True))
    w_state = jnp.exp(r + (b_last - m_new))
    decay = jnp.exp(b_last + m_prev - m_new)
    r_rows = r.T

    for h in range(M_HEADS):
        r_row = r_rows[h:h + 1, :]
        dmat = jnp.where(causal, jnp.broadcast_to(r_row, (steps, steps)), -jnp.inf)
        m_h = m_prev[:, h:h + 1]
        gmax = jnp.maximum(jnp.max(dmat, axis=-1, keepdims=True), m_h)
        w_intra = jnp.exp(dmat - gmax)
        w_inter = jnp.exp(m_h - gmax)
        inv_floor = jnp.exp(-(b[:, h:h + 1] + gmax))

        q = q_ref[0, :, h * M_QK_DIM:(h + 1) * M_QK_DIM] * (M_QK_DIM ** -0.5)
        k = k_ref[0, :, h * M_QK_DIM:(h + 1) * M_QK_DIM]
        v = v_ref[0, :, h * M_V_DIM:(h + 1) * M_V_DIM]
        c_old = c_in[0, h]
        n_old = n_in[0, h:h + 1, :]

        qk = lax.dot_general(q, k, (((1,), (1,)), ((), ())), preferred_element_type=F32)
        s = qk * w_intra
        intra = jnp.dot(s.astype(BF16), v, preferred_element_type=F32)
        inter = lax.dot_general(q, c_old.astype(BF16), (((1,), (1,)), ((), ())),
                                preferred_element_type=F32)
        nq = jnp.sum(q.astype(F32) * n_old, axis=-1, keepdims=True)
        num = intra + w_inter * inter
        den = jnp.sum(s, axis=-1, keepdims=True) + w_inter * nq
        hval = num / jnp.maximum(jnp.abs(den), inv_floor)
        og = jax.nn.sigmoid(o_ref[0, :, h * M_V_DIM:(h + 1) * M_V_DIM].astype(F32))
        y_ref[0, :, h * M_V_DIM:(h + 1) * M_V_DIM] = (og * hval).astype(y_ref.dtype)

        ws = w_state[:, h:h + 1]
        vw = (v.astype(F32) * ws).astype(BF16)
        dc = lax.dot_general(vw, k, (((0,), (0,)), ((), ())), preferred_element_type=F32)
        dec = decay[:, h:h + 1]
        c_ref[0, h] = dec * c_old + dc
        n_ref[0, h:h + 1, :] = dec * n_old + jnp.sum(k.astype(F32) * ws, axis=0, keepdims=True)

    m_ref[0] = m_new


def _mlstm(za, zb, gates, bias, c0, n0, m0):
    batch, steps, _ = za.shape
    rows = min(steps, MLSTM_ROWS)
    state_specs = [pl.BlockSpec((1, M_HEADS, M_V_DIM, M_QK_DIM), lambda b, c: (b, 0, 0, 0)),
                   pl.BlockSpec((1, M_HEADS, M_QK_DIM), lambda b, c: (b, 0, 0)),
                   pl.BlockSpec((1, 1, LANES), lambda b, c: (b, 0, 0))]
    return pl.pallas_call(
        functools.partial(_mlstm_body, single_step=steps == rows),
        grid=(batch, steps // rows),
        in_specs=[pl.BlockSpec((1, rows, M_QK_WIDTH), lambda b, c: (b, c, 0)),
                  pl.BlockSpec((1, rows, M_QK_WIDTH), lambda b, c: (b, c, 1)),
                  pl.BlockSpec((1, rows, M_V_WIDTH), lambda b, c: (b, c, 1)),
                  pl.BlockSpec((1, rows, M_V_WIDTH), lambda b, c: (b, c, 0)),
                  pl.BlockSpec((1, rows, GATE_PAD), lambda b, c: (b, c, 0)),
                  pl.BlockSpec((1, GATE_PAD), lambda b, c: (0, 0))] + state_specs,
        out_specs=[pl.BlockSpec((1, rows, M_V_WIDTH), lambda b, c: (b, c, 0))] + state_specs,
        out_shape=[jax.ShapeDtypeStruct((batch, steps, M_V_WIDTH), BF16),
                   jax.ShapeDtypeStruct(c0.shape, F32),
                   jax.ShapeDtypeStruct(n0.shape, F32),
                   jax.ShapeDtypeStruct(m0.shape, F32)],
        compiler_params=_params(("parallel", "arbitrary")),
        name="mlstm",
    )(za, za, za, zb, gates, bias, c0, n0, m0)


def _ln_rows(x, g, b):
    mu = jnp.mean(x, axis=-1, keepdims=True)
    xc = x - mu
    var = jnp.mean(xc * xc, axis=-1, keepdims=True)
    return xc * lax.rsqrt(var + LN_EPS) * g + b


def _ln_mod_body(r_ref, g_ref, b_ref, sc_ref, sh_ref, x_ref, u_ref):
    y = _ln_rows(r_ref[...], g_ref[...], b_ref[...])
    x_ref[...] = y
    u_ref[...] = (y * (1.0 + sc_ref[...]) + sh_ref[...]).astype(u_ref.dtype)


def _ln_body(r_ref, g_ref, b_ref, x_ref):
    x_ref[...] = _ln_rows(r_ref[...], g_ref[...], b_ref[...])


def _layer_norm(r, g, b, mod=None, sc_idx=None, sh_idx=None):
    batch, steps, d = r.shape
    bb, tt = _row_blocks(batch, steps, ROW_TILE // 2)
    act = pl.BlockSpec((bb, tt, d), lambda i, t: (i, t, 0))
    vec = pl.BlockSpec((1, 1, d), lambda i, t: (0, 0, 0))
    g3, b3 = g.reshape(1, 1, d), b.reshape(1, 1, d)
    if mod is None:
        return pl.pallas_call(
            _ln_body, grid=(batch // bb, steps // tt),
            in_specs=[act, vec, vec], out_specs=act,
            out_shape=jax.ShapeDtypeStruct(r.shape, F32),
            compiler_params=_params(("parallel", "parallel")), name="layer_norm",
        )(r, g3, b3)
    return pl.pallas_call(
        _ln_mod_body, grid=(batch // bb, steps // tt),
        in_specs=[act, vec, vec,
                  pl.BlockSpec((bb, 1, d), lambda i, t: (i, 0, sc_idx)),
                  pl.BlockSpec((bb, 1, d), lambda i, t: (i, 0, sh_idx))],
        out_specs=[act, act],
        out_shape=[jax.ShapeDtypeStruct(r.shape, F32), jax.ShapeDtypeStruct(r.shape, BF16)],
        compiler_params=_params(("parallel", "parallel")), name="layer_norm_modulate",
    )(r, g3, b3, mod, mod)


SH1, SC1, GT1, SH2, SC2, GT2 = range(6)


def _rope_tables(pos):
    half = ROT_DIM // 2
    inv = ROPE_THETA ** (-2.0 * jnp.arange(half, dtype=F32) / ROT_DIM)
    ang = pos.astype(F32)[:, None] * inv[None, :]
    cos, sin = jnp.cos(ang), jnp.sin(ang)
    zeros = jnp.zeros_like(cos)
    rest = HEAD_DIM - ROT_DIM
    steps = pos.shape[0]
    one_head = lambda first, second, fill: jnp.concatenate(
        [first, second, jnp.full((steps, rest), fill, F32)], axis=1)
    cos_t = one_head(cos, cos, 1.0)
    up_t = one_head(-sin, zeros, 0.0)
    dn_t = one_head(zeros, sin, 0.0)
    reps = LANES // HEAD_DIM
    return tuple(jnp.tile(t, (1, reps)) for t in (cos_t, up_t, dn_t))


CAST_TILE = 512
FFN_TILE = 256
MERGE_CAST_TILE = 256
GATE_COLS = 2 * M_HEADS
PROJ_HEAD = ATT_WIDTH + 2 * KV_WIDTH + 2 * M_QK_WIDTH + M_V_WIDTH


def _group_layer(x, mod, pos, wts, raw, k_prev, v_prev, state, *, attn_bb, attn_tq, mask_first):
    batch, steps, d = x.shape
    ropes = _rope_tables(pos)
    casting = raw is not None
    tile = CAST_TILE if casting else COL_TILE

    def dense(name, body, act, keys, casts, extras, extra_specs, outs, n_cols, wtile, side_keys=(), rows=ROW_TILE):
        n_tiles = -(-n_cols // wtile)
        if not casting:
            weights = [(wts[key], wtile, lambda j: j) for key in keys]
            return _matmul(body, act, weights, extras, extra_specs, outs, n_tiles=n_tiles, rows=rows, name=name)
        res = _matmul_casting(body, act, casts, extras, extra_specs, outs, n_tiles=n_tiles, name=name,
                              n_side=len(side_keys))
        for key, w in zip(list(keys) + list(side_keys), res[len(outs):]):
            wts[key] = w
        return res[0] if len(outs) == 1 else res[:len(outs)]

    def proj_rows(first, total):
        srcs = [(raw["w_in_t"], (1, tile, d), lambda j: (0, first // tile + j, 0))]
        return [(srcs, lambda w: w[0].T, (d, total), (d, tile), lambda j: (0, j))]

    def proj_rows_shifted(first, total):
        base = first - GATE_COLS
        srcs = [(raw["w_in_t"], (1, tile, d), lambda j: (0, base // tile + j, 0)),
                (raw["w_in_t"], (1, GATE_COLS, d), lambda j: (0, (base + tile * (j + 1)) // GATE_COLS, 0))]

        def transform(main, nxt):
            rows = jnp.concatenate([main[0], nxt[0]], axis=0)
            return rows[GATE_COLS:GATE_COLS + tile].T

        return [(srcs, transform, (d, total), (d, tile), lambda j: (0, j))]

    def columns(key, first, total, width):
        srcs = [(raw[key], (1, d, width), lambda j: (0, 0, first // width + j))]
        return (srcs, lambda w: w[0], (d, total), (d, width), lambda j: (0, j))

    def mod_full(idx):
        return lambda bb, tt: pl.BlockSpec((bb, 1, d), lambda b, t_, j: (b, 0, idx))

    u, ka, va, gates = _matmul(
        _mm_mod_kvg_body, x, [(wts["w_kv"], 2 * KV_WIDTH, lambda j: 0), (wts["w_g"], GATE_PAD, lambda j: 0)],
        [mod, mod, *ropes], [mod_full(SC1), mod_full(SH1)] + _rope_specs(),
        [(d, BF16, d), (KV_WIDTH, F32, KV_WIDTH), (KV_WIDTH, F32, KV_WIDTH), (GATE_PAD, F32, GATE_PAD)],
        n_tiles=1, rows=ROW_TILE // 2, name="proj_kvg")

    mem_first = ATT_WIDTH + 2 * KV_WIDTH
    za_width = 2 * M_QK_WIDTH + M_V_WIDTH
    zb_width = M_V_WIDTH + 2 * d
    qa = dense("proj_q", _mm_rope_body, u, ["w_qa"], casting and proj_rows(0, ATT_WIDTH),
               ropes, _rope_specs(), [(ATT_WIDTH, BF16, tile)], ATT_WIDTH, tile)
    za = dense("proj_mem", _mm_plain_body, u, ["w_mem"], casting and proj_rows(mem_first, za_width),
               [], [], [(za_width, BF16, tile)], za_width, tile)
    zb = dense("proj_gates", _mm_plain_body, u, ["w_gates"],
               casting and proj_rows_shifted(PROJ_HEAD + GATE_COLS, zb_width),
               [], [], [(zb_width, BF16, tile)], zb_width, tile)

    if k_prev is None:
        k_prev, v_prev = ka, va
    ya = _attention(qa, ka, va, k_prev, v_prev, wts["sink_tab"], bb=attn_bb, tq=attn_tq, mask_first=mask_first)

    yb, c_new, n_new, m_new = _mlstm(za, zb, gates, wts["gate_bias"], *state)

    mtile = MERGE_CAST_TILE if casting else COL_TILE
    ga_col = M_V_WIDTH // mtile
    gb_col = ga_col + d // mtile
    t = dense("merge_a", _mm_gate_first_body, ya, ["w_up_a"], casting and [columns("w_up_a", 0, d, mtile)],
              [zb], [lambda bb, tt: _act_spec(bb, tt, mtile, ga_col)], [(d, F32, mtile)], d, mtile)
    merged = dense("merge_b", _mm_gate_second_body, yb, ["w_up_b"], casting and [columns("w_up_b", 0, d, mtile)],
                   [zb, t], [lambda bb, tt: _act_spec(bb, tt, mtile, gb_col),
                             lambda bb, tt: _act_spec(bb, tt, mtile)], [(d, BF16, mtile)], d, mtile)

    def mod_spec(idx, width):
        per_vec = d // width
        return lambda bb, tt: pl.BlockSpec((bb, 1, width), lambda b, t_, j: (b, 0, idx * per_vec + j))

    r1 = dense("out_proj", _mm_residual_body, merged, ["w_o"], casting and [columns("w_o", 0, d, mtile)],
               [x, mod], [lambda bb, tt: _act_spec(bb, tt, mtile), mod_spec(GT1, mtile)],
               [(d, F32, mtile)], d, mtile)
    x1, u2 = _layer_norm(r1, wts["ln1_g"], wts["ln1_b"], mod, SC2, SH2)

    ff_tile = FFN_TILE
    ffn_casts = casting and [
        columns("w_ffn_in", 0, D_FF, ff_tile), columns("w_ffn_in", D_FF, D_FF, ff_tile),
        ([(raw["w_ffn_out"], (1, ff_tile, d), lambda j: (0, j, 0))], lambda w: w[0],
         (D_FF, d), (ff_tile, d), lambda j: (j, 0))]
    hidden = dense("ffn_in", _mm_swiglu_body, u2, ["w_ffn_gate", "w_ffn_up"], ffn_casts,
                   [], [], [(D_FF, BF16, ff_tile)], D_FF, ff_tile, side_keys=["w_ffn_out"] if casting else (),
                   rows=2 * ROW_TILE)

    out_tile = COL_TILE // 2
    r2 = _matmul(_mm_residual_body, hidden, [(wts["w_ffn_out"], out_tile, lambda j: j)], [x1, mod],
                 [lambda bb, tt: _act_spec(bb, tt, out_tile), mod_spec(GT2, out_tile)],
                 [(d, F32, out_tile)], n_tiles=d // out_tile, rows=ROW_TILE // 2, name="ffn_out")
    y = _layer_norm(r2, wts["ln2_g"], wts["ln2_b"])
    return y, ka, va, c_new, n_new, m_new


def _cast_body(w_ref, o_ref):
    o_ref[...] = w_ref[0].T.astype(o_ref.dtype)


def _cast_gates_body(w_ref, o_ref):
    x = w_ref[0].T
    valid = lax.broadcasted_iota(jnp.int32, (1, LANES), 1) < M_HEADS
    o_ref[:, :LANES] = jnp.where(valid, x, 0.0).astype(o_ref.dtype)
    f_first = pltpu.roll(x, LANES - M_HEADS, axis=1)
    o_ref[:, LANES:] = jnp.where(valid, f_first, 0.0).astype(o_ref.dtype)


def _cast_kv_gates(w_in_t):
    _, _, k = w_in_t.shape
    params = _params(("parallel",))
    first = ATT_WIDTH // CAST_TILE
    w_kv = pl.pallas_call(
        _cast_body, grid=(2 * KV_WIDTH // CAST_TILE,),
        in_specs=[pl.BlockSpec((1, CAST_TILE, k), lambda j: (0, first + j, 0))],
        out_specs=pl.BlockSpec((k, CAST_TILE), lambda j: (0, j)),
        out_shape=jax.ShapeDtypeStruct((k, 2 * KV_WIDTH), BF16),
        compiler_params=params, name="cast_proj_kv",
    )(w_in_t)
    w_g = pl.pallas_call(
        _cast_gates_body, grid=(1,),
        in_specs=[pl.BlockSpec((1, LANES, k), lambda j: (0, PROJ_HEAD // LANES, 0))],
        out_specs=pl.BlockSpec((k, GATE_PAD), lambda j: (0, 0)),
        out_shape=jax.ShapeDtypeStruct((k, GATE_PAD), BF16),
        compiler_params=params, name="cast_proj_gates",
    )(w_in_t)
    return w_kv, w_g


def kernel(x_prompt, x_sample, cache_k_win, cache_v_win, state_C, state_n, state_m, c_prompt, c_sample, w_ada, b_ada, w_in, b_if, attn_sinks, w_up_a, w_up_b, w_o, ln1_g, ln1_b, w_ffn_in, w_ffn_out, ln2_g, ln2_b):
    bp, sp, d = x_prompt.shape
    bs, ts, _ = x_sample.shape
    keep = cache_k_win.shape[2]
    assert w_in.shape[0] == DEPTH == 1
    l = 0

    raw = {"w_in_t": jnp.swapaxes(w_in, 1, 2), "w_up_a": w_up_a, "w_up_b": w_up_b, "w_o": w_o,
           "w_ffn_in": w_ffn_in, "w_ffn_out": w_ffn_out}
    w_kv, w_g = _cast_kv_gates(raw["w_in_t"])
    pad_bias = jnp.zeros((LANES - M_HEADS,), F32)
    wts = {
        "w_kv": w_kv, "w_g": w_g,
        "gate_bias": jnp.concatenate([b_if[l, :M_HEADS], pad_bias, b_if[l, M_HEADS:], pad_bias]).reshape(1, GATE_PAD),
        "sink_tab": jnp.broadcast_to(attn_sinks[l].astype(F32).reshape(N_KV_HEADS, 1, GROUP, 1),
                                     (N_KV_HEADS, 1, GROUP, CHUNK)).reshape(N_KV_HEADS, 1, GROUP * CHUNK),
        "ln1_g": ln1_g[l], "ln1_b": ln1_b[l], "ln2_g": ln2_g[l], "ln2_b": ln2_b[l],
    }

    c_all = jnp.concatenate([c_prompt, c_sample], axis=0)
    rows = c_all.shape[0]
    rows_pad = -(-rows // 8) * 8
    c_all = jnp.pad(c_all, ((0, rows_pad - rows), (0, 0)))
    mod = _adaln(c_all, w_ada[l], b_ada[l])
    mod_p = mod[:bp].reshape(bp, 1, 6 * d)
    mod_s = mod[bp:bp + bs].reshape(bs, 1, 6 * d)

    def pad_lanes(m):
        return jnp.pad(m, ((0, 0), (0, LANES - M_HEADS))).reshape(m.shape[0], 1, LANES)

    ck = cache_k_win[l].reshape(bs, keep, KV_WIDTH)
    cv = cache_v_win[l].reshape(bs, keep, KV_WIDTH)
    state_s = (state_C[l], state_n[l], pad_lanes(state_m[l]))
    ys, kas, vas, cs, ns, ms = _group_layer(
        x_sample, mod_s, PAST_LEN + jnp.arange(ts), wts, raw, ck, cv, state_s,
        attn_bb=min(bs, ATTN_ROWS // ts), attn_tq=ts, mask_first=False)

    state_p = (jnp.zeros((bp, M_HEADS, M_V_DIM, M_QK_DIM), F32),
               jnp.zeros((bp, M_HEADS, M_QK_DIM), F32),
               pad_lanes(jnp.full((bp, M_HEADS), M_INIT, F32)))
    yp, kap, vap, cp, np_, mp = _group_layer(
        x_prompt, mod_p, jnp.arange(sp), wts, None, None, None, state_p,
        attn_bb=1, attn_tq=min(ATTN_ROWS, sp), mask_first=True)

    def heads(a):
        return a.reshape(a.shape[0], a.shape[1], N_KV_HEADS, HEAD_DIM)

    kws = jnp.concatenate([ck, kas], axis=1)[:, -keep:]
    vws = jnp.concatenate([cv, vas], axis=1)[:, -keep:]
    return (yp, ys,
            heads(kap[:, -keep:])[None], heads(vap[:, -keep:])[None],
            cp[None], np_[None], mp[:, 0, :M_HEADS][None],
            heads(kws)[None], heads(vws)[None],
            cs[None], ns[None], ms[:, 0, :M_HEADS][None])
```

```python
import functools

import jax
import jax.numpy as jnp
from jax import lax
from jax.experimental import pallas as pl
from jax.experimental.pallas import tpu as pltpu

F32 = jnp.float32
BF16 = jnp.bfloat16

D_MODEL = 4096
CHUNK = 64
N_Q_HEADS = 64
N_KV_HEADS = 8
HEAD_DIM = 64
GROUP = N_Q_HEADS // N_KV_HEADS
WINDOW = 128
ROT_DIM = HEAD_DIM // 4
ROPE_THETA = 500000.0
M_HEADS = 8
M_QK_DIM = 256
M_V_DIM = 512
ATT_WIDTH = N_Q_HEADS * HEAD_DIM
KV_WIDTH = N_KV_HEADS * HEAD_DIM
M_QK_WIDTH = M_HEADS * M_QK_DIM
M_V_WIDTH = M_HEADS * M_V_DIM
D_FF = 11008
DEPTH = 1
PAST_LEN = 2048
ALPHA = (2.0 * DEPTH) ** 0.25
LN_EPS = 1e-5
M_INIT = -1e30

LANES = 128
ROW_TILE = 1024
COL_TILE = 1024
ROW_BANDS = 4
ADA_TILE = 512
ATTN_ROWS = 2048
MLSTM_ROWS = 256
GATE_PAD = 2 * LANES
VMEM_LIMIT = 60 * 1024 * 1024


def _params(semantics):
    return pltpu.CompilerParams(dimension_semantics=semantics, vmem_limit_bytes=VMEM_LIMIT)


def _row_blocks(batch, steps, rows):
    if steps >= rows:
        assert steps % rows == 0
        return 1, rows
    bb = min(batch, rows // steps)
    assert batch % bb == 0
    return bb, steps


def _adaln_body(c_ref, w_ref, b_ref, o_ref):
    c = c_ref[...]
    s = c * jax.nn.sigmoid(c)
    o_ref[...] = jnp.dot(s, w_ref[...], preferred_element_type=F32) + b_ref[...]


def _adaln(c, w_ada, b_ada):
    rows, d = c.shape
    n = w_ada.shape[1]
    tn = ADA_TILE
    return pl.pallas_call(
        _adaln_body,
        grid=(n // tn,),
        in_specs=[pl.BlockSpec((rows, d), lambda j: (0, 0)),
                  pl.BlockSpec((d, tn), lambda j: (0, j)),
                  pl.BlockSpec((1, tn), lambda j: (0, j))],
        out_specs=pl.BlockSpec((rows, tn), lambda j: (0, j)),
        out_shape=jax.ShapeDtypeStruct((rows, n), F32),
        compiler_params=_params(("parallel",)),
        name="adaln",
    )(c, w_ada, b_ada.reshape(1, n))


def _load_rows(a_ref):
    bb, tt, k = a_ref.shape
    return a_ref[...].reshape(bb * tt, k)


def _bands(ref):
    bb, tt = ref.shape[0], ref.shape[1]
    if bb > 1:
        n = min(ROW_BANDS, bb)
        assert bb % n == 0
        return [(slice(p * (bb // n), (p + 1) * (bb // n)), slice(0, tt)) for p in range(n)]
    assert tt % ROW_BANDS == 0
    return [(slice(0, 1), slice(p * (tt // ROW_BANDS), (p + 1) * (tt // ROW_BANDS))) for p in range(ROW_BANDS)]


def _band(ref, bs, ts):
    if ref.shape[1] == 1:
        return ref[bs, :, :]
    blk = ref[bs, ts, :]
    return blk.reshape(blk.shape[0] * blk.shape[1], blk.shape[2])


def _store_band(ref, bs, ts, rows):
    nb = bs.stop - bs.start
    ref[bs, ts, :] = rows.reshape(nb, ts.stop - ts.start, rows.shape[-1]).astype(ref.dtype)


def _rope_rows(x, cos_ref, sa_ref, sb_ref, bs, ts):
    rows, n = x.shape
    nb = bs.stop - bs.start

    def table(ref):
        t = ref[ts, :]
        if nb == 1:
            return t
        return jnp.broadcast_to(t[None], (nb, t.shape[0], LANES)).reshape(rows, LANES)

    cos, sa, sb = table(cos_ref), table(sa_ref), table(sb_ref)
    outs = []
    for s in range(n // LANES):
        xs = x[:, s * LANES:(s + 1) * LANES]
        up = pltpu.roll(xs, LANES - ROT_DIM // 2, axis=1)
        dn = pltpu.roll(xs, ROT_DIM // 2, axis=1)
        outs.append(xs * cos + up * sa + dn * sb)
    return jnp.concatenate(outs, axis=1)


def _mm_plain_body(a_ref, w_ref, o_ref):
    acc = jnp.dot(_load_rows(a_ref), w_ref[...], preferred_element_type=F32)
    o_ref[...] = acc.reshape(o_ref.shape).astype(o_ref.dtype)


def _mm_rope_body(a_ref, w_ref, cos_ref, sa_ref, sb_ref, o_ref):
    for bs, ts in _bands(a_ref):
        acc = jnp.dot(_band(a_ref, bs, ts), w_ref[...], preferred_element_type=F32)
        _store_band(o_ref, bs, ts, _rope_rows(acc, cos_ref, sa_ref, sb_ref, bs, ts))


def _mm_mod_kvg_body(x_ref, w_ref, wg_ref, sc_ref, sh_ref, cos_ref, sa_ref, sb_ref, u_ref, k_ref, v_ref, g_ref):
    u_ref[...] = (x_ref[...] * (1.0 + sc_ref[...]) + sh_ref[...]).astype(u_ref.dtype)
    for bs, ts in _bands(x_ref):
        a = _band(u_ref, bs, ts)
        acc = jnp.dot(a, w_ref[...], preferred_element_type=F32)
        _store_band(k_ref, bs, ts, _rope_rows(acc[:, :KV_WIDTH], cos_ref, sa_ref, sb_ref, bs, ts))
        _store_band(v_ref, bs, ts, acc[:, KV_WIDTH:])
        _store_band(g_ref, bs, ts, jnp.dot(a, wg_ref[...], preferred_element_type=F32))


def _gated(acc, g_ref, bs, ts):
    return jax.nn.sigmoid(_band(g_ref, bs, ts).astype(F32)) * acc


def _mm_gate_first_body(a_ref, w_ref, g_ref, o_ref):
    for bs, ts in _bands(a_ref):
        acc = jnp.dot(_band(a_ref, bs, ts), w_ref[...], preferred_element_type=F32)
        _store_band(o_ref, bs, ts, _gated(acc, g_ref, bs, ts))


def _mm_gate_second_body(a_ref, w_ref, g_ref, t_ref, o_ref):
    for bs, ts in _bands(a_ref):
        acc = jnp.dot(_band(a_ref, bs, ts), w_ref[...], preferred_element_type=F32)
        _store_band(o_ref, bs, ts, _band(t_ref, bs, ts) + _gated(acc, g_ref, bs, ts))


def _mm_residual_body(a_ref, w_ref, x_ref, g_ref, o_ref):
    acc = jnp.dot(_load_rows(a_ref), w_ref[...], preferred_element_type=F32)
    o_ref[...] = ALPHA * x_ref[...] + g_ref[...] * acc.reshape(o_ref.shape)


def _mm_swiglu_body(a_ref, wg_ref, wu_ref, o_ref):
    for bs, ts in _bands(a_ref):
        a = _band(a_ref, bs, ts)
        gate = jnp.dot(a, wg_ref[...], preferred_element_type=F32)
        up = jnp.dot(a, wu_ref[...], preferred_element_type=F32)
        _store_band(o_ref, bs, ts, gate * jax.nn.sigmoid(gate) * up)


def _act_spec(bb, tt, width, col=None):
    if col is None:
        return pl.BlockSpec((bb, tt, width), lambda b, t, j: (b, t, j))
    return pl.BlockSpec((bb, tt, width), lambda b, t, j: (b, t, j + col))


def _matmul(body, a, weights, extras, extra_specs, outs, *, n_tiles, rows=ROW_TILE, name):
    batch, steps, k = a.shape
    bb, tt = _row_blocks(batch, steps, rows)
    grid = (batch // bb, steps // tt, n_tiles)
    in_specs = [pl.BlockSpec((bb, tt, k), lambda b, t, j: (b, t, 0))]
    in_specs += [pl.BlockSpec((k, cols), functools.partial(lambda b, t, j, col: (0, col(j)), col=col))
                 for _, cols, col in weights]
    in_specs += [spec(bb, tt) for spec in extra_specs]
    weights = [w for w, _, _ in weights]
    out_specs = [pl.BlockSpec((bb, tt, cols_tile), lambda b, t, j: (b, t, j)) for _, _, cols_tile in outs]
    out_shape = [jax.ShapeDtypeStruct((batch, steps, cols), dt) for cols, dt, _ in outs]
    single = len(outs) == 1
    return pl.pallas_call(
        body,
        grid=grid,
        in_specs=in_specs,
        out_specs=out_specs[0] if single else out_specs,
        out_shape=out_shape[0] if single else out_shape,
        compiler_params=_params(("parallel", "parallel", "parallel")),
        name=name,
    )(a, *weights, *extras)


def _matmul_casting(body, a, casts, extras, extra_specs, outs, *, n_tiles, name, n_side=0):
    batch, steps, k = a.shape

    def per_tile(spec):
        return pl.BlockSpec(spec.block_shape, lambda j: spec.index_map(0, 0, j))

    in_specs = [pl.BlockSpec((batch, steps, k), lambda j: (0, 0, 0), pipeline_mode=pl.Buffered(1))]
    sources = []
    for srcs, *_ in casts:
        for arr, shape, index in srcs:
            in_specs.append(pl.BlockSpec(shape, index))
            sources.append(arr)
    in_specs += [per_tile(spec(batch, steps)) for spec in extra_specs]
    out_specs = [pl.BlockSpec((batch, steps, cols_tile), lambda j: (0, 0, j)) for _, _, cols_tile in outs]
    out_shape = [jax.ShapeDtypeStruct((batch, steps, cols), dt) for cols, dt, _ in outs]
    out_specs += [pl.BlockSpec(block, index) for _, _, _, block, index in casts]
    out_shape += [jax.ShapeDtypeStruct(shape, BF16) for _, _, shape, _, _ in casts]
    counts = [len(srcs) for srcs, *_ in casts]
    transforms = [tr for _, tr, *_ in casts]
    n_used = len(casts) - n_side
    n_extra, n_out = len(extras), len(outs)

    def casting_body(a_ref, *refs):
        src_refs, pos = [], 0
        for n in counts:
            src_refs.append(refs[pos:pos + n])
            pos += n
        extra_refs = refs[pos:pos + n_extra]
        out_refs = refs[pos + n_extra:pos + n_extra + n_out]
        w_refs = refs[pos + n_extra + n_out:]
        for w_ref, srcs, transform in zip(w_refs, src_refs, transforms):
            w_ref[...] = transform(*[s[...] for s in srcs]).astype(w_ref.dtype)
        body(a_ref, *w_refs[:n_used], *extra_refs, *out_refs)

    return pl.pallas_call(
        casting_body,
        grid=(n_tiles,),
        in_specs=in_specs,
        out_specs=out_specs,
        out_shape=out_shape,
        compiler_params=_params(("arbitrary",)),
        name=name,
    )(a, *sources, *extras)


def _rope_specs():
    return [lambda bb, tt: pl.BlockSpec((tt, LANES), lambda b, t, j: (t, 0))] * 3


def _attn_body(q_ref, kc_ref, kp_ref, vc_ref, vp_ref, sink_ref, o_ref, *, mask_first):
    bb, tq, _ = q_ref.shape
    n_chunks = tq // CHUNK
    n_keys = WINDOW + CHUNK
    i = pl.program_id(1)
    h = pl.program_id(2)

    lane = lax.broadcasted_iota(jnp.int32, (1, LANES), 1)
    keep = (lane // HEAD_DIM) == (h % 2)
    sink = sink_ref[0]
    lane_head = lax.broadcasted_iota(jnp.int32, (CHUNK, 2 * LANES), 1) // HEAD_DIM
    out_low = lax.broadcasted_iota(jnp.int32, (CHUNK, LANES), 1) < HEAD_DIM
    key_row = lax.broadcasted_iota(jnp.int32, (n_keys, 1), 0)
    ones = jnp.ones((16, n_keys), BF16)

    def both_halves(cur_ref, prev_ref, bi):
        pair = jnp.concatenate([prev_ref[bi].astype(F32), cur_ref[bi].astype(F32)], axis=0)
        swapped = pltpu.roll(pair, HEAD_DIM, axis=1)
        return jnp.where(keep, pair, swapped)

    def scores(bi, j, k4):
        q = q_ref[bi, j * CHUNK:(j + 1) * CHUNK, :] * (HEAD_DIM ** -0.5)
        parts = []
        for cg in range(GROUP // 4):
            qc = q[:, cg * 2 * LANES:(cg + 1) * 2 * LANES]
            for r in range(4):
                parts.append(jnp.where(lane_head == r, qc, jnp.zeros_like(qc)))
        qrows = jnp.concatenate(parts, axis=0)
        keys = k4[j * CHUNK:j * CHUNK + n_keys]
        st = lax.dot_general(keys, qrows, (((1,), (1,)), ((), ())), preferred_element_type=F32)
        if mask_first and j < WINDOW // CHUNK:
            first_valid = WINDOW - (i * tq + j * CHUNK)
            st = jnp.where(key_row >= first_valid, st, -jnp.inf)
        return st

    def values_t(j, v2):
        vwin = v2[j * CHUNK:j * CHUNK + n_keys]
        head = vwin[:LANES].T
        tail = vwin[LANES:]
        tail = jnp.concatenate([tail, tail], axis=0).T
        vt = jnp.concatenate([head[:HEAD_DIM], tail[:HEAD_DIM, :CHUNK]], axis=1).astype(BF16)
        return jnp.concatenate([vt, ones], axis=0)

    def finish(st, lhs):
        m = jnp.maximum(jnp.max(st, axis=0, keepdims=True), sink)
        pt = jnp.exp(st - m).astype(BF16)
        ot = jnp.dot(lhs, pt, preferred_element_type=F32)
        den = ot[HEAD_DIM:HEAD_DIM + 1] + jnp.exp(sink - m)
        return ot[:HEAD_DIM] / den

    def emit(bi, j, on):
        for pr in range(GROUP // 2):
            x = on[:, pr * LANES:(pr + 1) * LANES]
            xt = jnp.concatenate([x, x], axis=0).T
            o_ref[bi, j * CHUNK:(j + 1) * CHUNK, pr * LANES:(pr + 1) * LANES] = (
                jnp.where(out_low, xt[:CHUNK], xt[CHUNK:]).astype(o_ref.dtype))

    units, sts, lhss = [], [], []
    for bi in range(bb):
        k2 = both_halves(kc_ref, kp_ref, bi)
        v2 = both_halves(vc_ref, vp_ref, bi)
        k4 = jnp.concatenate([k2, k2], axis=1).astype(BF16)
        for j in range(n_chunks):
            units.append((bi, j))
            sts.append(scores(bi, j, k4))
            lhss.append(values_t(j, v2))
    for (bi, j), st, lhs in zip(units, sts, lhss):
        emit(bi, j, finish(st, lhs))


def _attention(q, k_cur, v_cur, k_prev, v_prev, sink_tab, *, bb, tq, mask_first):
    batch, steps, _ = q.shape
    qw = GROUP * HEAD_DIM
    same_array = k_prev is k_cur
    per_tile = tq // WINDOW

    def prev_map(b, i, h):
        if same_array:
            return (b, jnp.maximum(i * per_tile - 1, 0), h // 2)
        return (b, 0, h // 2)

    return pl.pallas_call(
        functools.partial(_attn_body, mask_first=mask_first),
        grid=(batch // bb, steps // tq, N_KV_HEADS),
        in_specs=[pl.BlockSpec((bb, tq, qw), lambda b, i, h: (b, i, h)),
                  pl.BlockSpec((bb, tq, LANES), lambda b, i, h: (b, i, h // 2)),
                  pl.BlockSpec((bb, WINDOW, LANES), prev_map),
                  pl.BlockSpec((bb, tq, LANES), lambda b, i, h: (b, i, h // 2)),
                  pl.BlockSpec((bb, WINDOW, LANES), prev_map),
                  pl.BlockSpec((1, 1, GROUP * CHUNK), lambda b, i, h: (h, 0, 0))],
        out_specs=pl.BlockSpec((bb, tq, qw), lambda b, i, h: (b, i, h)),
        out_shape=jax.ShapeDtypeStruct((batch, steps, ATT_WIDTH), BF16),
        compiler_params=_params(("parallel", "parallel", "parallel")),
        name="attention",
    )(q, k_cur, k_prev, v_cur, v_prev, sink_tab)


def _log_sigmoid(x):
    return jnp.minimum(x, 0.0) - jnp.log1p(jnp.exp(-jnp.abs(x)))


def _mlstm_body(q_ref, k_ref, v_ref, o_ref, g_ref, bias_ref, c0_ref, n0_ref, m0_ref,
                y_ref, c_ref, n_ref, m_ref, *, single_step):
    if single_step:
        c_in, n_in, m_in = c0_ref, n0_ref, m0_ref
    else:
        c_in, n_in, m_in = c_ref, n_ref, m_ref

        @pl.when(pl.program_id(1) == 0)
        def _():
            c_ref[...] = c0_ref[...]
            n_ref[...] = n0_ref[...]
            m_ref[...] = m0_ref[...]

    steps = q_ref.shape[1]
    g = g_ref[0] + bias_ref[...]
    li = g[:, :LANES]
    lf = _log_sigmoid(g[:, LANES:])
    row = lax.broadcasted_iota(jnp.int32, (steps, steps), 0)
    col = lax.broadcasted_iota(jnp.int32, (steps, steps), 1)
    causal = col <= row
    b = jnp.dot(causal.astype(F32), lf, preferred_element_type=F32,
                precision=lax.Precision.HIGHEST)
    r = li - b
    m_prev = m_in[0]
    b_last = b[steps - 1:steps, :]
    m_new = b_last + jnp.maximum(m_prev, jnp.max(r, axis=0, keepdims=True))
    w_state = jnp.exp(r + (b_last - m_new))
    decay = jnp.exp(b_last + m_prev - m_new)
    r_rows = r.T

    for h in range(M_HEADS):
        r_row = r_rows[h:h + 1, :]
        dmat = jnp.where(causal, jnp.broadcast_to(r_row, (steps, steps)), -jnp.inf)
        m_h = m_prev[:, h:h + 1]
        gmax = jnp.maximum(jnp.max(dmat, axis=-1, keepdims=True), m_h)
        w_intra = jnp.exp(dmat - gmax)
        w_inter = jnp.exp(m_h - gmax)
        inv_floor = jnp.exp(-(b[:, h:h + 1] + gmax))

        q = q_ref[0, :, h * M_QK_DIM:(h + 1) * M_QK_DIM] * (M_QK_DIM ** -0.5)
        k = k_ref[0, :, h * M_QK_DIM:(h + 1) * M_QK_DIM]
        v = v_ref[0, :, h * M_V_DIM:(h + 1) * M_V_DIM]
        c_old = c_in[0, h]
        n_old = n_in[0, h:h + 1, :]

        qk = lax.dot_general(q, k, (((1,), (1,)), ((), ())), preferred_element_type=F32)
        s = qk * w_intra
        intra = jnp.dot(s.astype(BF16), v, preferred_element_type=F32)
        inter = lax.dot_general(q, c_old.astype(BF16), (((1,), (1,)), ((), ())),
                                preferred_element_type=F32)
        nq = jnp.sum(q.astype(F32) * n_old, axis=-1, keepdims=True)
        num = intra + w_inter * inter
        den = jnp.sum(s, axis=-1, keepdims=True) + w_inter * nq
        hval = num / jnp.maximum(jnp.abs(den), inv_floor)
        og = jax.nn.sigmoid(o_ref[0, :, h * M_V_DIM:(h + 1) * M_V_DIM].astype(F32))
        y_ref[0, :, h * M_V_DIM:(h + 1) * M_V_DIM] = (og * hval).astype(y_ref.dtype)

        ws = w_state[:, h:h + 1]
        vw = (v.astype(F32) * ws).astype(BF16)
        dc = lax.dot_general(vw, k, (((0,), (0,)), ((), ())), preferred_element_type=F32)
        dec = decay[:, h:h + 1]
        c_ref[0, h] = dec * c_old + dc
        n_ref[0, h:h + 1, :] = dec * n_old + jnp.sum(k.astype(F32) * ws, axis=0, keepdims=True)

    m_ref[0] = m_new


def _mlstm(za, zb, gates, bias, c0, n0, m0):
    batch, steps, _ = za.shape
    rows = min(steps, MLSTM_ROWS)
    state_specs = [pl.BlockSpec((1, M_HEADS, M_V_DIM, M_QK_DIM), lambda b, c: (b, 0, 0, 0)),
                   pl.BlockSpec((1, M_HEADS, M_QK_DIM), lambda b, c: (b, 0, 0)),
                   pl.BlockSpec((1, 1, LANES), lambda b, c: (b, 0, 0))]
    return pl.pallas_call(
        functools.partial(_mlstm_body, single_step=steps == rows),
        grid=(batch, steps // rows),
        in_specs=[pl.BlockSpec((1, rows, M_QK_WIDTH), lambda b, c: (b, c, 0)),
                  pl.BlockSpec((1, rows, M_QK_WIDTH), lambda b, c: (b, c, 1)),
                  pl.BlockSpec((1, rows, M_V_WIDTH), lambda b, c: (b, c, 1)),
                  pl.BlockSpec((1, rows, M_V_WIDTH), lambda b, c: (b, c, 0)),
                  pl.BlockSpec((1, rows, GATE_PAD), lambda b, c: (b, c, 0)),
                  pl.BlockSpec((1, GATE_PAD), lambda b, c: (0, 0))] + state_specs,
        out_specs=[pl.BlockSpec((1, rows, M_V_WIDTH), lambda b, c: (b, c, 0))] + state_specs,
        out_shape=[jax.ShapeDtypeStruct((batch, steps, M_V_WIDTH), BF16),
                   jax.ShapeDtypeStruct(c0.shape, F32),
                   jax.ShapeDtypeStruct(n0.shape, F32),
                   jax.ShapeDtypeStruct(m0.shape, F32)],
        compiler_params=_params(("parallel", "arbitrary")),
        name="mlstm",
    )(za, za, za, zb, gates, bias, c0, n0, m0)


def _ln_rows(x, g, b):
    mu = jnp.mean(x, axis=-1, keepdims=True)
    xc = x - mu
    var = jnp.mean(xc * xc, axis=-1, keepdims=True)
    return xc * lax.rsqrt(var + LN_EPS) * g + b


def _ln_mod_body(r_ref, g_ref, b_ref, sc_ref, sh_ref, x_ref, u_ref):
    y = _ln_rows(r_ref[...], g_ref[...], b_ref[...])
    x_ref[...] = y
    u_ref[...] = (y * (1.0 + sc_ref[...]) + sh_ref[...]).astype(u_ref.dtype)


def _ln_body(r_ref, g_ref, b_ref, x_ref):
    x_ref[...] = _ln_rows(r_ref[...], g_ref[...], b_ref[...])


def _layer_norm(r, g, b, mod=None, sc_idx=None, sh_idx=None):
    batch, steps, d = r.shape
    bb, tt = _row_blocks(batch, steps, ROW_TILE // 2)
    act = pl.BlockSpec((bb, tt, d), lambda i, t: (i, t, 0))
    vec = pl.BlockSpec((1, 1, d), lambda i, t: (0, 0, 0))
    g3, b3 = g.reshape(1, 1, d), b.reshape(1, 1, d)
    if mod is None:
        return pl.pallas_call(
            _ln_body, grid=(batch // bb, steps // tt),
            in_specs=[act, vec, vec], out_specs=act,
            out_shape=jax.ShapeDtypeStruct(r.shape, F32),
            compiler_params=_params(("parallel", "parallel")), name="layer_norm",
        )(r, g3, b3)
    return pl.pallas_call(
        _ln_mod_body, grid=(batch // bb, steps // tt),
        in_specs=[act, vec, vec,
                  pl.BlockSpec((bb, 1, d), lambda i, t: (i, 0, sc_idx)),
                  pl.BlockSpec((bb, 1, d), lambda i, t: (i, 0, sh_idx))],
        out_specs=[act, act],
        out_shape=[jax.ShapeDtypeStruct(r.shape, F32), jax.ShapeDtypeStruct(r.shape, BF16)],
        compiler_params=_params(("parallel", "parallel")), name="layer_norm_modulate",
    )(r, g3, b3, mod, mod)


SH1, SC1, GT1, SH2, SC2, GT2 = range(6)


def _rope_tables(pos):
    half = ROT_DIM // 2
    inv = ROPE_THETA ** (-2.0 * jnp.arange(half, dtype=F32) / ROT_DIM)
    ang = pos.astype(F32)[:, None] * inv[None, :]
    cos, sin = jnp.cos(ang), jnp.sin(ang)
    zeros = jnp.zeros_like(cos)
    rest = HEAD_DIM - ROT_DIM
    steps = pos.shape[0]
    one_head = lambda first, second, fill: jnp.concatenate(
        [first, second, jnp.full((steps, rest), fill, F32)], axis=1)
    cos_t = one_head(cos, cos, 1.0)
    up_t = one_head(-sin, zeros, 0.0)
    dn_t = one_head(zeros, sin, 0.0)
    reps = LANES // HEAD_DIM
    return tuple(jnp.tile(t, (1, reps)) for t in (cos_t, up_t, dn_t))


CAST_TILE = 512
FFN_TILE = 256
MERGE_CAST_TILE = 256
GATE_COLS = 2 * M_HEADS
PROJ_HEAD = ATT_WIDTH + 2 * KV_WIDTH + 2 * M_QK_WIDTH + M_V_WIDTH


def _group_layer(x, mod, pos, wts, raw, k_prev, v_prev, state, *, attn_bb, attn_tq, mask_first):
    batch, steps, d = x.shape
    ropes = _rope_tables(pos)
    casting = raw is not None
    tile = CAST_TILE if casting else COL_TILE

    def dense(name, body, act, keys, casts, extras, extra_specs, outs, n_cols, wtile, side_keys=(), rows=ROW_TILE):
        n_tiles = -(-n_cols // wtile)
        if not casting:
            weights = [(wts[key], wtile, lambda j: j) for key in keys]
            return _matmul(body, act, weights, extras, extra_specs, outs, n_tiles=n_tiles, rows=rows, name=name)
        res = _matmul_casting(body, act, casts, extras, extra_specs, outs, n_tiles=n_tiles, name=name,
                              n_side=len(side_keys))
        for key, w in zip(list(keys) + list(side_keys), res[len(outs):]):
            wts[key] = w
        return res[0] if len(outs) == 1 else res[:len(outs)]

    def proj_rows(first, total):
        srcs = [(raw["w_in_t"], (1, tile, d), lambda j: (0, first // tile + j, 0))]
        return [(srcs, lambda w: w[0].T, (d, total), (d, tile), lambda j: (0, j))]

    def proj_rows_shifted(first, total):
        base = first - GATE_COLS
        srcs = [(raw["w_in_t"], (1, tile, d), lambda j: (0, base // tile + j, 0)),
                (raw["w_in_t"], (1, GATE_COLS, d), lambda j: (0, (base + tile * (j + 1)) // GATE_COLS, 0))]

        def transform(main, nxt):
            rows = jnp.concatenate([main[0], nxt[0]], axis=0)
            return rows[GATE_COLS:GATE_COLS + tile].T

        return [(srcs, transform, (d, total), (d, tile), lambda j: (0, j))]

    def columns(key, first, total, width):
        srcs = [(raw[key], (1, d, width), lambda j: (0, 0, first // width + j))]
        return (srcs, lambda w: w[0], (d, total), (d, width), lambda j: (0, j))

    def mod_full(idx):
        return lambda bb, tt: pl.BlockSpec((bb, 1, d), lambda b, t_, j: (b, 0, idx))

    u, ka, va, gates = _matmul(
        _mm_mod_kvg_body, x, [(wts["w_kv"], 2 * KV_WIDTH, lambda j: 0), (wts["w_g"], GATE_PAD, lambda j: 0)],
        [mod, mod, *ropes], [mod_full(SC1), mod_full(SH1)] + _rope_specs(),
        [(d, BF16, d), (KV_WIDTH, F32, KV_WIDTH), (KV_WIDTH, F32, KV_WIDTH), (GATE_PAD, F32, GATE_PAD)],
        n_tiles=1, rows=ROW_TILE // 2, name="proj_kvg")

    mem_first = ATT_WIDTH + 2 * KV_WIDTH
    za_width = 2 * M_QK_WIDTH + M_V_WIDTH
    zb_width = M_V_WIDTH + 2 * d
    qa = dense("proj_q", _mm_rope_body, u, ["w_qa"], casting and proj_rows(0, ATT_WIDTH),
               ropes, _rope_specs(), [(ATT_WIDTH, BF16, tile)], ATT_WIDTH, tile)
    za = dense("proj_mem", _mm_plain_body, u, ["w_mem"], casting and proj_rows(mem_first, za_width),
               [], [], [(za_width, BF16, tile)], za_width, tile)
    zb = dense("proj_gates", _mm_plain_body, u, ["w_gates"],
               casting and proj_rows_shifted(PROJ_HEAD + GATE_COLS, zb_width),
               [], [], [(zb_width, BF16, tile)], zb_width, tile)

    if k_prev is None:
        k_prev, v_prev = ka, va
    ya = _attention(qa, ka, va, k_prev, v_prev, wts["sink_tab"], bb=attn_bb, tq=attn_tq, mask_first=mask_first)

    yb, c_new, n_new, m_new = _mlstm(za, zb, gates, wts["gate_bias"], *state)

    mtile = MERGE_CAST_TILE if casting else COL_TILE
    ga_col = M_V_WIDTH // mtile
    gb_col = ga_col + d // mtile
    t = dense("merge_a", _mm_gate_first_body, ya, ["w_up_a"], casting and [columns("w_up_a", 0, d, mtile)],
              [zb], [lambda bb, tt: _act_spec(bb, tt, mtile, ga_col)], [(d, F32, mtile)], d, mtile)
    merged = dense("merge_b", _mm_gate_second_body, yb, ["w_up_b"], casting and [columns("w_up_b", 0, d, mtile)],
                   [zb, t], [lambda bb, tt: _act_spec(bb, tt, mtile, gb_col),
                             lambda bb, tt: _act_spec(bb, tt, mtile)], [(d, BF16, mtile)], d, mtile)

    def mod_spec(idx, width):
        per_vec = d // width
        return lambda bb, tt: pl.BlockSpec((bb, 1, width), lambda b, t_, j: (b, 0, idx * per_vec + j))

    r1 = dense("out_proj", _mm_residual_body, merged, ["w_o"], casting and [columns("w_o", 0, d, mtile)],
               [x, mod], [lambda bb, tt: _act_spec(bb, tt, mtile), mod_spec(GT1, mtile)],
               [(d, F32, mtile)], d, mtile)
    x1, u2 = _layer_norm(r1, wts["ln1_g"], wts["ln1_b"], mod, SC2, SH2)

    ff_tile = FFN_TILE
    ffn_casts = casting and [
        columns("w_ffn_in", 0, D_FF, ff_tile), columns("w_ffn_in", D_FF, D_FF, ff_tile),
        ([(raw["w_ffn_out"], (1, ff_tile, d), lambda j: (0, j, 0))], lambda w: w[0],
         (D_FF, d), (ff_tile, d), lambda j: (j, 0))]
    hidden = dense("ffn_in", _mm_swiglu_body, u2, ["w_ffn_gate", "w_ffn_up"], ffn_casts,
                   [], [], [(D_FF, BF16, ff_tile)], D_FF, ff_tile, side_keys=["w_ffn_out"] if casting else (),
                   rows=2 * ROW_TILE)

    out_tile = COL_TILE // 2
    r2 = _matmul(_mm_residual_body, hidden, [(wts["w_ffn_out"], out_tile, lambda j: j)], [x1, mod],
                 [lambda bb, tt: _act_spec(bb, tt, out_tile), mod_spec(GT2, out_tile)],
                 [(d, F32, out_tile)], n_tiles=d // out_tile, rows=ROW_TILE // 2, name="ffn_out")
    y = _layer_norm(r2, wts["ln2_g"], wts["ln2_b"])
    return y, ka, va, c_new, n_new, m_new


def _cast_body(w_ref, o_ref):
    o_ref[...] = w_ref[0].T.astype(o_ref.dtype)


def _cast_gates_body(w_ref, o_ref):
    x = w_ref[0].T
    valid = lax.broadcasted_iota(jnp.int32, (1, LANES), 1) < M_HEADS
    o_ref[:, :LANES] = jnp.where(valid, x, 0.0).astype(o_ref.dtype)
    f_first = pltpu.roll(x, LANES - M_HEADS, axis=1)
    o_ref[:, LANES:] = jnp.where(valid, f_first, 0.0).astype(o_ref.dtype)


def _cast_kv_gates(w_in_t):
    _, _, k = w_in_t.shape
    params = _params(("parallel",))
    first = ATT_WIDTH // CAST_TILE
    w_kv = pl.pallas_call(
        _cast_body, grid=(2 * KV_WIDTH // CAST_TILE,),
        in_specs=[pl.BlockSpec((1, CAST_TILE, k), lambda j: (0, first + j, 0))],
        out_specs=pl.BlockSpec((k, CAST_TILE), lambda j: (0, j)),
        out_shape=jax.ShapeDtypeStruct((k, 2 * KV_WIDTH), BF16),
        compiler_params=params, name="cast_proj_kv",
    )(w_in_t)
    w_g = pl.pallas_call(
        _cast_gates_body, grid=(1,),
        in_specs=[pl.BlockSpec((1, LANES, k), lambda j: (0, PROJ_HEAD // LANES, 0))],
        out_specs=pl.BlockSpec((k, GATE_PAD), lambda j: (0, 0)),
        out_shape=jax.ShapeDtypeStruct((k, GATE_PAD), BF16),
        compiler_params=params, name="cast_proj_gates",
    )(w_in_t)
    return w_kv, w_g


def kernel(x_prompt, x_sample, cache_k_win, cache_v_win, state_C, state_n, state_m, c_prompt, c_sample, w_ada, b_ada, w_in, b_if, attn_sinks, w_up_a, w_up_b, w_o, ln1_g, ln1_b, w_ffn_in, w_ffn_out, ln2_g, ln2_b):
    bp, sp, d = x_prompt.shape
    bs, ts, _ = x_sample.shape
    keep = cache_k_win.shape[2]
    assert w_in.shape[0] == DEPTH == 1
    l = 0

    raw = {"w_in_t": jnp.swapaxes(w_in, 1, 2), "w_up_a": w_up_a, "w_up_b": w_up_b, "w_o": w_o,
           "w_ffn_in": w_ffn_in, "w_ffn_out": w_ffn_out}
    w_kv, w_g = _cast_kv_gates(raw["w_in_t"])
    pad_bias = jnp.zeros((LANES - M_HEADS,), F32)
    wts = {
        "w_kv": w_kv, "w_g": w_g,
        "gate_bias": jnp.concatenate([b_if[l, :M_HEADS], pad_bias, b_if[l, M_HEADS:], pad_bias]).reshape(1, GATE_PAD),
        "sink_tab": jnp.broadcast_to(attn_sinks[l].astype(F32).reshape(N_KV_HEADS, 1, GROUP, 1),
                                     (N_KV_HEADS, 1, GROUP, CHUNK)).reshape(N_KV_HEADS, 1, GROUP * CHUNK),
        "ln1_g": ln1_g[l], "ln1_b": ln1_b[l], "ln2_g": ln2_g[l], "ln2_b": ln2_b[l],
    }

    c_all = jnp.concatenate([c_prompt, c_sample], axis=0)
    rows = c_all.shape[0]
    rows_pad = -(-rows // 8) * 8
    c_all = jnp.pad(c_all, ((0, rows_pad - rows), (0, 0)))
    mod = _adaln(c_all, w_ada[l], b_ada[l])
    mod_p = mod[:bp].reshape(bp, 1, 6 * d)
    mod_s = mod[bp:bp + bs].reshape(bs, 1, 6 * d)

    def pad_lanes(m):
        return jnp.pad(m, ((0, 0), (0, LANES - M_HEADS))).reshape(m.shape[0], 1, LANES)

    ck = cache_k_win[l].reshape(bs, keep, KV_WIDTH)
    cv = cache_v_win[l].reshape(bs, keep, KV_WIDTH)
    state_s = (state_C[l], state_n[l], pad_lanes(state_m[l]))
    ys, kas, vas, cs, ns, ms = _group_layer(
        x_sample, mod_s, PAST_LEN + jnp.arange(ts), wts, raw, ck, cv, state_s,
        attn_bb=min(bs, ATTN_ROWS // ts), attn_tq=ts, mask_first=False)

    state_p = (jnp.zeros((bp, M_HEADS, M_V_DIM, M_QK_DIM), F32),
               jnp.zeros((bp, M_HEADS, M_QK_DIM), F32),
               pad_lanes(jnp.full((bp, M_HEADS), M_INIT, F32)))
    yp, kap, vap, cp, np_, mp = _group_layer(
        x_prompt, mod_p, jnp.arange(sp), wts, None, None, None, state_p,
        attn_bb=1, attn_tq=min(ATTN_ROWS, sp), mask_first=True)

    def heads(a):
        return a.reshape(a.shape[0], a.shape[1], N_KV_HEADS, HEAD_DIM)

    kws = jnp.concatenate([ck, kas], axis=1)[:, -keep:]
    vws = jnp.concatenate([cv, vas], axis=1)[:, -keep:]
    return (yp, ys,
            heads(kap[:, -keep:])[None], heads(vap[:, -keep:])[None],
            cp[None], np_[None], mp[:, 0, :M_HEADS][None],
            heads(kws)[None], heads(vws)[None],
            cs[None], ns[None], ms[:, 0, :M_HEADS][None])
```

```python
import functools

import jax
import jax.numpy as jnp
from jax import lax
from jax.experimental import pallas as pl
from jax.experimental.pallas import tpu as pltpu

F32 = jnp.float32
BF16 = jnp.bfloat16

D_MODEL = 4096
CHUNK = 64
N_Q_HEADS = 64
N_KV_HEADS = 8
HEAD_DIM = 64
GROUP = N_Q_HEADS // N_KV_HEADS
WINDOW = 128
ROT_DIM = HEAD_DIM // 4
ROPE_THETA = 500000.0
M_HEADS = 8
M_QK_DIM = 256
M_V_DIM = 512
ATT_WIDTH = N_Q_HEADS * HEAD_DIM
KV_WIDTH = N_KV_HEADS * HEAD_DIM
M_QK_WIDTH = M_HEADS * M_QK_DIM
M_V_WIDTH = M_HEADS * M_V_DIM
D_FF = 11008
DEPTH = 1
PAST_LEN = 2048
ALPHA = (2.0 * DEPTH) ** 0.25
LN_EPS = 1e-5
M_INIT = -1e30

LANES = 128
ROW_TILE = 1024
COL_TILE = 1024
ROW_BANDS = 4
ADA_TILE = 512
ATTN_ROWS = 2048
MLSTM_ROWS = 256
GATE_PAD = 2 * LANES
VMEM_LIMIT = 60 * 1024 * 1024


def _params(semantics):
    return pltpu.CompilerParams(dimension_semantics=semantics, vmem_limit_bytes=VMEM_LIMIT)


def _row_blocks(batch, steps, rows):
    if steps >= rows:
        assert steps % rows == 0
        return 1, rows
    bb = min(batch, rows // steps)
    assert batch % bb == 0
    return bb, steps


def _adaln_body(c_ref, w_ref, b_ref, o_ref):
    c = c_ref[...]
    s = c * jax.nn.sigmoid(c)
    o_ref[...] = jnp.dot(s, w_ref[...], preferred_element_type=F32) + b_ref[...]


def _adaln(c, w_ada, b_ada):
    rows, d = c.shape
    n = w_ada.shape[1]
    tn = ADA_TILE
    return pl.pallas_call(
        _adaln_body,
        grid=(n // tn,),
        in_specs=[pl.BlockSpec((rows, d), lambda j: (0, 0)),
                  pl.BlockSpec((d, tn), lambda j: (0, j)),
                  pl.BlockSpec((1, tn), lambda j: (0, j))],
        out_specs=pl.BlockSpec((rows, tn), lambda j: (0, j)),
        out_shape=jax.ShapeDtypeStruct((rows, n), F32),
        compiler_params=_params(("parallel",)),
        name="adaln",
    )(c, w_ada, b_ada.reshape(1, n))


def _load_rows(a_ref):
    bb, tt, k = a_ref.shape
    return a_ref[...].reshape(bb * tt, k)


def _bands(ref):
    bb, tt = ref.shape[0], ref.shape[1]
    if bb > 1:
        n = min(ROW_BANDS, bb)
        assert bb % n == 0
        return [(slice(p * (bb // n), (p + 1) * (bb // n)), slice(0, tt)) for p in range(n)]
    assert tt % ROW_BANDS == 0
    return [(slice(0, 1), slice(p * (tt // ROW_BANDS), (p + 1) * (tt // ROW_BANDS))) for p in range(ROW_BANDS)]


def _band(ref, bs, ts):
    if ref.shape[1] == 1:
        return ref[bs, :, :]
    blk = ref[bs, ts, :]
    return blk.reshape(blk.shape[0] * blk.shape[1], blk.shape[2])


def _store_band(ref, bs, ts, rows):
    nb = bs.stop - bs.start
    ref[bs, ts, :] = rows.reshape(nb, ts.stop - ts.start, rows.shape[-1]).astype(ref.dtype)


def _rope_rows(x, cos_ref, sa_ref, sb_ref, bs, ts):
    rows, n = x.shape
    nb = bs.stop - bs.start

    def table(ref):
        t = ref[ts, :]
        if nb == 1:
            return t
        return jnp.broadcast_to(t[None], (nb, t.shape[0], LANES)).reshape(rows, LANES)

    cos, sa, sb = table(cos_ref), table(sa_ref), table(sb_ref)
    outs = []
    for s in range(n // LANES):
        xs = x[:, s * LANES:(s + 1) * LANES]
        up = pltpu.roll(xs, LANES - ROT_DIM // 2, axis=1)
        dn = pltpu.roll(xs, ROT_DIM // 2, axis=1)
        outs.append(xs * cos + up * sa + dn * sb)
    return jnp.concatenate(outs, axis=1)


def _mm_plain_body(a_ref, w_ref, o_ref):
    acc = jnp.dot(_load_rows(a_ref), w_ref[...], preferred_element_type=F32)
    o_ref[...] = acc.reshape(o_ref.shape).astype(o_ref.dtype)


def _mm_rope_body(a_ref, w_ref, cos_ref, sa_ref, sb_ref, o_ref):
    for bs, ts in _bands(a_ref):
        acc = jnp.dot(_band(a_ref, bs, ts), w_ref[...], preferred_element_type=F32)
        _store_band(o_ref, bs, ts, _rope_rows(acc, cos_ref, sa_ref, sb_ref, bs, ts))


def _mm_mod_kvg_body(x_ref, w_ref, wg_ref, sc_ref, sh_ref, cos_ref, sa_ref, sb_ref, u_ref, k_ref, v_ref, g_ref):
    u_ref[...] = (x_ref[...] * (1.0 + sc_ref[...]) + sh_ref[...]).astype(u_ref.dtype)
    for bs, ts in _bands(x_ref):
        a = _band(u_ref, bs, ts)
        acc = jnp.dot(a, w_ref[...], preferred_element_type=F32)
        _store_band(k_ref, bs, ts, _rope_rows(acc[:, :KV_WIDTH], cos_ref, sa_ref, sb_ref, bs, ts))
        _store_band(v_ref, bs, ts, acc[:, KV_WIDTH:])
        _store_band(g_ref, bs, ts, jnp.dot(a, wg_ref[...], preferred_element_type=F32))


def _gated(acc, g_ref, bs, ts):
    return jax.nn.sigmoid(_band(g_ref, bs, ts).astype(F32)) * acc


def _mm_gate_first_body(a_ref, w_ref, g_ref, o_ref):
    for bs, ts in _bands(a_ref):
        acc = jnp.dot(_band(a_ref, bs, ts), w_ref[...], preferred_element_type=F32)
        _store_band(o_ref, bs, ts, _gated(acc, g_ref, bs, ts))


def _mm_gate_second_body(a_ref, w_ref, g_ref, t_ref, o_ref):
    for bs, ts in _bands(a_ref):
        acc = jnp.dot(_band(a_ref, bs, ts), w_ref[...], preferred_element_type=F32)
        _store_band(o_ref, bs, ts, _band(t_ref, bs, ts) + _gated(acc, g_ref, bs, ts))


def _mm_residual_body(a_ref, w_ref, x_ref, g_ref, o_ref):
    acc = jnp.dot(_load_rows(a_ref), w_ref[...], preferred_element_type=F32)
    o_ref[...] = ALPHA * x_ref[...] + g_ref[...] * acc.reshape(o_ref.shape)


def _mm_residual_add_body(a_ref, w_ref, p_ref, x_ref, g_ref, o_ref):
    acc = p_ref[...] + jnp.dot(_load_rows(a_ref), w_ref[...], preferred_element_type=F32).reshape(o_ref.shape)
    o_ref[...] = ALPHA * x_ref[...] + g_ref[...] * acc


def _mm_swiglu_body(a_ref, wg_ref, wu_ref, o_ref):
    for bs, ts in _bands(a_ref):
        a = _band(a_ref, bs, ts)
        gate = jnp.dot(a, wg_ref[...], preferred_element_type=F32)
        up = jnp.dot(a, wu_ref[...], preferred_element_type=F32)
        _store_band(o_ref, bs, ts, gate * jax.nn.sigmoid(gate) * up)


def _act_spec(bb, tt, width, col=None):
    if col is None:
        return pl.BlockSpec((bb, tt, width), lambda b, t, j: (b, t, j))
    return pl.BlockSpec((bb, tt, width), lambda b, t, j: (b, t, j + col))


def _matmul(body, a, weights, extras, extra_specs, outs, *, n_tiles, rows=ROW_TILE, name, k_block=None):
    batch, steps, k = a.shape
    k, kidx = (k, 0) if k_block is None else k_block
    bb, tt = _row_blocks(batch, steps, rows)
    grid = (batch // bb, steps // tt, n_tiles)
    in_specs = [pl.BlockSpec((bb, tt, k), lambda b, t, j: (b, t, kidx))]
    in_specs += [pl.BlockSpec((k, cols), functools.partial(lambda b, t, j, col: (kidx, col(j)), col=col))
                 for _, cols, col in weights]
    in_specs += [spec(bb, tt) for spec in extra_specs]
    weights = [w for w, _, _ in weights]
    out_specs = [pl.BlockSpec((bb, tt, cols_tile), lambda b, t, j: (b, t, j)) for _, _, cols_tile in outs]
    out_shape = [jax.ShapeDtypeStruct((batch, steps, cols), dt) for cols, dt, _ in outs]
    single = len(outs) == 1
    return pl.pallas_call(
        body,
        grid=grid,
        in_specs=in_specs,
        out_specs=out_specs[0] if single else out_specs,
        out_shape=out_shape[0] if single else out_shape,
        compiler_params=_params(("parallel", "parallel", "parallel")),
        name=name,
    )(a, *weights, *extras)


def _matmul_casting(body, a, casts, extras, extra_specs, outs, *, n_tiles, name, n_side=0):
    batch, steps, k = a.shape

    def per_tile(spec):
        return pl.BlockSpec(spec.block_shape, lambda j: spec.index_map(0, 0, j))

    in_specs = [pl.BlockSpec((batch, steps, k), lambda j: (0, 0, 0), pipeline_mode=pl.Buffered(1))]
    sources = []
    for srcs, *_ in casts:
        for arr, shape, index in srcs:
            in_specs.append(pl.BlockSpec(shape, index))
            sources.append(arr)
    in_specs += [per_tile(spec(batch, steps)) for spec in extra_specs]
    out_specs = [pl.BlockSpec((batch, steps, cols_tile), lambda j: (0, 0, j)) for _, _, cols_tile in outs]
    out_shape = [jax.ShapeDtypeStruct((batch, steps, cols), dt) for cols, dt, _ in outs]
    out_specs += [pl.BlockSpec(block, index) for _, _, _, block, index in casts]
    out_shape += [jax.ShapeDtypeStruct(shape, BF16) for _, _, shape, _, _ in casts]
    counts = [len(srcs) for srcs, *_ in casts]
    transforms = [tr for _, tr, *_ in casts]
    n_used = len(casts) - n_side
    n_extra, n_out = len(extras), len(outs)

    def casting_body(a_ref, *refs):
        src_refs, pos = [], 0
        for n in counts:
            src_refs.append(refs[pos:pos + n])
            pos += n
        extra_refs = refs[pos:pos + n_extra]
        out_refs = refs[pos + n_extra:pos + n_extra + n_out]
        w_refs = refs[pos + n_extra + n_out:]
        for w_ref, srcs, transform in zip(w_refs, src_refs, transforms):
            w_ref[...] = transform(*[s[...] for s in srcs]).astype(w_ref.dtype)
        body(a_ref, *w_refs[:n_used], *extra_refs, *out_refs)

    return pl.pallas_call(
        casting_body,
        grid=(n_tiles,),
        in_specs=in_specs,
        out_specs=out_specs,
        out_shape=out_shape,
        compiler_params=_params(("arbitrary",)),
        name=name,
    )(a, *sources, *extras)


def _rope_specs():
    return [lambda bb, tt: pl.BlockSpec((tt, LANES), lambda b, t, j: (t, 0))] * 3


def _attn_body(q_ref, kc_ref, kp_ref, vc_ref, vp_ref, sink_ref, o_ref, *, mask_first):
    bb, tq, _ = q_ref.shape
    n_chunks = tq // CHUNK
    n_keys = WINDOW + CHUNK
    i = pl.program_id(1)
    h = pl.program_id(2)

    lane = lax.broadcasted_iota(jnp.int32, (1, LANES), 1)
    keep = (lane // HEAD_DIM) == (h % 2)
    sink = sink_ref[0]
    lane_head = lax.broadcasted_iota(jnp.int32, (CHUNK, 2 * LANES), 1) // HEAD_DIM
    out_low = lax.broadcasted_iota(jnp.int32, (CHUNK, LANES), 1) < HEAD_DIM
    key_row = lax.broadcasted_iota(jnp.int32, (n_keys, 1), 0)
    ones = jnp.ones((16, n_keys), BF16)

    def both_halves(cur_ref, prev_ref, bi):
        pair = jnp.concatenate([prev_ref[bi].astype(F32), cur_ref[bi].astype(F32)], axis=0)
        swapped = pltpu.roll(pair, HEAD_DIM, axis=1)
        return jnp.where(keep, pair, swapped)

    def scores(bi, j, k4):
        q = q_ref[bi, j * CHUNK:(j + 1) * CHUNK, :] * (HEAD_DIM ** -0.5)
        parts = []
        for cg in range(GROUP // 4):
            qc = q[:, cg * 2 * LANES:(cg + 1) * 2 * LANES]
            for r in range(4):
                parts.append(jnp.where(lane_head == r, qc, jnp.zeros_like(qc)))
        qrows = jnp.concatenate(parts, axis=0)
        keys = k4[j * CHUNK:j * CHUNK + n_keys]
        st = lax.dot_general(keys, qrows, (((1,), (1,)), ((), ())), preferred_element_type=F32)
        if mask_first and j < WINDOW // CHUNK:
            first_valid = WINDOW - (i * tq + j * CHUNK)
            st = jnp.where(key_row >= first_valid, st, -jnp.inf)
        return st

    def values_t(j, v2):
        vwin = v2[j * CHUNK:j * CHUNK + n_keys]
        head = vwin[:LANES].T
        tail = vwin[LANES:]
        tail = jnp.concatenate([tail, tail], axis=0).T
        vt = jnp.concatenate([head[:HEAD_DIM], tail[:HEAD_DIM, :CHUNK]], axis=1).astype(BF16)
        return jnp.concatenate([vt, ones], axis=0)

    def finish(st, lhs):
        m = jnp.maximum(jnp.max(st, axis=0, keepdims=True), sink)
        pt = jnp.exp(st - m).astype(BF16)
        ot = jnp.dot(lhs, pt, preferred_element_type=F32)
        den = ot[HEAD_DIM:HEAD_DIM + 1] + jnp.exp(sink - m)
        return ot[:HEAD_DIM] / den

    def emit(bi, j, on):
        for pr in range(GROUP // 2):
            x = on[:, pr * LANES:(pr + 1) * LANES]
            xt = jnp.concatenate([x, x], axis=0).T
            o_ref[bi, j * CHUNK:(j + 1) * CHUNK, pr * LANES:(pr + 1) * LANES] = (
                jnp.where(out_low, xt[:CHUNK], xt[CHUNK:]).astype(o_ref.dtype))

    units, sts, lhss = [], [], []
    for bi in range(bb):
        k2 = both_halves(kc_ref, kp_ref, bi)
        v2 = both_halves(vc_ref, vp_ref, bi)
        k4 = jnp.concatenate([k2, k2], axis=1).astype(BF16)
        for j in range(n_chunks):
            units.append((bi, j))
            sts.append(scores(bi, j, k4))
            lhss.append(values_t(j, v2))
    for (bi, j), st, lhs in zip(units, sts, lhss):
        emit(bi, j, finish(st, lhs))


def _attention(q, k_cur, v_cur, k_prev, v_prev, sink_tab, *, bb, tq, mask_first):
    batch, steps, _ = q.shape
    qw = GROUP * HEAD_DIM
    same_array = k_prev is k_cur
    per_tile = tq // WINDOW

    def prev_map(b, i, h):
        if same_array:
            return (b, jnp.maximum(i * per_tile - 1, 0), h // 2)
        return (b, 0, h // 2)

    return pl.pallas_call(
        functools.partial(_attn_body, mask_first=mask_first),
        grid=(batch // bb, steps // tq, N_KV_HEADS),
        in_specs=[pl.BlockSpec((bb, tq, qw), lambda b, i, h: (b, i, h)),
                  pl.BlockSpec((bb, tq, LANES), lambda b, i, h: (b, i, h // 2)),
                  pl.BlockSpec((bb, WINDOW, LANES), prev_map),
                  pl.BlockSpec((bb, tq, LANES), lambda b, i, h: (b, i, h // 2)),
                  pl.BlockSpec((bb, WINDOW, LANES), prev_map),
                  pl.BlockSpec((1, 1, GROUP * CHUNK), lambda b, i, h: (h, 0, 0))],
        out_specs=pl.BlockSpec((bb, tq, qw), lambda b, i, h: (b, i, h)),
        out_shape=jax.ShapeDtypeStruct((batch, steps, ATT_WIDTH), BF16),
        compiler_params=_params(("parallel", "parallel", "parallel")),
        name="attention",
    )(q, k_cur, k_prev, v_cur, v_prev, sink_tab)


def _log_sigmoid(x):
    return jnp.minimum(x, 0.0) - jnp.log1p(jnp.exp(-jnp.abs(x)))


def _mlstm_body(q_ref, k_ref, v_ref, o_ref, g_ref, bias_ref, c0_ref, n0_ref, m0_ref,
                y_ref, c_ref, n_ref, m_ref, *, single_step):
    if single_step:
        c_in, n_in, m_in = c0_ref, n0_ref, m0_ref
    else:
        c_in, n_in, m_in = c_ref, n_ref, m_ref

        @pl.when(pl.program_id(1) == 0)
        def _():
            c_ref[...] = c0_ref[...]
            n_ref[...] = n0_ref[...]
            m_ref[...] = m0_ref[...]

    steps = q_ref.shape[1]
    g = g_ref[0] + bias_ref[...]
    li = g[:, :LANES]
    lf = _log_sigmoid(g[:, LANES:])
    row = lax.broadcasted_iota(jnp.int32, (steps, steps), 0)
    col = lax.broadcasted_iota(jnp.int32, (steps, steps), 1)
    causal = col <= row
    b = jnp.dot(causal.astype(F32), lf, preferred_element_type=F32,
                precision=lax.Precision.HIGHEST)
    r = li - b
    m_prev = m_in[0]
    b_last = b[steps - 1:steps, :]
    m_new = b_last + jnp.maximum(m_prev, jnp.max(r, axis=0, keepdims=True))
    w_state = jnp.exp(r + (b_last - m_new))
    decay = jnp.exp(b_last + m_prev - m_new)
    r_rows = r.T

    for h in range(M_HEADS):
        r_row = r_rows[h:h + 1, :]
        dmat = jnp.where(causal, jnp.broadcast_to(r_row, (steps, steps)), -jnp.inf)
        m_h = m_prev[:, h:h + 1]
        gmax = jnp.maximum(jnp.max(dmat, axis=-1, keepdims=True), m_h)
        w_intra = jnp.exp(dmat - gmax)
        w_inter = jnp.exp(m_h - gmax)
        inv_floor = jnp.exp(-(b[:, h:h + 1] + gmax))

        q = q_ref[0, :, h * M_QK_DIM:(h + 1) * M_QK_DIM] * (M_QK_DIM ** -0.5)
        k = k_ref[0, :, h * M_QK_DIM:(h + 1) * M_QK_DIM]
        v = v_ref[0, :, h * M_V_DIM:(h + 1) * M_V_DIM]
        c_old = c_in[0, h]
        n_old = n_in[0, h:h + 1, :]

        qk = lax.dot_general(q, k, (((1,), (1,)), ((), ())), preferred_element_type=F32)
        s = qk * w_intra
        intra = jnp.dot(s.astype(BF16), v, preferred_element_type=F32)
        inter = lax.dot_general(q, c_old.astype(BF16), (((1,), (1,)), ((), ())),
                                preferred_element_type=F32)
        nq = jnp.sum(q.astype(F32) * n_old, axis=-1, keepdims=True)
        num = intra + w_inter * inter
        den = jnp.sum(s, axis=-1, keepdims=True) + w_inter * nq
        hval = num / jnp.maximum(jnp.abs(den), inv_floor)
        og = jax.nn.sigmoid(o_ref[0, :, h * M_V_DIM:(h + 1) * M_V_DIM].astype(F32))
        y_ref[0, :, h * M_V_DIM:(h + 1) * M_V_DIM] = (og * hval).astype(y_ref.dtype)

        ws = w_state[:, h:h + 1]
        vw = (v.astype(F32) * ws).astype(BF16)
        dc = lax.dot_general(vw, k, (((0,), (0,)), ((), ())), preferred_element_type=F32)
        dec = decay[:, h:h + 1]
        c_ref[0, h] = dec * c_old + dc
        n_ref[0, h:h + 1, :] = dec * n_old + jnp.sum(k.astype(F32) * ws, axis=0, keepdims=True)

    m_ref[0] = m_new


def _mlstm(za, zb, gates, bias, c0, n0, m0):
    batch, steps, _ = za.shape
    rows = min(steps, MLSTM_ROWS)
    state_specs = [pl.BlockSpec((1, M_HEADS, M_V_DIM, M_QK_DIM), lambda b, c: (b, 0, 0, 0)),
                   pl.BlockSpec((1, M_HEADS, M_QK_DIM), lambda b, c: (b, 0, 0)),
                   pl.BlockSpec((1, 1, LANES), lambda b, c: (b, 0, 0))]
    return pl.pallas_call(
        functools.partial(_mlstm_body, single_step=steps == rows),
        grid=(batch, steps // rows),
        in_specs=[pl.BlockSpec((1, rows, M_QK_WIDTH), lambda b, c: (b, c, 0)),
                  pl.BlockSpec((1, rows, M_QK_WIDTH), lambda b, c: (b, c, 1)),
                  pl.BlockSpec((1, rows, M_V_WIDTH), lambda b, c: (b, c, 1)),
                  pl.BlockSpec((1, rows, M_V_WIDTH), lambda b, c: (b, c, 0)),
                  pl.BlockSpec((1, rows, GATE_PAD), lambda b, c: (b, c, 0)),
                  pl.BlockSpec((1, GATE_PAD), lambda b, c: (0, 0))] + state_specs,
        out_specs=[pl.BlockSpec((1, rows, M_V_WIDTH), lambda b, c: (b, c, 0))] + state_specs,
        out_shape=[jax.ShapeDtypeStruct((batch, steps, M_V_WIDTH), BF16),
                   jax.ShapeDtypeStruct(c0.shape, F32),
                   jax.ShapeDtypeStruct(n0.shape, F32),
                   jax.ShapeDtypeStruct(m0.shape, F32)],
        compiler_params=_params(("parallel", "arbitrary")),
        name="mlstm",
    )(za, za, za, zb, gates, bias, c0, n0, m0)


def _ln_rows(x, g, b):
    mu = jnp.mean(x, axis=-1, keepdims=True)
    xc = x - mu
    var = jnp.mean(xc * xc, axis=-1, keepdims=True)
    return xc * lax.rsqrt(var + LN_EPS) * g + b


def _ln_mod_body(r_ref, g_ref, b_ref, sc_ref, sh_ref, x_ref, u_ref):
    y = _ln_rows(r_ref[...], g_ref[...], b_ref[...])
    x_ref[...] = y
    u_ref[...] = (y * (1.0 + sc_ref[...]) + sh_ref[...]).astype(u_ref.dtype)


def _ln_body(r_ref, g_ref, b_ref, x_ref):
    x_ref[...] = _ln_rows(r_ref[...], g_ref[...], b_ref[...])


def _layer_norm(r, g, b, mod=None, sc_idx=None, sh_idx=None):
    batch, steps, d = r.shape
    bb, tt = _row_blocks(batch, steps, ROW_TILE // 2)
    act = pl.BlockSpec((bb, tt, d), lambda i, t: (i, t, 0))
    vec = pl.BlockSpec((1, 1, d), lambda i, t: (0, 0, 0))
    g3, b3 = g.reshape(1, 1, d), b.reshape(1, 1, d)
    if mod is None:
        return pl.pallas_call(
            _ln_body, grid=(batch // bb, steps // tt),
            in_specs=[act, vec, vec], out_specs=act,
            out_shape=jax.ShapeDtypeStruct(r.shape, F32),
            compiler_params=_params(("parallel", "parallel")), name="layer_norm",
        )(r, g3, b3)
    return pl.pallas_call(
        _ln_mod_body, grid=(batch // bb, steps // tt),
        in_specs=[act, vec, vec,
                  pl.BlockSpec((bb, 1, d), lambda i, t: (i, 0, sc_idx)),
                  pl.BlockSpec((bb, 1, d), lambda i, t: (i, 0, sh_idx))],
        out_specs=[act, act],
        out_shape=[jax.ShapeDtypeStruct(r.shape, F32), jax.ShapeDtypeStruct(r.shape, BF16)],
        compiler_params=_params(("parallel", "parallel")), name="layer_norm_modulate",
    )(r, g3, b3, mod, mod)


SH1, SC1, GT1, SH2, SC2, GT2 = range(6)


def _rope_tables(pos):
    half = ROT_DIM // 2
    inv = ROPE_THETA ** (-2.0 * jnp.arange(half, dtype=F32) / ROT_DIM)
    ang = pos.astype(F32)[:, None] * inv[None, :]
    cos, sin = jnp.cos(ang), jnp.sin(ang)
    zeros = jnp.zeros_like(cos)
    rest = HEAD_DIM - ROT_DIM
    steps = pos.shape[0]
    one_head = lambda first, second, fill: jnp.concatenate(
        [first, second, jnp.full((steps, rest), fill, F32)], axis=1)
    cos_t = one_head(cos, cos, 1.0)
    up_t = one_head(-sin, zeros, 0.0)
    dn_t = one_head(zeros, sin, 0.0)
    reps = LANES // HEAD_DIM
    return tuple(jnp.tile(t, (1, reps)) for t in (cos_t, up_t, dn_t))


CAST_TILE = 512
FFN_TILE = 256
MERGE_CAST_TILE = 256
GATE_COLS = 2 * M_HEADS
PROJ_HEAD = ATT_WIDTH + 2 * KV_WIDTH + 2 * M_QK_WIDTH + M_V_WIDTH


def _group_layer(x, mod, pos, wts, raw, k_prev, v_prev, state, *, attn_bb, attn_tq, mask_first):
    batch, steps, d = x.shape
    ropes = _rope_tables(pos)
    casting = raw is not None
    tile = CAST_TILE if casting else COL_TILE

    def dense(name, body, act, keys, casts, extras, extra_specs, outs, n_cols, wtile, side_keys=(), rows=ROW_TILE):
        n_tiles = -(-n_cols // wtile)
        if not casting:
            weights = [(wts[key], wtile, lambda j: j) for key in keys]
            return _matmul(body, act, weights, extras, extra_specs, outs, n_tiles=n_tiles, rows=rows, name=name)
        res = _matmul_casting(body, act, casts, extras, extra_specs, outs, n_tiles=n_tiles, name=name,
                              n_side=len(side_keys))
        for key, w in zip(list(keys) + list(side_keys), res[len(outs):]):
            wts[key] = w
        return res[0] if len(outs) == 1 else res[:len(outs)]

    def proj_rows(first, total):
        srcs = [(raw["w_in_t"], (1, tile, d), lambda j: (0, first // tile + j, 0))]
        return [(srcs, lambda w: w[0].T, (d, total), (d, tile), lambda j: (0, j))]

    def proj_rows_shifted(first, total):
        base = first - GATE_COLS
        srcs = [(raw["w_in_t"], (1, tile, d), lambda j: (0, base // tile + j, 0)),
                (raw["w_in_t"], (1, GATE_COLS, d), lambda j: (0, (base + tile * (j + 1)) // GATE_COLS, 0))]

        def transform(main, nxt):
            rows = jnp.concatenate([main[0], nxt[0]], axis=0)
            return rows[GATE_COLS:GATE_COLS + tile].T

        return [(srcs, transform, (d, total), (d, tile), lambda j: (0, j))]

    def columns(key, first, total, width):
        srcs = [(raw[key], (1, d, width), lambda j: (0, 0, first // width + j))]
        return (srcs, lambda w: w[0], (d, total), (d, width), lambda j: (0, j))

    def mod_full(idx):
        return lambda bb, tt: pl.BlockSpec((bb, 1, d), lambda b, t_, j: (b, 0, idx))

    u, ka, va, gates = _matmul(
        _mm_mod_kvg_body, x, [(wts["w_kv"], 2 * KV_WIDTH, lambda j: 0), (wts["w_g"], GATE_PAD, lambda j: 0)],
        [mod, mod, *ropes], [mod_full(SC1), mod_full(SH1)] + _rope_specs(),
        [(d, BF16, d), (KV_WIDTH, F32, KV_WIDTH), (KV_WIDTH, F32, KV_WIDTH), (GATE_PAD, F32, GATE_PAD)],
        n_tiles=1, rows=ROW_TILE // 2, name="proj_kvg")

    mem_first = ATT_WIDTH + 2 * KV_WIDTH
    za_width = 2 * M_QK_WIDTH + M_V_WIDTH
    zb_width = M_V_WIDTH + 2 * d
    qa = dense("proj_q", _mm_rope_body, u, ["w_qa"], casting and proj_rows(0, ATT_WIDTH),
               ropes, _rope_specs(), [(ATT_WIDTH, BF16, tile)], ATT_WIDTH, tile)
    za = dense("proj_mem", _mm_plain_body, u, ["w_mem"], casting and proj_rows(mem_first, za_width),
               [], [], [(za_width, BF16, tile)], za_width, tile)
    zb = dense("proj_gates", _mm_plain_body, u, ["w_gates"],
               casting and proj_rows_shifted(PROJ_HEAD + GATE_COLS, zb_width),
               [], [], [(zb_width, BF16, tile)], zb_width, tile)

    if k_prev is None:
        k_prev, v_prev = ka, va
    ya = _attention(qa, ka, va, k_prev, v_prev, wts["sink_tab"], bb=attn_bb, tq=attn_tq, mask_first=mask_first)

    yb, c_new, n_new, m_new = _mlstm(za, zb, gates, wts["gate_bias"], *state)

    mtile = MERGE_CAST_TILE if casting else COL_TILE
    ga_col = M_V_WIDTH // mtile
    gb_col = ga_col + d // mtile
    t = dense("merge_a", _mm_gate_first_body, ya, ["w_up_a"], casting and [columns("w_up_a", 0, d, mtile)],
              [zb], [lambda bb, tt: _act_spec(bb, tt, mtile, ga_col)], [(d, F32, mtile)], d, mtile)
    merged = dense("merge_b", _mm_gate_second_body, yb, ["w_up_b"], casting and [columns("w_up_b", 0, d, mtile)],
                   [zb, t], [lambda bb, tt: _act_spec(bb, tt, mtile, gb_col),
                             lambda bb, tt: _act_spec(bb, tt, mtile)], [(d, BF16, mtile)], d, mtile)

    def mod_spec(idx, width):
        per_vec = d // width
        return lambda bb, tt: pl.BlockSpec((bb, 1, width), lambda b, t_, j: (b, 0, idx * per_vec + j))

    r1 = dense("out_proj", _mm_residual_body, merged, ["w_o"], casting and [columns("w_o", 0, d, mtile)],
               [x, mod], [lambda bb, tt: _act_spec(bb, tt, mtile), mod_spec(GT1, mtile)],
               [(d, F32, mtile)], d, mtile)
    x1, u2 = _layer_norm(r1, wts["ln1_g"], wts["ln1_b"], mod, SC2, SH2)

    ff_tile = FFN_TILE
    ffn_casts = casting and [
        columns("w_ffn_in", 0, D_FF, ff_tile), columns("w_ffn_in", D_FF, D_FF, ff_tile),
        ([(raw["w_ffn_out"], (1, ff_tile, d), lambda j: (0, j, 0))], lambda w: w[0],
         (D_FF, d), (ff_tile, d), lambda j: (j, 0))]
    hidden = dense("ffn_in", _mm_swiglu_body, u2, ["w_ffn_gate", "w_ffn_up"], ffn_casts,
                   [], [], [(D_FF, BF16, ff_tile)], D_FF, ff_tile, side_keys=["w_ffn_out"] if casting else (),
                   rows=2 * ROW_TILE)

    k_half = D_FF // 2
    out_tile = COL_TILE // 2
    w_out = wts["w_ffn_out"]
    part = _matmul(_mm_plain_body, hidden, [(w_out, COL_TILE, lambda j: j)], [], [],
                   [(d, F32, COL_TILE)], n_tiles=d // COL_TILE, k_block=(k_half, 0), name="ffn_out_lo")
    r2 = _matmul(_mm_residual_add_body, hidden, [(w_out, out_tile, lambda j: j)], [part, x1, mod],
                 [lambda bb, tt: _act_spec(bb, tt, out_tile), lambda bb, tt: _act_spec(bb, tt, out_tile),
                  mod_spec(GT2, out_tile)],
                 [(d, F32, out_tile)], n_tiles=d // out_tile, k_block=(k_half, 1), name="ffn_out_hi")
    y = _layer_norm(r2, wts["ln2_g"], wts["ln2_b"])
    return y, ka, va, c_new, n_new, m_new


def _cast_body(w_ref, o_ref):
    o_ref[...] = w_ref[0].T.astype(o_ref.dtype)


def _cast_gates_body(w_ref, o_ref):
    x = w_ref[0].T
    valid = lax.broadcasted_iota(jnp.int32, (1, LANES), 1) < M_HEADS
    o_ref[:, :LANES] = jnp.where(valid, x, 0.0).astype(o_ref.dtype)
    f_first = pltpu.roll(x, LANES - M_HEADS, axis=1)
    o_ref[:, LANES:] = jnp.where(valid, f_first, 0.0).astype(o_ref.dtype)


def _cast_kv_gates(w_in_t):
    _, _, k = w_in_t.shape
    params = _params(("parallel",))
    first = ATT_WIDTH // CAST_TILE
    w_kv = pl.pallas_call(
        _cast_body, grid=(2 * KV_WIDTH // CAST_TILE,),
        in_specs=[pl.BlockSpec((1, CAST_TILE, k), lambda j: (0, first + j, 0))],
        out_specs=pl.BlockSpec((k, CAST_TILE), lambda j: (0, j)),
        out_shape=jax.ShapeDtypeStruct((k, 2 * KV_WIDTH), BF16),
        compiler_params=params, name="cast_proj_kv",
    )(w_in_t)
    w_g = pl.pallas_call(
        _cast_gates_body, grid=(1,),
        in_specs=[pl.BlockSpec((1, LANES, k), lambda j: (0, PROJ_HEAD // LANES, 0))],
        out_specs=pl.BlockSpec((k, GATE_PAD), lambda j: (0, 0)),
        out_shape=jax.ShapeDtypeStruct((k, GATE_PAD), BF16),
        compiler_params=params, name="cast_proj_gates",
    )(w_in_t)
    return w_kv, w_g


def kernel(x_prompt, x_sample, cache_k_win, cache_v_win, state_C, state_n, state_m, c_prompt, c_sample, w_ada, b_ada, w_in, b_if, attn_sinks, w_up_a, w_up_b, w_o, ln1_g, ln1_b, w_ffn_in, w_ffn_out, ln2_g, ln2_b):
    bp, sp, d = x_prompt.shape
    bs, ts, _ = x_sample.shape
    keep = cache_k_win.shape[2]
    assert w_in.shape[0] == DEPTH == 1
    l = 0

    raw = {"w_in_t": jnp.swapaxes(w_in, 1, 2), "w_up_a": w_up_a, "w_up_b": w_up_b, "w_o": w_o,
           "w_ffn_in": w_ffn_in, "w_ffn_out": w_ffn_out}
    w_kv, w_g = _cast_kv_gates(raw["w_in_t"])
    pad_bias = jnp.zeros((LANES - M_HEADS,), F32)
    wts = {
        "w_kv": w_kv, "w_g": w_g,
        "gate_bias": jnp.concatenate([b_if[l, :M_HEADS], pad_bias, b_if[l, M_HEADS:], pad_bias]).reshape(1, GATE_PAD),
        "sink_tab": jnp.broadcast_to(attn_sinks[l].astype(F32).reshape(N_KV_HEADS, 1, GROUP, 1),
                                     (N_KV_HEADS, 1, GROUP, CHUNK)).reshape(N_KV_HEADS, 1, GROUP * CHUNK),
        "ln1_g": ln1_g[l], "ln1_b": ln1_b[l], "ln2_g": ln2_g[l], "ln2_b": ln2_b[l],
    }

    c_all = jnp.concatenate([c_prompt, c_sample], axis=0)
    rows = c_all.shape[0]
    rows_pad = -(-rows // 8) * 8
    c_all = jnp.pad(c_all, ((0, rows_pad - rows), (0, 0)))
    mod = _adaln(c_all, w_ada[l], b_ada[l])
    mod_p = mod[:bp].reshape(bp, 1, 6 * d)
    mod_s = mod[bp:bp + bs].reshape(bs, 1, 6 * d)

    def pad_lanes(m):
        return jnp.pad(m, ((0, 0), (0, LANES - M_HEADS))).reshape(m.shape[0], 1, LANES)

    ck = cache_k_win[l].reshape(bs, keep, KV_WIDTH)
    cv = cache_v_win[l].reshape(bs, keep, KV_WIDTH)
    state_s = (state_C[l], state_n[l], pad_lanes(state_m[l]))
    ys, kas, vas, cs, ns, ms = _group_layer(
        x_sample, mod_s, PAST_LEN + jnp.arange(ts), wts, raw, ck, cv, state_s,
        attn_bb=min(bs, ATTN_ROWS // ts), attn_tq=ts, mask_first=False)

    state_p = (jnp.zeros((bp, M_HEADS, M_V_DIM, M_QK_DIM), F32),
               jnp.zeros((bp, M_HEADS, M_QK_DIM), F32),
               pad_lanes(jnp.full((bp, M_HEADS), M_INIT, F32)))
    yp, kap, vap, cp, np_, mp = _group_layer(
        x_prompt, mod_p, jnp.arange(sp), wts, None, None, None, state_p,
        attn_bb=1, attn_tq=min(ATTN_ROWS, sp), mask_first=True)

    def heads(a):
        return a.reshape(a.shape[0], a.shape[1], N_KV_HEADS, HEAD_DIM)

    kws = jnp.concatenate([ck, kas], axis=1)[:, -keep:]
    vws = jnp.concatenate([cv, vas], axis=1)[:, -keep:]
    return (yp, ys,
            heads(kap[:, -keep:])[None], heads(vap[:, -keep:])[None],
            cp[None], np_[None], mp[:, 0, :M_HEADS][None],
            heads(kws)[None], heads(vws)[None],
            cs[None], ns[None], ms[:, 0, :M_HEADS][None])
```

```python
import functools

import jax
import jax.numpy as jnp
from jax import lax
from jax.experimental import pallas as pl
from jax.experimental.pallas import tpu as pltpu

F32 = jnp.float32
BF16 = jnp.bfloat16

D_MODEL = 4096
CHUNK = 64
N_Q_HEADS = 64
N_KV_HEADS = 8
HEAD_DIM = 64
GROUP = N_Q_HEADS // N_KV_HEADS
WINDOW = 128
ROT_DIM = HEAD_DIM // 4
ROPE_THETA = 500000.0
M_HEADS = 8
M_QK_DIM = 256
M_V_DIM = 512
ATT_WIDTH = N_Q_HEADS * HEAD_DIM
KV_WIDTH = N_KV_HEADS * HEAD_DIM
M_QK_WIDTH = M_HEADS * M_QK_DIM
M_V_WIDTH = M_HEADS * M_V_DIM
D_FF = 11008
DEPTH = 1
PAST_LEN = 2048
ALPHA = (2.0 * DEPTH) ** 0.25
LN_EPS = 1e-5
M_INIT = -1e30

LANES = 128
ROW_TILE = 1024
COL_TILE = 1024
ROW_BANDS = 4
ADA_TILE = 512
ATTN_ROWS = 4096
MLSTM_ROWS = 256
GATE_PAD = 2 * LANES
VMEM_LIMIT = 60 * 1024 * 1024


def _params(semantics):
    return pltpu.CompilerParams(dimension_semantics=semantics, vmem_limit_bytes=VMEM_LIMIT)


def _row_blocks(batch, steps, rows):
    if steps >= rows:
        assert steps % rows == 0
        return 1, rows
    bb = min(batch, rows // steps)
    assert batch % bb == 0
    return bb, steps


def _adaln_body(c_ref, w_ref, b_ref, o_ref):
    c = c_ref[...]
    s = c * jax.nn.sigmoid(c)
    o_ref[...] = jnp.dot(s, w_ref[...], preferred_element_type=F32) + b_ref[...]


def _adaln(c, w_ada, b_ada):
    rows, d = c.shape
    n = w_ada.shape[1]
    tn = ADA_TILE
    return pl.pallas_call(
        _adaln_body,
        grid=(n // tn,),
        in_specs=[pl.BlockSpec((rows, d), lambda j: (0, 0)),
                  pl.BlockSpec((d, tn), lambda j: (0, j)),
                  pl.BlockSpec((1, tn), lambda j: (0, j))],
        out_specs=pl.BlockSpec((rows, tn), lambda j: (0, j)),
        out_shape=jax.ShapeDtypeStruct((rows, n), F32),
        compiler_params=_params(("parallel",)),
        name="adaln",
    )(c, w_ada, b_ada.reshape(1, n))


def _load_rows(a_ref):
    bb, tt, k = a_ref.shape
    return a_ref[...].reshape(bb * tt, k)


def _bands(ref):
    bb, tt = ref.shape[0], ref.shape[1]
    if bb > 1:
        n = min(ROW_BANDS, bb)
        assert bb % n == 0
        return [(slice(p * (bb // n), (p + 1) * (bb // n)), slice(0, tt)) for p in range(n)]
    assert tt % ROW_BANDS == 0
    return [(slice(0, 1), slice(p * (tt // ROW_BANDS), (p + 1) * (tt // ROW_BANDS))) for p in range(ROW_BANDS)]


def _band(ref, bs, ts):
    if ref.shape[1] == 1:
        return ref[bs, :, :]
    blk = ref[bs, ts, :]
    return blk.reshape(blk.shape[0] * blk.shape[1], blk.shape[2])


def _store_band(ref, bs, ts, rows):
    nb = bs.stop - bs.start
    ref[bs, ts, :] = rows.reshape(nb, ts.stop - ts.start, rows.shape[-1]).astype(ref.dtype)


def _rope_rows(x, cos_ref, sa_ref, sb_ref, bs, ts):
    rows, n = x.shape
    nb = bs.stop - bs.start

    def table(ref):
        t = ref[ts, :]
        if nb == 1:
            return t
        return jnp.broadcast_to(t[None], (nb, t.shape[0], LANES)).reshape(rows, LANES)

    cos, sa, sb = table(cos_ref), table(sa_ref), table(sb_ref)
    outs = []
    for s in range(n // LANES):
        xs = x[:, s * LANES:(s + 1) * LANES]
        up = pltpu.roll(xs, LANES - ROT_DIM // 2, axis=1)
        dn = pltpu.roll(xs, ROT_DIM // 2, axis=1)
        outs.append(xs * cos + up * sa + dn * sb)
    return jnp.concatenate(outs, axis=1)


def _mm_plain_body(a_ref, w_ref, o_ref):
    acc = jnp.dot(_load_rows(a_ref), w_ref[...], preferred_element_type=F32)
    o_ref[...] = acc.reshape(o_ref.shape).astype(o_ref.dtype)


def _mm_rope_body(a_ref, w_ref, cos_ref, sa_ref, sb_ref, o_ref):
    for bs, ts in _bands(a_ref):
        acc = jnp.dot(_band(a_ref, bs, ts), w_ref[...], preferred_element_type=F32)
        _store_band(o_ref, bs, ts, _rope_rows(acc, cos_ref, sa_ref, sb_ref, bs, ts))


def _mm_mod_kvg_body(x_ref, w_ref, wg_ref, sc_ref, sh_ref, cos_ref, sa_ref, sb_ref, u_ref, k_ref, v_ref, g_ref):
    u_ref[...] = (x_ref[...] * (1.0 + sc_ref[...]) + sh_ref[...]).astype(u_ref.dtype)
    for bs, ts in _bands(x_ref):
        a = _band(u_ref, bs, ts)
        acc = jnp.dot(a, w_ref[...], preferred_element_type=F32)
        _store_band(k_ref, bs, ts, _rope_rows(acc[:, :KV_WIDTH], cos_ref, sa_ref, sb_ref, bs, ts))
        _store_band(v_ref, bs, ts, acc[:, KV_WIDTH:])
        _store_band(g_ref, bs, ts, jnp.dot(a, wg_ref[...], preferred_element_type=F32))


def _gated(acc, g_ref, bs, ts):
    return jax.nn.sigmoid(_band(g_ref, bs, ts).astype(F32)) * acc


def _mm_gate_first_body(a_ref, w_ref, g_ref, o_ref):
    for bs, ts in _bands(a_ref):
        acc = jnp.dot(_band(a_ref, bs, ts), w_ref[...], preferred_element_type=F32)
        _store_band(o_ref, bs, ts, _gated(acc, g_ref, bs, ts))


def _mm_gate_second_body(a_ref, w_ref, g_ref, t_ref, o_ref):
    for bs, ts in _bands(a_ref):
        acc = jnp.dot(_band(a_ref, bs, ts), w_ref[...], preferred_element_type=F32)
        _store_band(o_ref, bs, ts, _band(t_ref, bs, ts) + _gated(acc, g_ref, bs, ts))


def _mm_residual_body(a_ref, w_ref, x_ref, g_ref, o_ref):
    acc = jnp.dot(_load_rows(a_ref), w_ref[...], preferred_element_type=F32)
    o_ref[...] = ALPHA * x_ref[...] + g_ref[...] * acc.reshape(o_ref.shape)


def _mm_swiglu_body(a_ref, wg_ref, wu_ref, o_ref):
    for bs, ts in _bands(a_ref):
        a = _band(a_ref, bs, ts)
        gate = jnp.dot(a, wg_ref[...], preferred_element_type=F32)
        up = jnp.dot(a, wu_ref[...], preferred_element_type=F32)
        _store_band(o_ref, bs, ts, gate * jax.nn.sigmoid(gate) * up)


def _act_spec(bb, tt, width, col=None):
    if col is None:
        return pl.BlockSpec((bb, tt, width), lambda b, t, j: (b, t, j))
    return pl.BlockSpec((bb, tt, width), lambda b, t, j: (b, t, j + col))


def _matmul(body, a, weights, extras, extra_specs, outs, *, n_tiles, rows=ROW_TILE, name):
    batch, steps, k = a.shape
    bb, tt = _row_blocks(batch, steps, rows)
    grid = (batch // bb, steps // tt, n_tiles)
    in_specs = [pl.BlockSpec((bb, tt, k), lambda b, t, j: (b, t, 0))]
    in_specs += [pl.BlockSpec((k, cols), functools.partial(lambda b, t, j, col: (0, col(j)), col=col))
                 for _, cols, col in weights]
    in_specs += [spec(bb, tt) for spec in extra_specs]
    weights = [w for w, _, _ in weights]
    out_specs = [pl.BlockSpec((bb, tt, cols_tile), lambda b, t, j: (b, t, j)) for _, _, cols_tile in outs]
    out_shape = [jax.ShapeDtypeStruct((batch, steps, cols), dt) for cols, dt, _ in outs]
    single = len(outs) == 1
    return pl.pallas_call(
        body,
        grid=grid,
        in_specs=in_specs,
        out_specs=out_specs[0] if single else out_specs,
        out_shape=out_shape[0] if single else out_shape,
        compiler_params=_params(("parallel", "parallel", "parallel")),
        name=name,
    )(a, *weights, *extras)


def _matmul_casting(body, a, casts, extras, extra_specs, outs, *, n_tiles, name, n_side=0):
    batch, steps, k = a.shape

    def per_tile(spec):
        return pl.BlockSpec(spec.block_shape, lambda j: spec.index_map(0, 0, j))

    in_specs = [pl.BlockSpec((batch, steps, k), lambda j: (0, 0, 0), pipeline_mode=pl.Buffered(1))]
    sources = []
    for srcs, *_ in casts:
        for arr, shape, index in srcs:
            in_specs.append(pl.BlockSpec(shape, index))
            sources.append(arr)
    in_specs += [per_tile(spec(batch, steps)) for spec in extra_specs]
    out_specs = [pl.BlockSpec((batch, steps, cols_tile), lambda j: (0, 0, j)) for _, _, cols_tile in outs]
    out_shape = [jax.ShapeDtypeStruct((batch, steps, cols), dt) for cols, dt, _ in outs]
    out_specs += [pl.BlockSpec(block, index) for _, _, _, block, index in casts]
    out_shape += [jax.ShapeDtypeStruct(shape, BF16) for _, _, shape, _, _ in casts]
    counts = [len(srcs) for srcs, *_ in casts]
    transforms = [tr for _, tr, *_ in casts]
    n_used = len(casts) - n_side
    n_extra, n_out = len(extras), len(outs)

    def casting_body(a_ref, *refs):
        src_refs, pos = [], 0
        for n in counts:
            src_refs.append(refs[pos:pos + n])
            pos += n
        extra_refs = refs[pos:pos + n_extra]
        out_refs = refs[pos + n_extra:pos + n_extra + n_out]
        w_refs = refs[pos + n_extra + n_out:]
        for w_ref, srcs, transform in zip(w_refs, src_refs, transforms):
            w_ref[...] = transform(*[s[...] for s in srcs]).astype(w_ref.dtype)
        body(a_ref, *w_refs[:n_used], *extra_refs, *out_refs)

    return pl.pallas_call(
        casting_body,
        grid=(n_tiles,),
        in_specs=in_specs,
        out_specs=out_specs,
        out_shape=out_shape,
        compiler_params=_params(("arbitrary",)),
        name=name,
    )(a, *sources, *extras)


def _rope_specs():
    return [lambda bb, tt: pl.BlockSpec((tt, LANES), lambda b, t, j: (t, 0))] * 3


def _attn_body(q_ref, kc_ref, kp_ref, vc_ref, vp_ref, sink_ref, o_ref, *, mask_first):
    bb, tq, _ = q_ref.shape
    n_chunks = tq // CHUNK
    n_keys = WINDOW + CHUNK
    i = pl.program_id(1)
    h = pl.program_id(2)

    lane = lax.broadcasted_iota(jnp.int32, (1, LANES), 1)
    keep = (lane // HEAD_DIM) == (h % 2)
    sink = sink_ref[0]
    lane_head = lax.broadcasted_iota(jnp.int32, (CHUNK, 2 * LANES), 1) // HEAD_DIM
    out_low = lax.broadcasted_iota(jnp.int32, (CHUNK, LANES), 1) < HEAD_DIM
    key_row = lax.broadcasted_iota(jnp.int32, (n_keys, 1), 0)
    ones = jnp.ones((16, n_keys), BF16)

    def both_halves(cur_ref, prev_ref, bi):
        pair = jnp.concatenate([prev_ref[bi].astype(F32), cur_ref[bi].astype(F32)], axis=0)
        swapped = pltpu.roll(pair, HEAD_DIM, axis=1)
        return jnp.where(keep, pair, swapped)

    def scores(bi, j, k4):
        q = q_ref[bi, j * CHUNK:(j + 1) * CHUNK, :] * (HEAD_DIM ** -0.5)
        parts = []
        for cg in range(GROUP // 4):
            qc = q[:, cg * 2 * LANES:(cg + 1) * 2 * LANES]
            for r in range(4):
                parts.append(jnp.where(lane_head == r, qc, jnp.zeros_like(qc)))
        qrows = jnp.concatenate(parts, axis=0)
        keys = k4[j * CHUNK:j * CHUNK + n_keys]
        st = lax.dot_general(keys, qrows, (((1,), (1,)), ((), ())), preferred_element_type=F32)
        if mask_first and j < WINDOW // CHUNK:
            first_valid = WINDOW - (i * tq + j * CHUNK)
            st = jnp.where(key_row >= first_valid, st, -jnp.inf)
        return st

    def values_t(j, v2):
        vwin = v2[j * CHUNK:j * CHUNK + n_keys]
        head = vwin[:LANES].T
        tail = vwin[LANES:]
        tail = jnp.concatenate([tail, tail], axis=0).T
        vt = jnp.concatenate([head[:HEAD_DIM], tail[:HEAD_DIM, :CHUNK]], axis=1).astype(BF16)
        return jnp.concatenate([vt, ones], axis=0)

    def finish(st, lhs):
        m = jnp.maximum(jnp.max(st, axis=0, keepdims=True), sink)
        pt = jnp.exp(st - m).astype(BF16)
        ot = jnp.dot(lhs, pt, preferred_element_type=F32)
        den = ot[HEAD_DIM:HEAD_DIM + 1] + jnp.exp(sink - m)
        return ot[:HEAD_DIM] / den

    def emit(bi, j, on):
        for pr in range(GROUP // 2):
            x = on[:, pr * LANES:(pr + 1) * LANES]
            xt = jnp.concatenate([x, x], axis=0).T
            o_ref[bi, j * CHUNK:(j + 1) * CHUNK, pr * LANES:(pr + 1) * LANES] = (
                jnp.where(out_low, xt[:CHUNK], xt[CHUNK:]).astype(o_ref.dtype))

    units, sts, lhss = [], [], []
    for bi in range(bb):
        k2 = both_halves(kc_ref, kp_ref, bi)
        v2 = both_halves(vc_ref, vp_ref, bi)
        k4 = jnp.concatenate([k2, k2], axis=1).astype(BF16)
        for j in range(n_chunks):
            units.append((bi, j))
            sts.append(scores(bi, j, k4))
            lhss.append(values_t(j, v2))
    for (bi, j), st, lhs in zip(units, sts, lhss):
        emit(bi, j, finish(st, lhs))


def _attention(q, k_cur, v_cur, k_prev, v_prev, sink_tab, *, bb, tq, mask_first):
    batch, steps, _ = q.shape
    qw = GROUP * HEAD_DIM
    same_array = k_prev is k_cur
    per_tile = tq // WINDOW

    def prev_map(b, i, h):
        if same_array:
            return (b, jnp.maximum(i * per_tile - 1, 0), h // 2)
        return (b, 0, h // 2)

    return pl.pallas_call(
        functools.partial(_attn_body, mask_first=mask_first),
        grid=(batch // bb, steps // tq, N_KV_HEADS),
        in_specs=[pl.BlockSpec((bb, tq, qw), lambda b, i, h: (b, i, h)),
                  pl.BlockSpec((bb, tq, LANES), lambda b, i, h: (b, i, h // 2)),
                  pl.BlockSpec((bb, WINDOW, LANES), prev_map),
                  pl.BlockSpec((bb, tq, LANES), lambda b, i, h: (b, i, h // 2)),
                  pl.BlockSpec((bb, WINDOW, LANES), prev_map),
                  pl.BlockSpec((1, 1, GROUP * CHUNK), lambda b, i, h: (h, 0, 0))],
        out_specs=pl.BlockSpec((bb, tq, qw), lambda b, i, h: (b, i, h)),
        out_shape=jax.ShapeDtypeStruct((batch, steps, ATT_WIDTH), BF16),
        compiler_params=_params(("parallel", "parallel", "parallel")),
        name="attention",
    )(q, k_cur, k_prev, v_cur, v_prev, sink_tab)


def _log_sigmoid(x):
    return jnp.minimum(x, 0.0) - jnp.log1p(jnp.exp(-jnp.abs(x)))


def _mlstm_body(q_ref, k_ref, v_ref, o_ref, g_ref, bias_ref, c0_ref, n0_ref, m0_ref,
                y_ref, c_ref, n_ref, m_ref, *, single_step):
    if single_step:
        c_in, n_in, m_in = c0_ref, n0_ref, m0_ref
    else:
        c_in, n_in, m_in = c_ref, n_ref, m_ref

        @pl.when(pl.program_id(1) == 0)
        def _():
            c_ref[...] = c0_ref[...]
            n_ref[...] = n0_ref[...]
            m_ref[...] = m0_ref[...]

    steps = q_ref.shape[1]
    g = g_ref[0] + bias_ref[...]
    li = g[:, :LANES]
    lf = _log_sigmoid(g[:, LANES:])
    row = lax.broadcasted_iota(jnp.int32, (steps, steps), 0)
    col = lax.broadcasted_iota(jnp.int32, (steps, steps), 1)
    causal = col <= row
    b = jnp.dot(causal.astype(F32), lf, preferred_element_type=F32,
                precision=lax.Precision.HIGHEST)
    r = li - b
    m_prev = m_in[0]
    b_last = b[steps - 1:steps, :]
    m_new = b_last + jnp.maximum(m_prev, jnp.max(r, axis=0, keepdims=True))
    w_state = jnp.exp(r + (b_last - m_new))
    decay = jnp.exp(b_last + m_prev - m_new)
    r_rows = r.T

    for h in range(M_HEADS):
        r_row = r_rows[h:h + 1, :]
        dmat = jnp.where(causal, jnp.broadcast_to(r_row, (steps, steps)), -jnp.inf)
        m_h = m_prev[:, h:h + 1]
        gmax = jnp.maximum(jnp.max(dmat, axis=-1, keepdims=True), m_h)
        w_intra = jnp.exp(dmat - gmax)
        w_inter = jnp.exp(m_h - gmax)
        inv_floor = jnp.exp(-(b[:, h:h + 1] + gmax))

        q = q_ref[0, :, h * M_QK_DIM:(h + 1) * M_QK_DIM] * (M_QK_DIM ** -0.5)
        k = k_ref[0, :, h * M_QK_DIM:(h + 1) * M_QK_DIM]
        v = v_ref[0, :, h * M_V_DIM:(h + 1) * M_V_DIM]
        c_old = c_in[0, h]
        n_old = n_in[0, h:h + 1, :]

        qk = lax.dot_general(q, k, (((1,), (1,)), ((), ())), preferred_element_type=F32)
        s = qk * w_intra
        intra = jnp.dot(s.astype(BF16), v, preferred_element_type=F32)
        inter = lax.dot_general(q, c_old.astype(BF16), (((1,), (1,)), ((), ())),
                                preferred_element_type=F32)
        nq = jnp.sum(q.astype(F32) * n_old, axis=-1, keepdims=True)
        num = intra + w_inter * inter
        den = jnp.sum(s, axis=-1, keepdims=True) + w_inter * nq
        hval = num / jnp.maximum(jnp.abs(den), inv_floor)
        og = jax.nn.sigmoid(o_ref[0, :, h * M_V_DIM:(h + 1) * M_V_DIM].astype(F32))
        y_ref[0, :, h * M_V_DIM:(h + 1) * M_V_DIM] = (og * hval).astype(y_ref.dtype)

        ws = w_state[:, h:h + 1]
        vw = (v.astype(F32) * ws).astype(BF16)
        dc = lax.dot_general(vw, k, (((0,), (0,)), ((), ())), preferred_element_type=F32)
        dec = decay[:, h:h + 1]
        c_ref[0, h] = dec * c_old + dc
        n_ref[0, h:h + 1, :] = dec * n_old + jnp.sum(k.astype(F32) * ws, axis=0, keepdims=True)

    m_ref[0] = m_new


def _mlstm(za, zb, gates, bias, c0, n0, m0):
    batch, steps, _ = za.shape
    rows = min(steps, MLSTM_ROWS)
    state_specs = [pl.BlockSpec((1, M_HEADS, M_V_DIM, M_QK_DIM), lambda b, c: (b, 0, 0, 0)),
                   pl.BlockSpec((1, M_HEADS, M_QK_DIM), lambda b, c: (b, 0, 0)),
                   pl.BlockSpec((1, 1, LANES), lambda b, c: (b, 0, 0))]
    return pl.pallas_call(
        functools.partial(_mlstm_body, single_step=steps == rows),
        grid=(batch, steps // rows),
        in_specs=[pl.BlockSpec((1, rows, M_QK_WIDTH), lambda b, c: (b, c, 0)),
                  pl.BlockSpec((1, rows, M_QK_WIDTH), lambda b, c: (b, c, 1)),
                  pl.BlockSpec((1, rows, M_V_WIDTH), lambda b, c: (b, c, 1)),
                  pl.BlockSpec((1, rows, M_V_WIDTH), lambda b, c: (b, c, 0)),
                  pl.BlockSpec((1, rows, GATE_PAD), lambda b, c: (b, c, 0)),
                  pl.BlockSpec((1, GATE_PAD), lambda b, c: (0, 0))] + state_specs,
        out_specs=[pl.BlockSpec((1, rows, M_V_WIDTH), lambda b, c: (b, c, 0))] + state_specs,
        out_shape=[jax.ShapeDtypeStruct((batch, steps, M_V_WIDTH), BF16),
                   jax.ShapeDtypeStruct(c0.shape, F32),
                   jax.ShapeDtypeStruct(n0.shape, F32),
                   jax.ShapeDtypeStruct(m0.shape, F32)],
        compiler_params=_params(("parallel", "arbitrary")),
        name="mlstm",
    )(za, za, za, zb, gates, bias, c0, n0, m0)


def _ln_rows(x, g, b):
    mu = jnp.mean(x, axis=-1, keepdims=True)
    xc = x - mu
    var = jnp.mean(xc * xc, axis=-1, keepdims=True)
    return xc * lax.rsqrt(var + LN_EPS) * g + b


def _ln_mod_body(r_ref, g_ref, b_ref, sc_ref, sh_ref, x_ref, u_ref):
    y = _ln_rows(r_ref[...], g_ref[...], b_ref[...])
    x_ref[...] = y
    u_ref[...] = (y * (1.0 + sc_ref[...]) + sh_ref[...]).astype(u_ref.dtype)


def _ln_body(r_ref, g_ref, b_ref, x_ref):
    x_ref[...] = _ln_rows(r_ref[...], g_ref[...], b_ref[...])


def _layer_norm(r, g, b, mod=None, sc_idx=None, sh_idx=None):
    batch, steps, d = r.shape
    bb, tt = _row_blocks(batch, steps, ROW_TILE // 2)
    act = pl.BlockSpec((bb, tt, d), lambda i, t: (i, t, 0))
    vec = pl.BlockSpec((1, 1, d), lambda i, t: (0, 0, 0))
    g3, b3 = g.reshape(1, 1, d), b.reshape(1, 1, d)
    if mod is None:
        return pl.pallas_call(
            _ln_body, grid=(batch // bb, steps // tt),
            in_specs=[act, vec, vec], out_specs=act,
            out_shape=jax.ShapeDtypeStruct(r.shape, F32),
            compiler_params=_params(("parallel", "parallel")), name="layer_norm",
        )(r, g3, b3)
    return pl.pallas_call(
        _ln_mod_body, grid=(batch // bb, steps // tt),
        in_specs=[act, vec, vec,
                  pl.BlockSpec((bb, 1, d), lambda i, t: (i, 0, sc_idx)),
                  pl.BlockSpec((bb, 1, d), lambda i, t: (i, 0, sh_idx))],
        out_specs=[act, act],
        out_shape=[jax.ShapeDtypeStruct(r.shape, F32), jax.ShapeDtypeStruct(r.shape, BF16)],
        compiler_params=_params(("parallel", "parallel")), name="layer_norm_modulate",
    )(r, g3, b3, mod, mod)


SH1, SC1, GT1, SH2, SC2, GT2 = range(6)


def _rope_tables(pos):
    half = ROT_DIM // 2
    inv = ROPE_THETA ** (-2.0 * jnp.arange(half, dtype=F32) / ROT_DIM)
    ang = pos.astype(F32)[:, None] * inv[None, :]
    cos, sin = jnp.cos(ang), jnp.sin(ang)
    zeros = jnp.zeros_like(cos)
    rest = HEAD_DIM - ROT_DIM
    steps = pos.shape[0]
    one_head = lambda first, second, fill: jnp.concatenate(
        [first, second, jnp.full((steps, rest), fill, F32)], axis=1)
    cos_t = one_head(cos, cos, 1.0)
    up_t = one_head(-sin, zeros, 0.0)
    dn_t = one_head(zeros, sin, 0.0)
    reps = LANES // HEAD_DIM
    return tuple(jnp.tile(t, (1, reps)) for t in (cos_t, up_t, dn_t))


CAST_TILE = 512
FFN_TILE = 256
MERGE_CAST_TILE = 256
GATE_COLS = 2 * M_HEADS
PROJ_HEAD = ATT_WIDTH + 2 * KV_WIDTH + 2 * M_QK_WIDTH + M_V_WIDTH


def _group_layer(x, mod, pos, wts, raw, k_prev, v_prev, state, *, attn_bb, attn_tq, mask_first):
    batch, steps, d = x.shape
    ropes = _rope_tables(pos)
    casting = raw is not None
    tile = CAST_TILE if casting else COL_TILE

    def dense(name, body, act, keys, casts, extras, extra_specs, outs, n_cols, wtile, side_keys=(), rows=ROW_TILE):
        n_tiles = -(-n_cols // wtile)
        if not casting:
            weights = [(wts[key], wtile, lambda j: j) for key in keys]
            return _matmul(body, act, weights, extras, extra_specs, outs, n_tiles=n_tiles, rows=rows, name=name)
        res = _matmul_casting(body, act, casts, extras, extra_specs, outs, n_tiles=n_tiles, name=name,
                              n_side=len(side_keys))
        for key, w in zip(list(keys) + list(side_keys), res[len(outs):]):
            wts[key] = w
        return res[0] if len(outs) == 1 else res[:len(outs)]

    def proj_rows(first, total):
        srcs = [(raw["w_in_t"], (1, tile, d), lambda j: (0, first // tile + j, 0))]
        return [(srcs, lambda w: w[0].T, (d, total), (d, tile), lambda j: (0, j))]

    def proj_rows_shifted(first, total):
        base = first - GATE_COLS
        srcs = [(raw["w_in_t"], (1, tile, d), lambda j: (0, base // tile + j, 0)),
                (raw["w_in_t"], (1, GATE_COLS, d), lambda j: (0, (base + tile * (j + 1)) // GATE_COLS, 0))]

        def transform(main, nxt):
            rows = jnp.concatenate([main[0], nxt[0]], axis=0)
            return rows[GATE_COLS:GATE_COLS + tile].T

        return [(srcs, transform, (d, total), (d, tile), lambda j: (0, j))]

    def columns(key, first, total, width):
        srcs = [(raw[key], (1, d, width), lambda j: (0, 0, first // width + j))]
        return (srcs, lambda w: w[0], (d, total), (d, width), lambda j: (0, j))

    def mod_full(idx):
        return lambda bb, tt: pl.BlockSpec((bb, 1, d), lambda b, t_, j: (b, 0, idx))

    u, ka, va, gates = _matmul(
        _mm_mod_kvg_body, x, [(wts["w_kv"], 2 * KV_WIDTH, lambda j: 0), (wts["w_g"], GATE_PAD, lambda j: 0)],
        [mod, mod, *ropes], [mod_full(SC1), mod_full(SH1)] + _rope_specs(),
        [(d, BF16, d), (KV_WIDTH, F32, KV_WIDTH), (KV_WIDTH, F32, KV_WIDTH), (GATE_PAD, F32, GATE_PAD)],
        n_tiles=1, rows=ROW_TILE // 2, name="proj_kvg")

    mem_first = ATT_WIDTH + 2 * KV_WIDTH
    za_width = 2 * M_QK_WIDTH + M_V_WIDTH
    zb_width = M_V_WIDTH + 2 * d
    qa = dense("proj_q", _mm_rope_body, u, ["w_qa"], casting and proj_rows(0, ATT_WIDTH),
               ropes, _rope_specs(), [(ATT_WIDTH, BF16, tile)], ATT_WIDTH, tile)
    za = dense("proj_mem", _mm_plain_body, u, ["w_mem"], casting and proj_rows(mem_first, za_width),
               [], [], [(za_width, BF16, tile)], za_width, tile)
    zb = dense("proj_gates", _mm_plain_body, u, ["w_gates"],
               casting and proj_rows_shifted(PROJ_HEAD + GATE_COLS, zb_width),
               [], [], [(zb_width, BF16, tile)], zb_width, tile)

    if k_prev is None:
        k_prev, v_prev = ka, va
    ya = _attention(qa, ka, va, k_prev, v_prev, wts["sink_tab"], bb=attn_bb, tq=attn_tq, mask_first=mask_first)

    yb, c_new, n_new, m_new = _mlstm(za, zb, gates, wts["gate_bias"], *state)

    mtile = MERGE_CAST_TILE if casting else COL_TILE
    ga_col = M_V_WIDTH // mtile
    gb_col = ga_col + d // mtile
    t = dense("merge_a", _mm_gate_first_body, ya, ["w_up_a"], casting and [columns("w_up_a", 0, d, mtile)],
              [zb], [lambda bb, tt: _act_spec(bb, tt, mtile, ga_col)], [(d, F32, mtile)], d, mtile)
    merged = dense("merge_b", _mm_gate_second_body, yb, ["w_up_b"], casting and [columns("w_up_b", 0, d, mtile)],
                   [zb, t], [lambda bb, tt: _act_spec(bb, tt, mtile, gb_col),
                             lambda bb, tt: _act_spec(bb, tt, mtile)], [(d, BF16, mtile)], d, mtile)

    def mod_spec(idx, width):
        per_vec = d // width
        return lambda bb, tt: pl.BlockSpec((bb, 1, width), lambda b, t_, j: (b, 0, idx * per_vec + j))

    r1 = dense("out_proj", _mm_residual_body, merged, ["w_o"], casting and [columns("w_o", 0, d, mtile)],
               [x, mod], [lambda bb, tt: _act_spec(bb, tt, mtile), mod_spec(GT1, mtile)],
               [(d, F32, mtile)], d, mtile)
    x1, u2 = _layer_norm(r1, wts["ln1_g"], wts["ln1_b"], mod, SC2, SH2)

    ff_tile = FFN_TILE
    ffn_casts = casting and [
        columns("w_ffn_in", 0, D_FF, ff_tile), columns("w_ffn_in", D_FF, D_FF, ff_tile),
        ([(raw["w_ffn_out"], (1, ff_tile, d), lambda j: (0, j, 0))], lambda w: w[0],
         (D_FF, d), (ff_tile, d), lambda j: (j, 0))]
    hidden = dense("ffn_in", _mm_swiglu_body, u2, ["w_ffn_gate", "w_ffn_up"], ffn_casts,
                   [], [], [(D_FF, BF16, ff_tile)], D_FF, ff_tile, side_keys=["w_ffn_out"] if casting else (),
                   rows=2 * ROW_TILE)

    out_tile = COL_TILE // 2
    r2 = _matmul(_mm_residual_body, hidden, [(wts["w_ffn_out"], out_tile, lambda j: j)], [x1, mod],
                 [lambda bb, tt: _act_spec(bb, tt, out_tile), mod_spec(GT2, out_tile)],
                 [(d, F32, out_tile)], n_tiles=d // out_tile, rows=ROW_TILE // 2, name="ffn_out")
    y = _layer_norm(r2, wts["ln2_g"], wts["ln2_b"])
    return y, ka, va, c_new, n_new, m_new


def _cast_body(w_ref, o_ref):
    o_ref[...] = w_ref[0].T.astype(o_ref.dtype)


def _cast_gates_body(w_ref, o_ref):
    x = w_ref[0].T
    valid = lax.broadcasted_iota(jnp.int32, (1, LANES), 1) < M_HEADS
    o_ref[:, :LANES] = jnp.where(valid, x, 0.0).astype(o_ref.dtype)
    f_first = pltpu.roll(x, LANES - M_HEADS, axis=1)
    o_ref[:, LANES:] = jnp.where(valid, f_first, 0.0).astype(o_ref.dtype)


def _cast_kv_gates(w_in_t):
    _, _, k = w_in_t.shape
    params = _params(("parallel",))
    first = ATT_WIDTH // CAST_TILE
    w_kv = pl.pallas_call(
        _cast_body, grid=(2 * KV_WIDTH // CAST_TILE,),
        in_specs=[pl.BlockSpec((1, CAST_TILE, k), lambda j: (0, first + j, 0))],
        out_specs=pl.BlockSpec((k, CAST_TILE), lambda j: (0, j)),
        out_shape=jax.ShapeDtypeStruct((k, 2 * KV_WIDTH), BF16),
        compiler_params=params, name="cast_proj_kv",
    )(w_in_t)
    w_g = pl.pallas_call(
        _cast_gates_body, grid=(1,),
        in_specs=[pl.BlockSpec((1, LANES, k), lambda j: (0, PROJ_HEAD // LANES, 0))],
        out_specs=pl.BlockSpec((k, GATE_PAD), lambda j: (0, 0)),
        out_shape=jax.ShapeDtypeStruct((k, GATE_PAD), BF16),
        compiler_params=params, name="cast_proj_gates",
    )(w_in_t)
    return w_kv, w_g


def kernel(x_prompt, x_sample, cache_k_win, cache_v_win, state_C, state_n, state_m, c_prompt, c_sample, w_ada, b_ada, w_in, b_if, attn_sinks, w_up_a, w_up_b, w_o, ln1_g, ln1_b, w_ffn_in, w_ffn_out, ln2_g, ln2_b):
    bp, sp, d = x_prompt.shape
    bs, ts, _ = x_sample.shape
    keep = cache_k_win.shape[2]
    assert w_in.shape[0] == DEPTH == 1
    l = 0

    raw = {"w_in_t": jnp.swapaxes(w_in, 1, 2), "w_up_a": w_up_a, "w_up_b": w_up_b, "w_o": w_o,
           "w_ffn_in": w_ffn_in, "w_ffn_out": w_ffn_out}
    w_kv, w_g = _cast_kv_gates(raw["w_in_t"])
    pad_bias = jnp.zeros((LANES - M_HEADS,), F32)
    wts = {
        "w_kv": w_kv, "w_g": w_g,
        "gate_bias": jnp.concatenate([b_if[l, :M_HEADS], pad_bias, b_if[l, M_HEADS:], pad_bias]).reshape(1, GATE_PAD),
        "sink_tab": jnp.broadcast_to(attn_sinks[l].astype(F32).reshape(N_KV_HEADS, 1, GROUP, 1),
                                     (N_KV_HEADS, 1, GROUP, CHUNK)).reshape(N_KV_HEADS, 1, GROUP * CHUNK),
        "ln1_g": ln1_g[l], "ln1_b": ln1_b[l], "ln2_g": ln2_g[l], "ln2_b": ln2_b[l],
    }

    c_all = jnp.concatenate([c_prompt, c_sample], axis=0)
    rows = c_all.shape[0]
    rows_pad = -(-rows // 8) * 8
    c_all = jnp.pad(c_all, ((0, rows_pad - rows), (0, 0)))
    mod = _adaln(c_all, w_ada[l], b_ada[l])
    mod_p = mod[:bp].reshape(bp, 1, 6 * d)
    mod_s = mod[bp:bp + bs].reshape(bs, 1, 6 * d)

    def pad_lanes(m):
        return jnp.pad(m, ((0, 0), (0, LANES - M_HEADS))).reshape(m.shape[0], 1, LANES)

    ck = cache_k_win[l].reshape(bs, keep, KV_WIDTH)
    cv = cache_v_win[l].reshape(bs, keep, KV_WIDTH)
    state_s = (state_C[l], state_n[l], pad_lanes(state_m[l]))
    ys, kas, vas, cs, ns, ms = _group_layer(
        x_sample, mod_s, PAST_LEN + jnp.arange(ts), wts, raw, ck, cv, state_s,
        attn_bb=min(bs, ATTN_ROWS // ts), attn_tq=ts, mask_first=False)

    state_p = (jnp.zeros((bp, M_HEADS, M_V_DIM, M_QK_DIM), F32),
               jnp.zeros((bp, M_HEADS, M_QK_DIM), F32),
               pad_lanes(jnp.full((bp, M_HEADS), M_INIT, F32)))
    yp, kap, vap, cp, np_, mp = _group_layer(
        x_prompt, mod_p, jnp.arange(sp), wts, None, None, None, state_p,
        attn_bb=1, attn_tq=min(ATTN_ROWS, sp), mask_first=True)

    def heads(a):
        return a.reshape(a.shape[0], a.shape[1], N_KV_HEADS, HEAD_DIM)

    kws = jnp.concatenate([ck, kas], axis=1)[:, -keep:]
    vws = jnp.concatenate([cv, vas], axis=1)[:, -keep:]
    return (yp, ys,
            heads(kap[:, -keep:])[None], heads(vap[:, -keep:])[None],
            cp[None], np_[None], mp[:, 0, :M_HEADS][None],
            heads(kws)[None], heads(vws)[None],
            cs[None], ns[None], ms[:, 0, :M_HEADS][None])
```

```python
import functools

import jax
import jax.numpy as jnp
from jax import lax
from jax.experimental import pallas as pl
from jax.experimental.pallas import tpu as pltpu

F32 = jnp.float32
BF16 = jnp.bfloat16

D_MODEL = 4096
CHUNK = 64
N_Q_HEADS = 64
N_KV_HEADS = 8
HEAD_DIM = 64
GROUP = N_Q_HEADS // N_KV_HEADS
WINDOW = 128
ROT_DIM = HEAD_DIM // 4
ROPE_THETA = 500000.0
M_HEADS = 8
M_QK_DIM = 256
M_V_DIM = 512
ATT_WIDTH = N_Q_HEADS * HEAD_DIM
KV_WIDTH = N_KV_HEADS * HEAD_DIM
M_QK_WIDTH = M_HEADS * M_QK_DIM
M_V_WIDTH = M_HEADS * M_V_DIM
D_FF = 11008
DEPTH = 1
PAST_LEN = 2048
ALPHA = (2.0 * DEPTH) ** 0.25
LN_EPS = 1e-5
M_INIT = -1e30

LANES = 128
ROW_TILE = 1024
COL_TILE = 1024
ROW_BANDS = 4
ADA_TILE = 512
ATTN_ROWS = 4096
MLSTM_ROWS = 256
HEAD_UNROLL = 4
GATE_PAD = 2 * LANES
VMEM_LIMIT = 60 * 1024 * 1024


def _params(semantics):
    return pltpu.CompilerParams(dimension_semantics=semantics, vmem_limit_bytes=VMEM_LIMIT)


def _row_blocks(batch, steps, rows):
    if steps >= rows:
        assert steps % rows == 0
        return 1, rows
    bb = min(batch, rows // steps)
    assert batch % bb == 0
    return bb, steps


def _adaln_body(c_ref, w_ref, b_ref, o_ref):
    c = c_ref[...]
    s = c * jax.nn.sigmoid(c)
    o_ref[...] = jnp.dot(s, w_ref[...], preferred_element_type=F32) + b_ref[...]


def _adaln(c, w_ada, b_ada):
    rows, d = c.shape
    n = w_ada.shape[1]
    tn = ADA_TILE
    return pl.pallas_call(
        _adaln_body,
        grid=(n // tn,),
        in_specs=[pl.BlockSpec((rows, d), lambda j: (0, 0)),
                  pl.BlockSpec((d, tn), lambda j: (0, j)),
                  pl.BlockSpec((1, tn), lambda j: (0, j))],
        out_specs=pl.BlockSpec((rows, tn), lambda j: (0, j)),
        out_shape=jax.ShapeDtypeStruct((rows, n), F32),
        compiler_params=_params(("parallel",)),
        name="adaln",
    )(c, w_ada, b_ada.reshape(1, n))


def _load_rows(a_ref):
    bb, tt, k = a_ref.shape
    return a_ref[...].reshape(bb * tt, k)


def _bands(ref):
    bb, tt = ref.shape[0], ref.shape[1]
    if bb > 1:
        n = min(ROW_BANDS, bb)
        assert bb % n == 0
        return [(slice(p * (bb // n), (p + 1) * (bb // n)), slice(0, tt)) for p in range(n)]
    assert tt % ROW_BANDS == 0
    return [(slice(0, 1), slice(p * (tt // ROW_BANDS), (p + 1) * (tt // ROW_BANDS))) for p in range(ROW_BANDS)]


def _band(ref, bs, ts):
    if ref.shape[1] == 1:
        return ref[bs, :, :]
    blk = ref[bs, ts, :]
    return blk.reshape(blk.shape[0] * blk.shape[1], blk.shape[2])


def _store_band(ref, bs, ts, rows):
    nb = bs.stop - bs.start
    ref[bs, ts, :] = rows.reshape(nb, ts.stop - ts.start, rows.shape[-1]).astype(ref.dtype)


def _rope_rows(x, cos_ref, sa_ref, sb_ref, bs, ts):
    rows, n = x.shape
    nb = bs.stop - bs.start

    def table(ref):
        t = ref[ts, :]
        if nb == 1:
            return t
        return jnp.broadcast_to(t[None], (nb, t.shape[0], LANES)).reshape(rows, LANES)

    cos, sa, sb = table(cos_ref), table(sa_ref), table(sb_ref)
    outs = []
    for s in range(n // LANES):
        xs = x[:, s * LANES:(s + 1) * LANES]
        up = pltpu.roll(xs, LANES - ROT_DIM // 2, axis=1)
        dn = pltpu.roll(xs, ROT_DIM // 2, axis=1)
        outs.append(xs * cos + up * sa + dn * sb)
    return jnp.concatenate(outs, axis=1)


def _mm_plain_body(a_ref, w_ref, o_ref):
    acc = jnp.dot(_load_rows(a_ref), w_ref[...], preferred_element_type=F32)
    o_ref[...] = acc.reshape(o_ref.shape).astype(o_ref.dtype)


def _mm_rope_body(a_ref, w_ref, cos_ref, sa_ref, sb_ref, o_ref):
    for bs, ts in _bands(a_ref):
        acc = jnp.dot(_band(a_ref, bs, ts), w_ref[...], preferred_element_type=F32)
        _store_band(o_ref, bs, ts, _rope_rows(acc, cos_ref, sa_ref, sb_ref, bs, ts))


def _mm_mod_kvg_body(x_ref, w_ref, wg_ref, sc_ref, sh_ref, cos_ref, sa_ref, sb_ref, u_ref, k_ref, v_ref, g_ref):
    u_ref[...] = (x_ref[...] * (1.0 + sc_ref[...]) + sh_ref[...]).astype(u_ref.dtype)
    for bs, ts in _bands(x_ref):
        a = _band(u_ref, bs, ts)
        acc = jnp.dot(a, w_ref[...], preferred_element_type=F32)
        _store_band(k_ref, bs, ts, _rope_rows(acc[:, :KV_WIDTH], cos_ref, sa_ref, sb_ref, bs, ts))
        _store_band(v_ref, bs, ts, acc[:, KV_WIDTH:])
        _store_band(g_ref, bs, ts, jnp.dot(a, wg_ref[...], preferred_element_type=F32))


def _gated(acc, g_ref, bs, ts):
    return jax.nn.sigmoid(_band(g_ref, bs, ts).astype(F32)) * acc


def _mm_gate_first_body(a_ref, w_ref, g_ref, o_ref):
    for bs, ts in _bands(a_ref):
        acc = jnp.dot(_band(a_ref, bs, ts), w_ref[...], preferred_element_type=F32)
        _store_band(o_ref, bs, ts, _gated(acc, g_ref, bs, ts))


def _mm_gate_second_body(a_ref, w_ref, g_ref, t_ref, o_ref):
    for bs, ts in _bands(a_ref):
        acc = jnp.dot(_band(a_ref, bs, ts), w_ref[...], preferred_element_type=F32)
        _store_band(o_ref, bs, ts, _band(t_ref, bs, ts) + _gated(acc, g_ref, bs, ts))


def _mm_residual_body(a_ref, w_ref, x_ref, g_ref, o_ref):
    acc = jnp.dot(_load_rows(a_ref), w_ref[...], preferred_element_type=F32)
    o_ref[...] = ALPHA * x_ref[...] + g_ref[...] * acc.reshape(o_ref.shape)


def _mm_swiglu_body(a_ref, wg_ref, wu_ref, o_ref):
    for bs, ts in _bands(a_ref):
        a = _band(a_ref, bs, ts)
        gate = jnp.dot(a, wg_ref[...], preferred_element_type=F32)
        up = jnp.dot(a, wu_ref[...], preferred_element_type=F32)
        _store_band(o_ref, bs, ts, gate * jax.nn.sigmoid(gate) * up)


def _act_spec(bb, tt, width, col=None):
    if col is None:
        return pl.BlockSpec((bb, tt, width), lambda b, t, j: (b, t, j))
    return pl.BlockSpec((bb, tt, width), lambda b, t, j: (b, t, j + col))


def _matmul(body, a, weights, extras, extra_specs, outs, *, n_tiles, rows=ROW_TILE, name):
    batch, steps, k = a.shape
    bb, tt = _row_blocks(batch, steps, rows)
    grid = (batch // bb, steps // tt, n_tiles)
    in_specs = [pl.BlockSpec((bb, tt, k), lambda b, t, j: (b, t, 0))]
    in_specs += [pl.BlockSpec((k, cols), functools.partial(lambda b, t, j, col: (0, col(j)), col=col))
                 for _, cols, col in weights]
    in_specs += [spec(bb, tt) for spec in extra_specs]
    weights = [w for w, _, _ in weights]
    out_specs = [pl.BlockSpec((bb, tt, cols_tile), lambda b, t, j: (b, t, j)) for _, _, cols_tile in outs]
    out_shape = [jax.ShapeDtypeStruct((batch, steps, cols), dt) for cols, dt, _ in outs]
    single = len(outs) == 1
    return pl.pallas_call(
        body,
        grid=grid,
        in_specs=in_specs,
        out_specs=out_specs[0] if single else out_specs,
        out_shape=out_shape[0] if single else out_shape,
        compiler_params=_params(("parallel", "parallel", "parallel")),
        name=name,
    )(a, *weights, *extras)


def _matmul_casting(body, a, casts, extras, extra_specs, outs, *, n_tiles, name, n_side=0):
    batch, steps, k = a.shape

    def per_tile(spec):
        return pl.BlockSpec(spec.block_shape, lambda j: spec.index_map(0, 0, j))

    in_specs = [pl.BlockSpec((batch, steps, k), lambda j: (0, 0, 0), pipeline_mode=pl.Buffered(1))]
    sources = []
    for srcs, *_ in casts:
        for arr, shape, index in srcs:
            in_specs.append(pl.BlockSpec(shape, index))
            sources.append(arr)
    in_specs += [per_tile(spec(batch, steps)) for spec in extra_specs]
    out_specs = [pl.BlockSpec((batch, steps, cols_tile), lambda j: (0, 0, j)) for _, _, cols_tile in outs]
    out_shape = [jax.ShapeDtypeStruct((batch, steps, cols), dt) for cols, dt, _ in outs]
    out_specs += [pl.BlockSpec(block, index) for _, _, _, block, index in casts]
    out_shape += [jax.ShapeDtypeStruct(shape, BF16) for _, _, shape, _, _ in casts]
    counts = [len(srcs) for srcs, *_ in casts]
    transforms = [tr for _, tr, *_ in casts]
    n_used = len(casts) - n_side
    n_extra, n_out = len(extras), len(outs)

    def casting_body(a_ref, *refs):
        src_refs, pos = [], 0
        for n in counts:
            src_refs.append(refs[pos:pos + n])
            pos += n
        extra_refs = refs[pos:pos + n_extra]
        out_refs = refs[pos + n_extra:pos + n_extra + n_out]
        w_refs = refs[pos + n_extra + n_out:]
        for w_ref, srcs, transform in zip(w_refs, src_refs, transforms):
            w_ref[...] = transform(*[s[...] for s in srcs]).astype(w_ref.dtype)
        body(a_ref, *w_refs[:n_used], *extra_refs, *out_refs)

    return pl.pallas_call(
        casting_body,
        grid=(n_tiles,),
        in_specs=in_specs,
        out_specs=out_specs,
        out_shape=out_shape,
        compiler_params=_params(("arbitrary",)),
        name=name,
    )(a, *sources, *extras)


def _rope_specs():
    return [lambda bb, tt: pl.BlockSpec((tt, LANES), lambda b, t, j: (t, 0))] * 3


def _attn_body(q_ref, kc_ref, kp_ref, vc_ref, vp_ref, sink_ref, o_ref, *, mask_first):
    bb, tq, _ = q_ref.shape
    n_chunks = tq // CHUNK
    n_keys = WINDOW + CHUNK
    i = pl.program_id(1)
    h = pl.program_id(2)

    lane = lax.broadcasted_iota(jnp.int32, (1, LANES), 1)
    keep = (lane // HEAD_DIM) == (h % 2)
    sink = sink_ref[0]
    lane_head = lax.broadcasted_iota(jnp.int32, (CHUNK, 2 * LANES), 1) // HEAD_DIM
    out_low = lax.broadcasted_iota(jnp.int32, (CHUNK, LANES), 1) < HEAD_DIM
    key_row = lax.broadcasted_iota(jnp.int32, (n_keys, 1), 0)
    ones = jnp.ones((16, n_keys), BF16)

    def both_halves(cur_ref, prev_ref, bi):
        pair = jnp.concatenate([prev_ref[bi].astype(F32), cur_ref[bi].astype(F32)], axis=0)
        swapped = pltpu.roll(pair, HEAD_DIM, axis=1)
        return jnp.where(keep, pair, swapped)

    def scores(bi, j, k4):
        q = q_ref[bi, j * CHUNK:(j + 1) * CHUNK, :] * (HEAD_DIM ** -0.5)
        parts = []
        for cg in range(GROUP // 4):
            qc = q[:, cg * 2 * LANES:(cg + 1) * 2 * LANES]
            for r in range(4):
                parts.append(jnp.where(lane_head == r, qc, jnp.zeros_like(qc)))
        qrows = jnp.concatenate(parts, axis=0)
        keys = k4[j * CHUNK:j * CHUNK + n_keys]
        st = lax.dot_general(keys, qrows, (((1,), (1,)), ((), ())), preferred_element_type=F32)
        if mask_first and j < WINDOW // CHUNK:
            first_valid = WINDOW - (i * tq + j * CHUNK)
            st = jnp.where(key_row >= first_valid, st, -jnp.inf)
        return st

    def values_t(j, v2):
        vwin = v2[j * CHUNK:j * CHUNK + n_keys]
        head = vwin[:LANES].T
        tail = vwin[LANES:]
        tail = jnp.concatenate([tail, tail], axis=0).T
        vt = jnp.concatenate([head[:HEAD_DIM], tail[:HEAD_DIM, :CHUNK]], axis=1).astype(BF16)
        return jnp.concatenate([vt, ones], axis=0)

    def finish(st, lhs):
        m = jnp.maximum(jnp.max(st, axis=0, keepdims=True), sink)
        pt = jnp.exp(st - m).astype(BF16)
        ot = jnp.dot(lhs, pt, preferred_element_type=F32)
        den = ot[HEAD_DIM:HEAD_DIM + 1] + jnp.exp(sink - m)
        return ot[:HEAD_DIM] / den

    def emit(bi, j, on):
        for pr in range(GROUP // 2):
            x = on[:, pr * LANES:(pr + 1) * LANES]
            xt = jnp.concatenate([x, x], axis=0).T
            o_ref[bi, j * CHUNK:(j + 1) * CHUNK, pr * LANES:(pr + 1) * LANES] = (
                jnp.where(out_low, xt[:CHUNK], xt[CHUNK:]).astype(o_ref.dtype))

    units, sts, lhss = [], [], []
    for bi in range(bb):
        k2 = both_halves(kc_ref, kp_ref, bi)
        v2 = both_halves(vc_ref, vp_ref, bi)
        k4 = jnp.concatenate([k2, k2], axis=1).astype(BF16)
        for j in range(n_chunks):
            units.append((bi, j))
            sts.append(scores(bi, j, k4))
            lhss.append(values_t(j, v2))
    for (bi, j), st, lhs in zip(units, sts, lhss):
        emit(bi, j, finish(st, lhs))


def _attention(q, k_cur, v_cur, k_prev, v_prev, sink_tab, *, bb, tq, mask_first):
    batch, steps, _ = q.shape
    qw = GROUP * HEAD_DIM
    same_array = k_prev is k_cur
    per_tile = tq // WINDOW

    def prev_map(b, i, h):
        if same_array:
            return (b, jnp.maximum(i * per_tile - 1, 0), h // 2)
        return (b, 0, h // 2)

    return pl.pallas_call(
        functools.partial(_attn_body, mask_first=mask_first),
        grid=(batch // bb, steps // tq, N_KV_HEADS),
        in_specs=[pl.BlockSpec((bb, tq, qw), lambda b, i, h: (b, i, h)),
                  pl.BlockSpec((bb, tq, LANES), lambda b, i, h: (b, i, h // 2)),
                  pl.BlockSpec((bb, WINDOW, LANES), prev_map),
                  pl.BlockSpec((bb, tq, LANES), lambda b, i, h: (b, i, h // 2)),
                  pl.BlockSpec((bb, WINDOW, LANES), prev_map),
                  pl.BlockSpec((1, 1, GROUP * CHUNK), lambda b, i, h: (h, 0, 0))],
        out_specs=pl.BlockSpec((bb, tq, qw), lambda b, i, h: (b, i, h)),
        out_shape=jax.ShapeDtypeStruct((batch, steps, ATT_WIDTH), BF16),
        compiler_params=_params(("parallel", "parallel", "parallel")),
        name="attention",
    )(q, k_cur, k_prev, v_cur, v_prev, sink_tab)


def _log_sigmoid(x):
    return jnp.minimum(x, 0.0) - jnp.log1p(jnp.exp(-jnp.abs(x)))


def _mlstm_body(q_ref, k_ref, v_ref, o_ref, g_ref, bias_ref, c0_ref, n0_ref, m0_ref,
                y_ref, c_ref, n_ref, m_ref, *, single_step):
    if single_step:
        c_in, n_in, m_in = c0_ref, n0_ref, m0_ref
    else:
        c_in, n_in, m_in = c_ref, n_ref, m_ref

        @pl.when(pl.program_id(1) == 0)
        def _():
            c_ref[...] = c0_ref[...]
            n_ref[...] = n0_ref[...]
            m_ref[...] = m0_ref[...]

    steps = q_ref.shape[1]
    g = g_ref[0] + bias_ref[...]
    li = g[:, :LANES]
    lf = _log_sigmoid(g[:, LANES:])
    row = lax.broadcasted_iota(jnp.int32, (steps, steps), 0)
    col = lax.broadcasted_iota(jnp.int32, (steps, steps), 1)
    causal = col <= row
    b = jnp.dot(causal.astype(F32), lf, preferred_element_type=F32,
                precision=lax.Precision.HIGHEST)
    r = li - b
    m_prev = m_in[0]
    b_last = b[steps - 1:steps, :]
    m_new = b_last + jnp.maximum(m_prev, jnp.max(r, axis=0, keepdims=True))
    w_state = jnp.exp(r + (b_last - m_new))
    decay = jnp.exp(b_last + m_prev - m_new)
    r_rows = r.T

    lane = lax.broadcasted_iota(jnp.int32, (1, LANES), 1)
    head_row = lax.broadcasted_iota(jnp.int32, (LANES, 1), 0)

    def one_head(h, carry):
        pick = lane == h

        def column(mat):
            return jnp.sum(jnp.where(pick, mat, 0.0), axis=1, keepdims=True)

        r_row = jnp.sum(jnp.where(head_row == h, r_rows, 0.0), axis=0, keepdims=True)
        dmat = jnp.where(causal, jnp.broadcast_to(r_row, (steps, steps)), -jnp.inf)
        m_h = column(m_prev)
        gmax = jnp.maximum(jnp.max(dmat, axis=-1, keepdims=True), m_h)
        w_intra = jnp.exp(dmat - gmax)
        w_inter = jnp.exp(m_h - gmax)
        inv_floor = jnp.exp(-(column(b) + gmax))

        qk_lanes = pl.ds(pl.multiple_of(h * M_QK_DIM, M_QK_DIM), M_QK_DIM)
        v_lanes = pl.ds(pl.multiple_of(h * M_V_DIM, M_V_DIM), M_V_DIM)
        q = q_ref[0, :, qk_lanes] * (M_QK_DIM ** -0.5)
        k = k_ref[0, :, qk_lanes]
        v = v_ref[0, :, v_lanes]
        c_old = c_in[0, h]
        n_old = n_in[0, pl.ds(h, 1), :]

        qk = lax.dot_general(q, k, (((1,), (1,)), ((), ())), preferred_element_type=F32)
        s = qk * w_intra
        intra = jnp.dot(s.astype(BF16), v, preferred_element_type=F32)
        inter = lax.dot_general(q, c_old.astype(BF16), (((1,), (1,)), ((), ())),
                                preferred_element_type=F32)
        nq = jnp.sum(q.astype(F32) * n_old, axis=-1, keepdims=True)
        num = intra + w_inter * inter
        den = jnp.sum(s, axis=-1, keepdims=True) + w_inter * nq
        hval = num / jnp.maximum(jnp.abs(den), inv_floor)
        og = jax.nn.sigmoid(o_ref[0, :, v_lanes].astype(F32))
        y_ref[0, :, v_lanes] = (og * hval).astype(y_ref.dtype)

        ws = column(w_state)
        vw = (v.astype(F32) * ws).astype(BF16)
        dc = lax.dot_general(vw, k, (((0,), (0,)), ((), ())), preferred_element_type=F32)
        dec = column(decay)
        c_ref[0, h] = dec * c_old + dc
        n_ref[0, pl.ds(h, 1), :] = dec * n_old + jnp.sum(k.astype(F32) * ws, axis=0, keepdims=True)
        return carry

    lax.fori_loop(0, M_HEADS, one_head, 0, unroll=HEAD_UNROLL)

    m_ref[0] = m_new


def _mlstm(za, zb, gates, bias, c0, n0, m0):
    batch, steps, _ = za.shape
    rows = min(steps, MLSTM_ROWS)
    state_specs = [pl.BlockSpec((1, M_HEADS, M_V_DIM, M_QK_DIM), lambda b, c: (b, 0, 0, 0)),
                   pl.BlockSpec((1, M_HEADS, M_QK_DIM), lambda b, c: (b, 0, 0)),
                   pl.BlockSpec((1, 1, LANES), lambda b, c: (b, 0, 0))]
    return pl.pallas_call(
        functools.partial(_mlstm_body, single_step=steps == rows),
        grid=(batch, steps // rows),
        in_specs=[pl.BlockSpec((1, rows, M_QK_WIDTH), lambda b, c: (b, c, 0)),
                  pl.BlockSpec((1, rows, M_QK_WIDTH), lambda b, c: (b, c, 1)),
                  pl.BlockSpec((1, rows, M_V_WIDTH), lambda b, c: (b, c, 1)),
                  pl.BlockSpec((1, rows, M_V_WIDTH), lambda b, c: (b, c, 0)),
                  pl.BlockSpec((1, rows, GATE_PAD), lambda b, c: (b, c, 0)),
                  pl.BlockSpec((1, GATE_PAD), lambda b, c: (0, 0))] + state_specs,
        out_specs=[pl.BlockSpec((1, rows, M_V_WIDTH), lambda b, c: (b, c, 0))] + state_specs,
        out_shape=[jax.ShapeDtypeStruct((batch, steps, M_V_WIDTH), BF16),
                   jax.ShapeDtypeStruct(c0.shape, F32),
                   jax.ShapeDtypeStruct(n0.shape, F32),
                   jax.ShapeDtypeStruct(m0.shape, F32)],
        compiler_params=_params(("parallel", "arbitrary")),
        name="mlstm",
    )(za, za, za, zb, gates, bias, c0, n0, m0)


def _ln_rows(x, g, b):
    mu = jnp.mean(x, axis=-1, keepdims=True)
    xc = x - mu
    var = jnp.mean(xc * xc, axis=-1, keepdims=True)
    return xc * lax.rsqrt(var + LN_EPS) * g + b


def _ln_mod_body(r_ref, g_ref, b_ref, sc_ref, sh_ref, x_ref, u_ref):
    y = _ln_rows(r_ref[...], g_ref[...], b_ref[...])
    x_ref[...] = y
    u_ref[...] = (y * (1.0 + sc_ref[...]) + sh_ref[...]).astype(u_ref.dtype)


def _ln_body(r_ref, g_ref, b_ref, x_ref):
    x_ref[...] = _ln_rows(r_ref[...], g_ref[...], b_ref[...])


def _layer_norm(r, g, b, mod=None, sc_idx=None, sh_idx=None):
    batch, steps, d = r.shape
    bb, tt = _row_blocks(batch, steps, ROW_TILE // 2)
    act = pl.BlockSpec((bb, tt, d), lambda i, t: (i, t, 0))
    vec = pl.BlockSpec((1, 1, d), lambda i, t: (0, 0, 0))
    g3, b3 = g.reshape(1, 1, d), b.reshape(1, 1, d)
    if mod is None:
        return pl.pallas_call(
            _ln_body, grid=(batch // bb, steps // tt),
            in_specs=[act, vec, vec], out_specs=act,
            out_shape=jax.ShapeDtypeStruct(r.shape, F32),
            compiler_params=_params(("parallel", "parallel")), name="layer_norm",
        )(r, g3, b3)
    return pl.pallas_call(
        _ln_mod_body, grid=(batch // bb, steps // tt),
        in_specs=[act, vec, vec,
                  pl.BlockSpec((bb, 1, d), lambda i, t: (i, 0, sc_idx)),
                  pl.BlockSpec((bb, 1, d), lambda i, t: (i, 0, sh_idx))],
        out_specs=[act, act],
        out_shape=[jax.ShapeDtypeStruct(r.shape, F32), jax.ShapeDtypeStruct(r.shape, BF16)],
        compiler_params=_params(("parallel", "parallel")), name="layer_norm_modulate",
    )(r, g3, b3, mod, mod)


SH1, SC1, GT1, SH2, SC2, GT2 = range(6)


def _rope_tables(pos):
    half = ROT_DIM // 2
    inv = ROPE_THETA ** (-2.0 * jnp.arange(half, dtype=F32) / ROT_DIM)
    ang = pos.astype(F32)[:, None] * inv[None, :]
    cos, sin = jnp.cos(ang), jnp.sin(ang)
    zeros = jnp.zeros_like(cos)
    rest = HEAD_DIM - ROT_DIM
    steps = pos.shape[0]
    one_head = lambda first, second, fill: jnp.concatenate(
        [first, second, jnp.full((steps, rest), fill, F32)], axis=1)
    cos_t = one_head(cos, cos, 1.0)
    up_t = one_head(-sin, zeros, 0.0)
    dn_t = one_head(zeros, sin, 0.0)
    reps = LANES // HEAD_DIM
    return tuple(jnp.tile(t, (1, reps)) for t in (cos_t, up_t, dn_t))


CAST_TILE = 512
FFN_TILE = 256
MERGE_CAST_TILE = 256
GATE_COLS = 2 * M_HEADS
PROJ_HEAD = ATT_WIDTH + 2 * KV_WIDTH + 2 * M_QK_WIDTH + M_V_WIDTH


def _group_layer(x, mod, pos, wts, raw, k_prev, v_prev, state, *, attn_bb, attn_tq, mask_first):
    batch, steps, d = x.shape
    ropes = _rope_tables(pos)
    casting = raw is not None
    tile = CAST_TILE if casting else COL_TILE

    def dense(name, body, act, keys, casts, extras, extra_specs, outs, n_cols, wtile, side_keys=(), rows=ROW_TILE):
        n_tiles = -(-n_cols // wtile)
        if not casting:
            weights = [(wts[key], wtile, lambda j: j) for key in keys]
            return _matmul(body, act, weights, extras, extra_specs, outs, n_tiles=n_tiles, rows=rows, name=name)
        res = _matmul_casting(body, act, casts, extras, extra_specs, outs, n_tiles=n_tiles, name=name,
                              n_side=len(side_keys))
        for key, w in zip(list(keys) + list(side_keys), res[len(outs):]):
            wts[key] = w
        return res[0] if len(outs) == 1 else res[:len(outs)]

    def proj_rows(first, total):
        srcs = [(raw["w_in_t"], (1, tile, d), lambda j: (0, first // tile + j, 0))]
        return [(srcs, lambda w: w[0].T, (d, total), (d, tile), lambda j: (0, j))]

    def proj_rows_shifted(first, total):
        base = first - GATE_COLS
        srcs = [(raw["w_in_t"], (1, tile, d), lambda j: (0, base // tile + j, 0)),
                (raw["w_in_t"], (1, GATE_COLS, d), lambda j: (0, (base + tile * (j + 1)) // GATE_COLS, 0))]

        def transform(main, nxt):
            rows = jnp.concatenate([main[0], nxt[0]], axis=0)
            return rows[GATE_COLS:GATE_COLS + tile].T

        return [(srcs, transform, (d, total), (d, tile), lambda j: (0, j))]

    def columns(key, first, total, width):
        srcs = [(raw[key], (1, d, width), lambda j: (0, 0, first // width + j))]
        return (srcs, lambda w: w[0], (d, total), (d, width), lambda j: (0, j))

    def mod_full(idx):
        return lambda bb, tt: pl.BlockSpec((bb, 1, d), lambda b, t_, j: (b, 0, idx))

    u, ka, va, gates = _matmul(
        _mm_mod_kvg_body, x, [(wts["w_kv"], 2 * KV_WIDTH, lambda j: 0), (wts["w_g"], GATE_PAD, lambda j: 0)],
        [mod, mod, *ropes], [mod_full(SC1), mod_full(SH1)] + _rope_specs(),
        [(d, BF16, d), (KV_WIDTH, F32, KV_WIDTH), (KV_WIDTH, F32, KV_WIDTH), (GATE_PAD, F32, GATE_PAD)],
        n_tiles=1, rows=ROW_TILE // 2, name="proj_kvg")

    mem_first = ATT_WIDTH + 2 * KV_WIDTH
    za_width = 2 * M_QK_WIDTH + M_V_WIDTH
    zb_width = M_V_WIDTH + 2 * d
    qa = dense("proj_q", _mm_rope_body, u, ["w_qa"], casting and proj_rows(0, ATT_WIDTH),
               ropes, _rope_specs(), [(ATT_WIDTH, BF16, tile)], ATT_WIDTH, tile)
    za = dense("proj_mem", _mm_plain_body, u, ["w_mem"], casting and proj_rows(mem_first, za_width),
               [], [], [(za_width, BF16, tile)], za_width, tile)
    zb = dense("proj_gates", _mm_plain_body, u, ["w_gates"],
               casting and proj_rows_shifted(PROJ_HEAD + GATE_COLS, zb_width),
               [], [], [(zb_width, BF16, tile)], zb_width, tile)

    if k_prev is None:
        k_prev, v_prev = ka, va
    ya = _attention(qa, ka, va, k_prev, v_prev, wts["sink_tab"], bb=attn_bb, tq=attn_tq, mask_first=mask_first)

    yb, c_new, n_new, m_new = _mlstm(za, zb, gates, wts["gate_bias"], *state)

    mtile = MERGE_CAST_TILE if casting else COL_TILE
    ga_col = M_V_WIDTH // mtile
    gb_col = ga_col + d // mtile
    t = dense("merge_a", _mm_gate_first_body, ya, ["w_up_a"], casting and [columns("w_up_a", 0, d, mtile)],
              [zb], [lambda bb, tt: _act_spec(bb, tt, mtile, ga_col)], [(d, F32, mtile)], d, mtile)
    merged = dense("merge_b", _mm_gate_second_body, yb, ["w_up_b"], casting and [columns("w_up_b", 0, d, mtile)],
                   [zb, t], [lambda bb, tt: _act_spec(bb, tt, mtile, gb_col),
                             lambda bb, tt: _act_spec(bb, tt, mtile)], [(d, BF16, mtile)], d, mtile)

    def mod_spec(idx, width):
        per_vec = d // width
        return lambda bb, tt: pl.BlockSpec((bb, 1, width), lambda b, t_, j: (b, 0, idx * per_vec + j))

    r1 = dense("out_proj", _mm_residual_body, merged, ["w_o"], casting and [columns("w_o", 0, d, mtile)],
               [x, mod], [lambda bb, tt: _act_spec(bb, tt, mtile), mod_spec(GT1, mtile)],
               [(d, F32, mtile)], d, mtile)
    x1, u2 = _layer_norm(r1, wts["ln1_g"], wts["ln1_b"], mod, SC2, SH2)

    ff_tile = FFN_TILE
    ffn_casts = casting and [
        columns("w_ffn_in", 0, D_FF, ff_tile), columns("w_ffn_in", D_FF, D_FF, ff_tile),
        ([(raw["w_ffn_out"], (1, ff_tile, d), lambda j: (0, j, 0))], lambda w: w[0],
         (D_FF, d), (ff_tile, d), lambda j: (j, 0))]
    hidden = dense("ffn_in", _mm_swiglu_body, u2, ["w_ffn_gate", "w_ffn_up"], ffn_casts,
                   [], [], [(D_FF, BF16, ff_tile)], D_FF, ff_tile, side_keys=["w_ffn_out"] if casting else (),
                   rows=2 * ROW_TILE)

    out_tile = COL_TILE // 2
    r2 = _matmul(_mm_residual_body, hidden, [(wts["w_ffn_out"], out_tile, lambda j: j)], [x1, mod],
                 [lambda bb, tt: _act_spec(bb, tt, out_tile), mod_spec(GT2, out_tile)],
                 [(d, F32, out_tile)], n_tiles=d // out_tile, rows=ROW_TILE // 2, name="ffn_out")
    y = _layer_norm(r2, wts["ln2_g"], wts["ln2_b"])
    return y, ka, va, c_new, n_new, m_new


def _cast_body(w_ref, o_ref):
    o_ref[...] = w_ref[0].T.astype(o_ref.dtype)


def _cast_gates_body(w_ref, o_ref):
    x = w_ref[0].T
    valid = lax.broadcasted_iota(jnp.int32, (1, LANES), 1) < M_HEADS
    o_ref[:, :LANES] = jnp.where(valid, x, 0.0).astype(o_ref.dtype)
    f_first = pltpu.roll(x, LANES - M_HEADS, axis=1)
    o_ref[:, LANES:] = jnp.where(valid, f_first, 0.0).astype(o_ref.dtype)


def _cast_kv_gates(w_in_t):
    _, _, k = w_in_t.shape
    params = _params(("parallel",))
    first = ATT_WIDTH // CAST_TILE
    w_kv = pl.pallas_call(
        _cast_body, grid=(2 * KV_WIDTH // CAST_TILE,),
        in_specs=[pl.BlockSpec((1, CAST_TILE, k), lambda j: (0, first + j, 0))],
        out_specs=pl.BlockSpec((k, CAST_TILE), lambda j: (0, j)),
        out_shape=jax.ShapeDtypeStruct((k, 2 * KV_WIDTH), BF16),
        compiler_params=params, name="cast_proj_kv",
    )(w_in_t)
    w_g = pl.pallas_call(
        _cast_gates_body, grid=(1,),
        in_specs=[pl.BlockSpec((1, LANES, k), lambda j: (0, PROJ_HEAD // LANES, 0))],
        out_specs=pl.BlockSpec((k, GATE_PAD), lambda j: (0, 0)),
        out_shape=jax.ShapeDtypeStruct((k, GATE_PAD), BF16),
        compiler_params=params, name="cast_proj_gates",
    )(w_in_t)
    return w_kv, w_g


def kernel(x_prompt, x_sample, cache_k_win, cache_v_win, state_C, state_n, state_m, c_prompt, c_sample, w_ada, b_ada, w_in, b_if, attn_sinks, w_up_a, w_up_b, w_o, ln1_g, ln1_b, w_ffn_in, w_ffn_out, ln2_g, ln2_b):
    bp, sp, d = x_prompt.shape
    bs, ts, _ = x_sample.shape
    keep = cache_k_win.shape[2]
    assert w_in.shape[0] == DEPTH == 1
    l = 0

    raw = {"w_in_t": jnp.swapaxes(w_in, 1, 2), "w_up_a": w_up_a, "w_up_b": w_up_b, "w_o": w_o,
           "w_ffn_in": w_ffn_in, "w_ffn_out": w_ffn_out}
    w_kv, w_g = _cast_kv_gates(raw["w_in_t"])
    pad_bias = jnp.zeros((LANES - M_HEADS,), F32)
    wts = {
        "w_kv": w_kv, "w_g": w_g,
        "gate_bias": jnp.concatenate([b_if[l, :M_HEADS], pad_bias, b_if[l, M_HEADS:], pad_bias]).reshape(1, GATE_PAD),
        "sink_tab": jnp.broadcast_to(attn_sinks[l].astype(F32).reshape(N_KV_HEADS, 1, GROUP, 1),
                                     (N_KV_HEADS, 1, GROUP, CHUNK)).reshape(N_KV_HEADS, 1, GROUP * CHUNK),
        "ln1_g": ln1_g[l], "ln1_b": ln1_b[l], "ln2_g": ln2_g[l], "ln2_b": ln2_b[l],
    }

    c_all = jnp.concatenate([c_prompt, c_sample], axis=0)
    rows = c_all.shape[0]
    rows_pad = -(-rows // 8) * 8
    c_all = jnp.pad(c_all, ((0, rows_pad - rows), (0, 0)))
    mod = _adaln(c_all, w_ada[l], b_ada[l])
    mod_p = mod[:bp].reshape(bp, 1, 6 * d)
    mod_s = mod[bp:bp + bs].reshape(bs, 1, 6 * d)

    def pad_lanes(m):
        return jnp.pad(m, ((0, 0), (0, LANES - M_HEADS))).reshape(m.shape[0], 1, LANES)

    ck = cache_k_win[l].reshape(bs, keep, KV_WIDTH)
    cv = cache_v_win[l].reshape(bs, keep, KV_WIDTH)
    state_s = (state_C[l], state_n[l], pad_lanes(state_m[l]))
    ys, kas, vas, cs, ns, ms = _group_layer(
        x_sample, mod_s, PAST_LEN + jnp.arange(ts), wts, raw, ck, cv, state_s,
        attn_bb=min(bs, ATTN_ROWS // ts), attn_tq=ts, mask_first=False)

    state_p = (jnp.zeros((bp, M_HEADS, M_V_DIM, M_QK_DIM), F32),
               jnp.zeros((bp, M_HEADS, M_QK_DIM), F32),
               pad_lanes(jnp.full((bp, M_HEADS), M_INIT, F32)))
    yp, kap, vap, cp, np_, mp = _group_layer(
        x_prompt, mod_p, jnp.arange(sp), wts, None, None, None, state_p,
        attn_bb=1, attn_tq=min(ATTN_ROWS, sp), mask_first=True)

    def heads(a):
        return a.reshape(a.shape[0], a.shape[1], N_KV_HEADS, HEAD_DIM)

    kws = jnp.concatenate([ck, kas], axis=1)[:, -keep:]
    vws = jnp.concatenate([cv, vas], axis=1)[:, -keep:]
    return (yp, ys,
            heads(kap[:, -keep:])[None], heads(vap[:, -keep:])[None],
            cp[None], np_[None], mp[:, 0, :M_HEADS][None],
            heads(kws)[None], heads(vws)[None],
            cs[None], ns[None], ms[:, 0, :M_HEADS][None])
```

```python
import functools

import jax
import jax.numpy as jnp
from jax import lax
from jax.experimental import pallas as pl
from jax.experimental.pallas import tpu as pltpu

F32 = jnp.float32
BF16 = jnp.bfloat16

D_MODEL = 4096
CHUNK = 64
N_Q_HEADS = 64
N_KV_HEADS = 8
HEAD_DIM = 64
GROUP = N_Q_HEADS // N_KV_HEADS
WINDOW = 128
ROT_DIM = HEAD_DIM // 4
ROPE_THETA = 500000.0
M_HEADS = 8
M_QK_DIM = 256
M_V_DIM = 512
ATT_WIDTH = N_Q_HEADS * HEAD_DIM
KV_WIDTH = N_KV_HEADS * HEAD_DIM
M_QK_WIDTH = M_HEADS * M_QK_DIM
M_V_WIDTH = M_HEADS * M_V_DIM
D_FF = 11008
DEPTH = 1
PAST_LEN = 2048
ALPHA = (2.0 * DEPTH) ** 0.25
LN_EPS = 1e-5
M_INIT = -1e30

LANES = 128
ROW_TILE = 1024
COL_TILE = 1024
ROW_BANDS = 4
ADA_TILE = 512
ATTN_ROWS = 4096
ATTN_UNROLL = 2
MLSTM_ROWS = 256
HEAD_UNROLL = 4
GATE_PAD = 2 * LANES
VMEM_LIMIT = 60 * 1024 * 1024


def _params(semantics):
    return pltpu.CompilerParams(dimension_semantics=semantics, vmem_limit_bytes=VMEM_LIMIT)


def _row_blocks(batch, steps, rows):
    if steps >= rows:
        assert steps % rows == 0
        return 1, rows
    bb = min(batch, rows // steps)
    assert batch % bb == 0
    return bb, steps


def _adaln_body(c_ref, w_ref, b_ref, o_ref):
    c = c_ref[...]
    s = c * jax.nn.sigmoid(c)
    o_ref[...] = jnp.dot(s, w_ref[...], preferred_element_type=F32) + b_ref[...]


def _adaln(c, w_ada, b_ada):
    rows, d = c.shape
    n = w_ada.shape[1]
    tn = ADA_TILE
    return pl.pallas_call(
        _adaln_body,
        grid=(n // tn,),
        in_specs=[pl.BlockSpec((rows, d), lambda j: (0, 0)),
                  pl.BlockSpec((d, tn), lambda j: (0, j)),
                  pl.BlockSpec((1, tn), lambda j: (0, j))],
        out_specs=pl.BlockSpec((rows, tn), lambda j: (0, j)),
        out_shape=jax.ShapeDtypeStruct((rows, n), F32),
        compiler_params=_params(("parallel",)),
        name="adaln",
    )(c, w_ada, b_ada.reshape(1, n))


def _load_rows(a_ref):
    bb, tt, k = a_ref.shape
    return a_ref[...].reshape(bb * tt, k)


def _bands(ref):
    bb, tt = ref.shape[0], ref.shape[1]
    if bb > 1:
        n = min(ROW_BANDS, bb)
        assert bb % n == 0
        return [(slice(p * (bb // n), (p + 1) * (bb // n)), slice(0, tt)) for p in range(n)]
    assert tt % ROW_BANDS == 0
    return [(slice(0, 1), slice(p * (tt // ROW_BANDS), (p + 1) * (tt // ROW_BANDS))) for p in range(ROW_BANDS)]


def _band(ref, bs, ts):
    if ref.shape[1] == 1:
        return ref[bs, :, :]
    blk = ref[bs, ts, :]
    return blk.reshape(blk.shape[0] * blk.shape[1], blk.shape[2])


def _store_band(ref, bs, ts, rows):
    nb = bs.stop - bs.start
    ref[bs, ts, :] = rows.reshape(nb, ts.stop - ts.start, rows.shape[-1]).astype(ref.dtype)


def _rope_rows(x, cos_ref, sa_ref, sb_ref, bs, ts):
    rows, n = x.shape
    nb = bs.stop - bs.start

    def table(ref):
        t = ref[ts, :]
        if nb == 1:
            return t
        return jnp.broadcast_to(t[None], (nb, t.shape[0], LANES)).reshape(rows, LANES)

    cos, sa, sb = table(cos_ref), table(sa_ref), table(sb_ref)
    outs = []
    for s in range(n // LANES):
        xs = x[:, s * LANES:(s + 1) * LANES]
        up = pltpu.roll(xs, LANES - ROT_DIM // 2, axis=1)
        dn = pltpu.roll(xs, ROT_DIM // 2, axis=1)
        outs.append(xs * cos + up * sa + dn * sb)
    return jnp.concatenate(outs, axis=1)


def _mm_plain_body(a_ref, w_ref, o_ref):
    acc = jnp.dot(_load_rows(a_ref), w_ref[...], preferred_element_type=F32)
    o_ref[...] = acc.reshape(o_ref.shape).astype(o_ref.dtype)


def _mm_rope_body(a_ref, w_ref, cos_ref, sa_ref, sb_ref, o_ref):
    for bs, ts in _bands(a_ref):
        acc = jnp.dot(_band(a_ref, bs, ts), w_ref[...], preferred_element_type=F32)
        _store_band(o_ref, bs, ts, _rope_rows(acc, cos_ref, sa_ref, sb_ref, bs, ts))


def _mm_mod_kvg_body(x_ref, w_ref, wg_ref, sc_ref, sh_ref, cos_ref, sa_ref, sb_ref, u_ref, k_ref, v_ref, g_ref):
    u_ref[...] = (x_ref[...] * (1.0 + sc_ref[...]) + sh_ref[...]).astype(u_ref.dtype)
    for bs, ts in _bands(x_ref):
        a = _band(u_ref, bs, ts)
        acc = jnp.dot(a, w_ref[...], preferred_element_type=F32)
        _store_band(k_ref, bs, ts, _rope_rows(acc[:, :KV_WIDTH], cos_ref, sa_ref, sb_ref, bs, ts))
        _store_band(v_ref, bs, ts, acc[:, KV_WIDTH:])
        _store_band(g_ref, bs, ts, jnp.dot(a, wg_ref[...], preferred_element_type=F32))


def _gated(acc, g_ref, bs, ts):
    return jax.nn.sigmoid(_band(g_ref, bs, ts).astype(F32)) * acc


def _mm_gate_first_body(a_ref, w_ref, g_ref, o_ref):
    for bs, ts in _bands(a_ref):
        acc = jnp.dot(_band(a_ref, bs, ts), w_ref[...], preferred_element_type=F32)
        _store_band(o_ref, bs, ts, _gated(acc, g_ref, bs, ts))


def _mm_gate_second_body(a_ref, w_ref, g_ref, t_ref, o_ref):
    for bs, ts in _bands(a_ref):
        acc = jnp.dot(_band(a_ref, bs, ts), w_ref[...], preferred_element_type=F32)
        _store_band(o_ref, bs, ts, _band(t_ref, bs, ts) + _gated(acc, g_ref, bs, ts))


def _mm_residual_body(a_ref, w_ref, x_ref, g_ref, o_ref):
    acc = jnp.dot(_load_rows(a_ref), w_ref[...], preferred_element_type=F32)
    o_ref[...] = ALPHA * x_ref[...] + g_ref[...] * acc.reshape(o_ref.shape)


def _mm_swiglu_body(a_ref, wg_ref, wu_ref, o_ref):
    for bs, ts in _bands(a_ref):
        a = _band(a_ref, bs, ts)
        gate = jnp.dot(a, wg_ref[...], preferred_element_type=F32)
        up = jnp.dot(a, wu_ref[...], preferred_element_type=F32)
        _store_band(o_ref, bs, ts, gate * jax.nn.sigmoid(gate) * up)


def _act_spec(bb, tt, width, col=None):
    if col is None:
        return pl.BlockSpec((bb, tt, width), lambda b, t, j: (b, t, j))
    return pl.BlockSpec((bb, tt, width), lambda b, t, j: (b, t, j + col))


def _matmul(body, a, weights, extras, extra_specs, outs, *, n_tiles, rows=ROW_TILE, name):
    batch, steps, k = a.shape
    bb, tt = _row_blocks(batch, steps, rows)
    grid = (batch // bb, steps // tt, n_tiles)
    in_specs = [pl.BlockSpec((bb, tt, k), lambda b, t, j: (b, t, 0))]
    in_specs += [pl.BlockSpec((k, cols), functools.partial(lambda b, t, j, col: (0, col(j)), col=col))
                 for _, cols, col in weights]
    in_specs += [spec(bb, tt) for spec in extra_specs]
    weights = [w for w, _, _ in weights]
    out_specs = [pl.BlockSpec((bb, tt, cols_tile), lambda b, t, j: (b, t, j)) for _, _, cols_tile in outs]
    out_shape = [jax.ShapeDtypeStruct((batch, steps, cols), dt) for cols, dt, _ in outs]
    single = len(outs) == 1
    return pl.pallas_call(
        body,
        grid=grid,
        in_specs=in_specs,
        out_specs=out_specs[0] if single else out_specs,
        out_shape=out_shape[0] if single else out_shape,
        compiler_params=_params(("parallel", "parallel", "parallel")),
        name=name,
    )(a, *weights, *extras)


def _matmul_casting(body, a, casts, extras, extra_specs, outs, *, n_tiles, name, n_side=0):
    batch, steps, k = a.shape

    def per_tile(spec):
        return pl.BlockSpec(spec.block_shape, lambda j: spec.index_map(0, 0, j))

    in_specs = [pl.BlockSpec((batch, steps, k), lambda j: (0, 0, 0), pipeline_mode=pl.Buffered(1))]
    sources = []
    for srcs, *_ in casts:
        for arr, shape, index in srcs:
            in_specs.append(pl.BlockSpec(shape, index))
            sources.append(arr)
    in_specs += [per_tile(spec(batch, steps)) for spec in extra_specs]
    out_specs = [pl.BlockSpec((batch, steps, cols_tile), lambda j: (0, 0, j)) for _, _, cols_tile in outs]
    out_shape = [jax.ShapeDtypeStruct((batch, steps, cols), dt) for cols, dt, _ in outs]
    out_specs += [pl.BlockSpec(block, index) for _, _, _, block, index in casts]
    out_shape += [jax.ShapeDtypeStruct(shape, BF16) for _, _, shape, _, _ in casts]
    counts = [len(srcs) for srcs, *_ in casts]
    transforms = [tr for _, tr, *_ in casts]
    n_used = len(casts) - n_side
    n_extra, n_out = len(extras), len(outs)

    def casting_body(a_ref, *refs):
        src_refs, pos = [], 0
        for n in counts:
            src_refs.append(refs[pos:pos + n])
            pos += n
        extra_refs = refs[pos:pos + n_extra]
        out_refs = refs[pos + n_extra:pos + n_extra + n_out]
        w_refs = refs[pos + n_extra + n_out:]
        for w_ref, srcs, transform in zip(w_refs, src_refs, transforms):
            w_ref[...] = transform(*[s[...] for s in srcs]).astype(w_ref.dtype)
        body(a_ref, *w_refs[:n_used], *extra_refs, *out_refs)

    return pl.pallas_call(
        casting_body,
        grid=(n_tiles,),
        in_specs=in_specs,
        out_specs=out_specs,
        out_shape=out_shape,
        compiler_params=_params(("arbitrary",)),
        name=name,
    )(a, *sources, *extras)


def _rope_specs():
    return [lambda bb, tt: pl.BlockSpec((tt, LANES), lambda b, t, j: (t, 0))] * 3


def _attn_body(q_ref, kc_ref, kp_ref, vc_ref, vp_ref, sink_ref, o_ref, st_scr, lhs_scr, *, mask_first):
    bb, tq, _ = q_ref.shape
    n_chunks = tq // CHUNK
    n_keys = WINDOW + CHUNK
    i = pl.program_id(1)
    h = pl.program_id(2)

    lane = lax.broadcasted_iota(jnp.int32, (1, LANES), 1)
    keep = (lane // HEAD_DIM) == (h % 2)
    sink = sink_ref[0]
    lane_head = lax.broadcasted_iota(jnp.int32, (CHUNK, 2 * LANES), 1) // HEAD_DIM
    out_low = lax.broadcasted_iota(jnp.int32, (CHUNK, LANES), 1) < HEAD_DIM
    key_row = lax.broadcasted_iota(jnp.int32, (n_keys, 1), 0)
    ones = jnp.ones((16, n_keys), BF16)

    def both_halves(cur_ref, prev_ref, bi):
        pair = jnp.concatenate([prev_ref[bi].astype(F32), cur_ref[bi].astype(F32)], axis=0)
        swapped = pltpu.roll(pair, HEAD_DIM, axis=1)
        return jnp.where(keep, pair, swapped)

    def scores(bi, j, k4):
        q = q_ref[bi, j * CHUNK:(j + 1) * CHUNK, :] * (HEAD_DIM ** -0.5)
        parts = []
        for cg in range(GROUP // 4):
            qc = q[:, cg * 2 * LANES:(cg + 1) * 2 * LANES]
            for r in range(4):
                parts.append(jnp.where(lane_head == r, qc, jnp.zeros_like(qc)))
        qrows = jnp.concatenate(parts, axis=0)
        keys = k4[j * CHUNK:j * CHUNK + n_keys]
        st = lax.dot_general(keys, qrows, (((1,), (1,)), ((), ())), preferred_element_type=F32)
        if mask_first and j < WINDOW // CHUNK:
            first_valid = WINDOW - (i * tq + j * CHUNK)
            st = jnp.where(key_row >= first_valid, st, -jnp.inf)
        return st

    def values_t(j, v2):
        vwin = v2[j * CHUNK:j * CHUNK + n_keys]
        head = vwin[:LANES].T
        tail = vwin[LANES:]
        tail = jnp.concatenate([tail, tail], axis=0).T
        vt = jnp.concatenate([head[:HEAD_DIM], tail[:HEAD_DIM, :CHUNK]], axis=1).astype(BF16)
        return jnp.concatenate([vt, ones], axis=0)

    def finish(st, lhs):
        m = jnp.maximum(jnp.max(st, axis=0, keepdims=True), sink)
        pt = jnp.exp(st - m).astype(BF16)
        ot = jnp.dot(lhs, pt, preferred_element_type=F32)
        den = ot[HEAD_DIM:HEAD_DIM + 1] + jnp.exp(sink - m)
        return ot[:HEAD_DIM] / den

    def emit(bi, j, on):
        rows = pl.ds(pl.multiple_of(j * CHUNK, CHUNK), CHUNK)
        for pr in range(GROUP // 2):
            x = on[:, pr * LANES:(pr + 1) * LANES]
            xt = jnp.concatenate([x, x], axis=0).T
            o_ref[bi, rows, pr * LANES:(pr + 1) * LANES] = (
                jnp.where(out_low, xt[:CHUNK], xt[CHUNK:]).astype(o_ref.dtype))

    u = 0
    for bi in range(bb):
        k2 = both_halves(kc_ref, kp_ref, bi)
        v2 = both_halves(vc_ref, vp_ref, bi)
        k4 = jnp.concatenate([k2, k2], axis=1).astype(BF16)
        for j in range(n_chunks):
            st_scr[u] = scores(bi, j, k4)
            lhs_scr[u] = values_t(j, v2)
            u += 1

    def unit(u, carry):
        emit(u // n_chunks, u % n_chunks, finish(st_scr[u], lhs_scr[u]))
        return carry

    lax.fori_loop(0, bb * n_chunks, unit, 0, unroll=ATTN_UNROLL)


def _attention(q, k_cur, v_cur, k_prev, v_prev, sink_tab, *, bb, tq, mask_first):
    batch, steps, _ = q.shape
    qw = GROUP * HEAD_DIM
    same_array = k_prev is k_cur
    per_tile = tq // WINDOW

    def prev_map(b, i, h):
        if same_array:
            return (b, jnp.maximum(i * per_tile - 1, 0), h // 2)
        return (b, 0, h // 2)

    return pl.pallas_call(
        functools.partial(_attn_body, mask_first=mask_first),
        grid=(batch // bb, steps // tq, N_KV_HEADS),
        in_specs=[pl.BlockSpec((bb, tq, qw), lambda b, i, h: (b, i, h)),
                  pl.BlockSpec((bb, tq, LANES), lambda b, i, h: (b, i, h // 2)),
                  pl.BlockSpec((bb, WINDOW, LANES), prev_map),
                  pl.BlockSpec((bb, tq, LANES), lambda b, i, h: (b, i, h // 2)),
                  pl.BlockSpec((bb, WINDOW, LANES), prev_map),
                  pl.BlockSpec((1, 1, GROUP * CHUNK), lambda b, i, h: (h, 0, 0))],
        out_specs=pl.BlockSpec((bb, tq, qw), lambda b, i, h: (b, i, h)),
        out_shape=jax.ShapeDtypeStruct((batch, steps, ATT_WIDTH), BF16),
        scratch_shapes=[pltpu.VMEM((bb * tq // CHUNK, WINDOW + CHUNK, GROUP * CHUNK), F32),
                        pltpu.VMEM((bb * tq // CHUNK, HEAD_DIM + 16, WINDOW + CHUNK), BF16)],
        compiler_params=_params(("parallel", "parallel", "parallel")),
        name="attention",
    )(q, k_cur, k_prev, v_cur, v_prev, sink_tab)


def _log_sigmoid(x):
    return jnp.minimum(x, 0.0) - jnp.log1p(jnp.exp(-jnp.abs(x)))


def _mlstm_body(q_ref, k_ref, v_ref, o_ref, g_ref, bias_ref, c0_ref, n0_ref, m0_ref,
                y_ref, c_ref, n_ref, m_ref, *, single_step):
    if single_step:
        c_in, n_in, m_in = c0_ref, n0_ref, m0_ref
    else:
        c_in, n_in, m_in = c_ref, n_ref, m_ref

        @pl.when(pl.program_id(1) == 0)
        def _():
            c_ref[...] = c0_ref[...]
            n_ref[...] = n0_ref[...]
            m_ref[...] = m0_ref[...]

    steps = q_ref.shape[1]
    g = g_ref[0] + bias_ref[...]
    li = g[:, :LANES]
    lf = _log_sigmoid(g[:, LANES:])
    row = lax.broadcasted_iota(jnp.int32, (steps, steps), 0)
    col = lax.broadcasted_iota(jnp.int32, (steps, steps), 1)
    causal = col <= row
    b = jnp.dot(causal.astype(F32), lf, preferred_element_type=F32,
                precision=lax.Precision.HIGHEST)
    r = li - b
    m_prev = m_in[0]
    b_last = b[steps - 1:steps, :]
    m_new = b_last + jnp.maximum(m_prev, jnp.max(r, axis=0, keepdims=True))
    w_state = jnp.exp(r + (b_last - m_new))
    decay = jnp.exp(b_last + m_prev - m_new)
    r_rows = r.T

    lane = lax.broadcasted_iota(jnp.int32, (1, LANES), 1)
    head_row = lax.broadcasted_iota(jnp.int32, (LANES, 1), 0)

    def one_head(h, carry):
        pick = lane == h

        def column(mat):
            return jnp.sum(jnp.where(pick, mat, 0.0), axis=1, keepdims=True)

        r_row = jnp.sum(jnp.where(head_row == h, r_rows, 0.0), axis=0, keepdims=True)
        dmat = jnp.where(causal, jnp.broadcast_to(r_row, (steps, steps)), -jnp.inf)
        m_h = column(m_prev)
        gmax = jnp.maximum(jnp.max(dmat, axis=-1, keepdims=True), m_h)
        w_intra = jnp.exp(dmat - gmax)
        w_inter = jnp.exp(m_h - gmax)
        inv_floor = jnp.exp(-(column(b) + gmax))

        qk_lanes = pl.ds(pl.multiple_of(h * M_QK_DIM, M_QK_DIM), M_QK_DIM)
        v_lanes = pl.ds(pl.multiple_of(h * M_V_DIM, M_V_DIM), M_V_DIM)
        q = q_ref[0, :, qk_lanes] * (M_QK_DIM ** -0.5)
        k = k_ref[0, :, qk_lanes]
        v = v_ref[0, :, v_lanes]
        c_old = c_in[0, h]
        n_old = n_in[0, pl.ds(h, 1), :]

        qk = lax.dot_general(q, k, (((1,), (1,)), ((), ())), preferred_element_type=F32)
        s = qk * w_intra
        intra = jnp.dot(s.astype(BF16), v, preferred_element_type=F32)
        inter = lax.dot_general(q, c_old.astype(BF16), (((1,), (1,)), ((), ())),
                                preferred_element_type=F32)
        nq = jnp.sum(q.astype(F32) * n_old, axis=-1, keepdims=True)
        num = intra + w_inter * inter
        den = jnp.sum(s, axis=-1, keepdims=True) + w_inter * nq
        hval = num / jnp.maximum(jnp.abs(den), inv_floor)
        og = jax.nn.sigmoid(o_ref[0, :, v_lanes].astype(F32))
        y_ref[0, :, v_lanes] = (og * hval).astype(y_ref.dtype)

        ws = column(w_state)
        vw = (v.astype(F32) * ws).astype(BF16)
        dc = lax.dot_general(vw, k, (((0,), (0,)), ((), ())), preferred_element_type=F32)
        dec = column(decay)
        c_ref[0, h] = dec * c_old + dc
        n_ref[0, pl.ds(h, 1), :] = dec * n_old + jnp.sum(k.astype(F32) * ws, axis=0, keepdims=True)
        return carry

    lax.fori_loop(0, M_HEADS, one_head, 0, unroll=HEAD_UNROLL)

    m_ref[0] = m_new


def _mlstm(za, zb, gates, bias, c0, n0, m0):
    batch, steps, _ = za.shape
    rows = min(steps, MLSTM_ROWS)
    state_specs = [pl.BlockSpec((1, M_HEADS, M_V_DIM, M_QK_DIM), lambda b, c: (b, 0, 0, 0)),
                   pl.BlockSpec((1, M_HEADS, M_QK_DIM), lambda b, c: (b, 0, 0)),
                   pl.BlockSpec((1, 1, LANES), lambda b, c: (b, 0, 0))]
    return pl.pallas_call(
        functools.partial(_mlstm_body, single_step=steps == rows),
        grid=(batch, steps // rows),
        in_specs=[pl.BlockSpec((1, rows, M_QK_WIDTH), lambda b, c: (b, c, 0)),
                  pl.BlockSpec((1, rows, M_QK_WIDTH), lambda b, c: (b, c, 1)),
                  pl.BlockSpec((1, rows, M_V_WIDTH), lambda b, c: (b, c, 1)),
                  pl.BlockSpec((1, rows, M_V_WIDTH), lambda b, c: (b, c, 0)),
                  pl.BlockSpec((1, rows, GATE_PAD), lambda b, c: (b, c, 0)),
                  pl.BlockSpec((1, GATE_PAD), lambda b, c: (0, 0))] + state_specs,
        out_specs=[pl.BlockSpec((1, rows, M_V_WIDTH), lambda b, c: (b, c, 0))] + state_specs,
        out_shape=[jax.ShapeDtypeStruct((batch, steps, M_V_WIDTH), BF16),
                   jax.ShapeDtypeStruct(c0.shape, F32),
                   jax.ShapeDtypeStruct(n0.shape, F32),
                   jax.ShapeDtypeStruct(m0.shape, F32)],
        compiler_params=_params(("parallel", "arbitrary")),
        name="mlstm",
    )(za, za, za, zb, gates, bias, c0, n0, m0)


def _ln_rows(x, g, b):
    mu = jnp.mean(x, axis=-1, keepdims=True)
    xc = x - mu
    var = jnp.mean(xc * xc, axis=-1, keepdims=True)
    return xc * lax.rsqrt(var + LN_EPS) * g + b


def _ln_mod_body(r_ref, g_ref, b_ref, sc_ref, sh_ref, x_ref, u_ref):
    y = _ln_rows(r_ref[...], g_ref[...], b_ref[...])
    x_ref[...] = y
    u_ref[...] = (y * (1.0 + sc_ref[...]) + sh_ref[...]).astype(u_ref.dtype)


def _ln_body(r_ref, g_ref, b_ref, x_ref):
    x_ref[...] = _ln_rows(r_ref[...], g_ref[...], b_ref[...])


def _layer_norm(r, g, b, mod=None, sc_idx=None, sh_idx=None):
    batch, steps, d = r.shape
    bb, tt = _row_blocks(batch, steps, ROW_TILE // 2)
    act = pl.BlockSpec((bb, tt, d), lambda i, t: (i, t, 0))
    vec = pl.BlockSpec((1, 1, d), lambda i, t: (0, 0, 0))
    g3, b3 = g.reshape(1, 1, d), b.reshape(1, 1, d)
    if mod is None:
        return pl.pallas_call(
            _ln_body, grid=(batch // bb, steps // tt),
            in_specs=[act, vec, vec], out_specs=act,
            out_shape=jax.ShapeDtypeStruct(r.shape, F32),
            compiler_params=_params(("parallel", "parallel")), name="layer_norm",
        )(r, g3, b3)
    return pl.pallas_call(
        _ln_mod_body, grid=(batch // bb, steps // tt),
        in_specs=[act, vec, vec,
                  pl.BlockSpec((bb, 1, d), lambda i, t: (i, 0, sc_idx)),
                  pl.BlockSpec((bb, 1, d), lambda i, t: (i, 0, sh_idx))],
        out_specs=[act, act],
        out_shape=[jax.ShapeDtypeStruct(r.shape, F32), jax.ShapeDtypeStruct(r.shape, BF16)],
        compiler_params=_params(("parallel", "parallel")), name="layer_norm_modulate",
    )(r, g3, b3, mod, mod)


SH1, SC1, GT1, SH2, SC2, GT2 = range(6)


def _rope_tables(pos):
    half = ROT_DIM // 2
    inv = ROPE_THETA ** (-2.0 * jnp.arange(half, dtype=F32) / ROT_DIM)
    ang = pos.astype(F32)[:, None] * inv[None, :]
    cos, sin = jnp.cos(ang), jnp.sin(ang)
    zeros = jnp.zeros_like(cos)
    rest = HEAD_DIM - ROT_DIM
    steps = pos.shape[0]
    one_head = lambda first, second, fill: jnp.concatenate(
        [first, second, jnp.full((steps, rest), fill, F32)], axis=1)
    cos_t = one_head(cos, cos, 1.0)
    up_t = one_head(-sin, zeros, 0.0)
    dn_t = one_head(zeros, sin, 0.0)
    reps = LANES // HEAD_DIM
    return tuple(jnp.tile(t, (1, reps)) for t in (cos_t, up_t, dn_t))


CAST_TILE = 512
FFN_TILE = 256
MERGE_CAST_TILE = 256
GATE_COLS = 2 * M_HEADS
PROJ_HEAD = ATT_WIDTH + 2 * KV_WIDTH + 2 * M_QK_WIDTH + M_V_WIDTH


def _group_layer(x, mod, pos, wts, raw, k_prev, v_prev, state, *, attn_bb, attn_tq, mask_first):
    batch, steps, d = x.shape
    ropes = _rope_tables(pos)
    casting = raw is not None
    tile = CAST_TILE if casting else COL_TILE

    def dense(name, body, act, keys, casts, extras, extra_specs, outs, n_cols, wtile, side_keys=(), rows=ROW_TILE):
        n_tiles = -(-n_cols // wtile)
        if not casting:
            weights = [(wts[key], wtile, lambda j: j) for key in keys]
            return _matmul(body, act, weights, extras, extra_specs, outs, n_tiles=n_tiles, rows=rows, name=name)
        res = _matmul_casting(body, act, casts, extras, extra_specs, outs, n_tiles=n_tiles, name=name,
                              n_side=len(side_keys))
        for key, w in zip(list(keys) + list(side_keys), res[len(outs):]):
            wts[key] = w
        return res[0] if len(outs) == 1 else res[:len(outs)]

    def proj_rows(first, total):
        srcs = [(raw["w_in_t"], (1, tile, d), lambda j: (0, first // tile + j, 0))]
        return [(srcs, lambda w: w[0].T, (d, total), (d, tile), lambda j: (0, j))]

    def proj_rows_shifted(first, total):
        base = first - GATE_COLS
        srcs = [(raw["w_in_t"], (1, tile, d), lambda j: (0, base // tile + j, 0)),
                (raw["w_in_t"], (1, GATE_COLS, d), lambda j: (0, (base + tile * (j + 1)) // GATE_COLS, 0))]

        def transform(main, nxt):
            rows = jnp.concatenate([main[0], nxt[0]], axis=0)
            return rows[GATE_COLS:GATE_COLS + tile].T

        return [(srcs, transform, (d, total), (d, tile), lambda j: (0, j))]

    def columns(key, first, total, width):
        srcs = [(raw[key], (1, d, width), lambda j: (0, 0, first // width + j))]
        return (srcs, lambda w: w[0], (d, total), (d, width), lambda j: (0, j))

    def mod_full(idx):
        return lambda bb, tt: pl.BlockSpec((bb, 1, d), lambda b, t_, j: (b, 0, idx))

    u, ka, va, gates = _matmul(
        _mm_mod_kvg_body, x, [(wts["w_kv"], 2 * KV_WIDTH, lambda j: 0), (wts["w_g"], GATE_PAD, lambda j: 0)],
        [mod, mod, *ropes], [mod_full(SC1), mod_full(SH1)] + _rope_specs(),
        [(d, BF16, d), (KV_WIDTH, F32, KV_WIDTH), (KV_WIDTH, F32, KV_WIDTH), (GATE_PAD, F32, GATE_PAD)],
        n_tiles=1, rows=ROW_TILE // 2, name="proj_kvg")

    mem_first = ATT_WIDTH + 2 * KV_WIDTH
    za_width = 2 * M_QK_WIDTH + M_V_WIDTH
    zb_width = M_V_WIDTH + 2 * d
    qa = dense("proj_q", _mm_rope_body, u, ["w_qa"], casting and proj_rows(0, ATT_WIDTH),
               ropes, _rope_specs(), [(ATT_WIDTH, BF16, tile)], ATT_WIDTH, tile)
    za = dense("proj_mem", _mm_plain_body, u, ["w_mem"], casting and proj_rows(mem_first, za_width),
               [], [], [(za_width, BF16, tile)], za_width, tile)
    zb = dense("proj_gates", _mm_plain_body, u, ["w_gates"],
               casting and proj_rows_shifted(PROJ_HEAD + GATE_COLS, zb_width),
               [], [], [(zb_width, BF16, tile)], zb_width, tile)

    if k_prev is None:
        k_prev, v_prev = ka, va
    ya = _attention(qa, ka, va, k_prev, v_prev, wts["sink_tab"], bb=attn_bb, tq=attn_tq, mask_first=mask_first)

    yb, c_new, n_new, m_new = _mlstm(za, zb, gates, wts["gate_bias"], *state)

    mtile = MERGE_CAST_TILE if casting else COL_TILE
    ga_col = M_V_WIDTH // mtile
    gb_col = ga_col + d // mtile
    t = dense("merge_a", _mm_gate_first_body, ya, ["w_up_a"], casting and [columns("w_up_a", 0, d, mtile)],
              [zb], [lambda bb, tt: _act_spec(bb, tt, mtile, ga_col)], [(d, F32, mtile)], d, mtile)
    merged = dense("merge_b", _mm_gate_second_body, yb, ["w_up_b"], casting and [columns("w_up_b", 0, d, mtile)],
                   [zb, t], [lambda bb, tt: _act_spec(bb, tt, mtile, gb_col),
                             lambda bb, tt: _act_spec(bb, tt, mtile)], [(d, BF16, mtile)], d, mtile)

    def mod_spec(idx, width):
        per_vec = d // width
        return lambda bb, tt: pl.BlockSpec((bb, 1, width), lambda b, t_, j: (b, 0, idx * per_vec + j))

    r1 = dense("out_proj", _mm_residual_body, merged, ["w_o"], casting and [columns("w_o", 0, d, mtile)],
               [x, mod], [lambda bb, tt: _act_spec(bb, tt, mtile), mod_spec(GT1, mtile)],
               [(d, F32, mtile)], d, mtile)
    x1, u2 = _layer_norm(r1, wts["ln1_g"], wts["ln1_b"], mod, SC2, SH2)

    ff_tile = FFN_TILE
    ffn_casts = casting and [
        columns("w_ffn_in", 0, D_FF, ff_tile), columns("w_ffn_in", D_FF, D_FF, ff_tile),
        ([(raw["w_ffn_out"], (1, ff_tile, d), lambda j: (0, j, 0))], lambda w: w[0],
         (D_FF, d), (ff_tile, d), lambda j: (j, 0))]
    hidden = dense("ffn_in", _mm_swiglu_body, u2, ["w_ffn_gate", "w_ffn_up"], ffn_casts,
                   [], [], [(D_FF, BF16, ff_tile)], D_FF, ff_tile, side_keys=["w_ffn_out"] if casting else (),
                   rows=2 * ROW_TILE)

    out_tile = COL_TILE // 2
    r2 = _matmul(_mm_residual_body, hidden, [(wts["w_ffn_out"], out_tile, lambda j: j)], [x1, mod],
                 [lambda bb, tt: _act_spec(bb, tt, out_tile), mod_spec(GT2, out_tile)],
                 [(d, F32, out_tile)], n_tiles=d // out_tile, rows=ROW_TILE // 2, name="ffn_out")
    y = _layer_norm(r2, wts["ln2_g"], wts["ln2_b"])
    return y, ka, va, c_new, n_new, m_new


def _cast_body(w_ref, o_ref):
    o_ref[...] = w_ref[0].T.astype(o_ref.dtype)


def _cast_gates_body(w_ref, o_ref):
    x = w_ref[0].T
    valid = lax.broadcasted_iota(jnp.int32, (1, LANES), 1) < M_HEADS
    o_ref[:, :LANES] = jnp.where(valid, x, 0.0).astype(o_ref.dtype)
    f_first = pltpu.roll(x, LANES - M_HEADS, axis=1)
    o_ref[:, LANES:] = jnp.where(valid, f_first, 0.0).astype(o_ref.dtype)


def _cast_kv_gates(w_in_t):
    _, _, k = w_in_t.shape
    params = _params(("parallel",))
    first = ATT_WIDTH // CAST_TILE
    w_kv = pl.pallas_call(
        _cast_body, grid=(2 * KV_WIDTH // CAST_TILE,),
        in_specs=[pl.BlockSpec((1, CAST_TILE, k), lambda j: (0, first + j, 0))],
        out_specs=pl.BlockSpec((k, CAST_TILE), lambda j: (0, j)),
        out_shape=jax.ShapeDtypeStruct((k, 2 * KV_WIDTH), BF16),
        compiler_params=params, name="cast_proj_kv",
    )(w_in_t)
    w_g = pl.pallas_call(
        _cast_gates_body, grid=(1,),
        in_specs=[pl.BlockSpec((1, LANES, k), lambda j: (0, PROJ_HEAD // LANES, 0))],
        out_specs=pl.BlockSpec((k, GATE_PAD), lambda j: (0, 0)),
        out_shape=jax.ShapeDtypeStruct((k, GATE_PAD), BF16),
        compiler_params=params, name="cast_proj_gates",
    )(w_in_t)
    return w_kv, w_g


def kernel(x_prompt, x_sample, cache_k_win, cache_v_win, state_C, state_n, state_m, c_prompt, c_sample, w_ada, b_ada, w_in, b_if, attn_sinks, w_up_a, w_up_b, w_o, ln1_g, ln1_b, w_ffn_in, w_ffn_out, ln2_g, ln2_b):
    bp, sp, d = x_prompt.shape
    bs, ts, _ = x_sample.shape
    keep = cache_k_win.shape[2]
    assert w_in.shape[0] == DEPTH == 1
    l = 0

    raw = {"w_in_t": jnp.swapaxes(w_in, 1, 2), "w_up_a": w_up_a, "w_up_b": w_up_b, "w_o": w_o,
           "w_ffn_in": w_ffn_in, "w_ffn_out": w_ffn_out}
    w_kv, w_g = _cast_kv_gates(raw["w_in_t"])
    pad_bias = jnp.zeros((LANES - M_HEADS,), F32)
    wts = {
        "w_kv": w_kv, "w_g": w_g,
        "gate_bias": jnp.concatenate([b_if[l, :M_HEADS], pad_bias, b_if[l, M_HEADS:], pad_bias]).reshape(1, GATE_PAD),
        "sink_tab": jnp.broadcast_to(attn_sinks[l].astype(F32).reshape(N_KV_HEADS, 1, GROUP, 1),
                                     (N_KV_HEADS, 1, GROUP, CHUNK)).reshape(N_KV_HEADS, 1, GROUP * CHUNK),
        "ln1_g": ln1_g[l], "ln1_b": ln1_b[l], "ln2_g": ln2_g[l], "ln2_b": ln2_b[l],
    }

    c_all = jnp.concatenate([c_prompt, c_sample], axis=0)
    rows = c_all.shape[0]
    rows_pad = -(-rows // 8) * 8
    c_all = jnp.pad(c_all, ((0, rows_pad - rows), (0, 0)))
    mod = _adaln(c_all, w_ada[l], b_ada[l])
    mod_p = mod[:bp].reshape(bp, 1, 6 * d)
    mod_s = mod[bp:bp + bs].reshape(bs, 1, 6 * d)

    def pad_lanes(m):
        return jnp.pad(m, ((0, 0), (0, LANES - M_HEADS))).reshape(m.shape[0], 1, LANES)

    ck = cache_k_win[l].reshape(bs, keep, KV_WIDTH)
    cv = cache_v_win[l].reshape(bs, keep, KV_WIDTH)
    state_s = (state_C[l], state_n[l], pad_lanes(state_m[l]))
    ys, kas, vas, cs, ns, ms = _group_layer(
        x_sample, mod_s, PAST_LEN + jnp.arange(ts), wts, raw, ck, cv, state_s,
        attn_bb=min(bs, ATTN_ROWS // ts), attn_tq=ts, mask_first=False)

    state_p = (jnp.zeros((bp, M_HEADS, M_V_DIM, M_QK_DIM), F32),
               jnp.zeros((bp, M_HEADS, M_QK_DIM), F32),
               pad_lanes(jnp.full((bp, M_HEADS), M_INIT, F32)))
    yp, kap, vap, cp, np_, mp = _group_layer(
        x_prompt, mod_p, jnp.arange(sp), wts, None, None, None, state_p,
        attn_bb=1, attn_tq=min(ATTN_ROWS, sp), mask_first=True)

    def heads(a):
        return a.reshape(a.shape[0], a.shape[1], N_KV_HEADS, HEAD_DIM)

    kws = jnp.concatenate([ck, kas], axis=1)[:, -keep:]
    vws = jnp.concatenate([cv, vas], axis=1)[:, -keep:]
    return (yp, ys,
            heads(kap[:, -keep:])[None], heads(vap[:, -keep:])[None],
            cp[None], np_[None], mp[:, 0, :M_HEADS][None],
            heads(kws)[None], heads(vws)[None],
            cs[None], ns[None], ms[:, 0, :M_HEADS][None])
```
